```python
import math
import jax
import jax.numpy as jnp
from jax import lax
import numpy as np

D_MODEL = 1024
BATCH = 8
SEQ = 4096
DEPTH = 4

CTX_LEN = 256
GRID_W = 64
HEAD_DIM = 64
DN_HEADS = 4
GA_HEADS = 4
GA_KV_HEADS = 2
WA_HEADS = 4
WA_KV_HEADS = 2
FT_GROUPS = 4
FT_GROUP_DIM = 64
D_MIX = (DN_HEADS + GA_HEADS + WA_HEADS) * HEAD_DIM + FT_GROUPS * FT_GROUP_DIM
DN_DIM = DN_HEADS * HEAD_DIM
DN_COLS = 4 * DN_DIM + 4 * DN_HEADS
GA_COLS = (GA_HEADS + 2 * GA_KV_HEADS) * HEAD_DIM
WA_COLS = (WA_HEADS + 2 * WA_KV_HEADS) * HEAD_DIM
FT_COLS = FT_GROUPS * FT_GROUP_DIM
IN_COLS = DN_COLS + GA_COLS + WA_COLS + FT_COLS
CONV_W = 3
DN_CHUNK = 64
Q_BLOCK = 128
WINDOW = 128
ROPE_THETA = 10000.0
FFN_HIDDEN = 256 * math.ceil(8 * D_MODEL / (3 * 256))
N_MOD = 6
EPS = 1e-6
NEG_INF = -1e30

kernel_name = 'hybrid_dit_deltanet_gqa_window_fourier'


def _split(t, sizes):
    return jnp.split(t, [int(s) for s in np.cumsum(sizes)[:-1]], axis=-1)


def rms_norm(t, gain):
    tf = t.astype(jnp.float32)
    y = tf * lax.rsqrt(jnp.mean(tf * tf, -1, keepdims=True) + EPS)
    return (y * gain.astype(jnp.float32)).astype(t.dtype)


def l2_normalize(t):
    return t * lax.rsqrt(jnp.sum(t * t, -1, keepdims=True) + EPS)


def modulate(h, shift, scale):
    return h * (1 + scale) + shift


def swiglu(h, w_gate, w_up, w_down):
    return (jax.nn.silu(h @ w_gate) * (h @ w_up)) @ w_down


def axial_rope(n_tokens):
    rows = n_tokens // GRID_W
    row = jnp.repeat(jnp.arange(rows, dtype=jnp.float32), GRID_W)
    col = jnp.tile(jnp.arange(GRID_W, dtype=jnp.float32), rows)
    n_freq = HEAD_DIM // 4
    inv_freq = ROPE_THETA ** (-jnp.arange(n_freq, dtype=jnp.float32) / n_freq)
    ang = jnp.concatenate([row[:, None] * inv_freq, col[:, None] * inv_freq], -1)
    return jnp.cos(ang), jnp.sin(ang)


def apply_rope(t, cos, sin):
    tf = t.astype(jnp.float32)
    t1, t2 = tf[..., :HEAD_DIM // 2], tf[..., HEAD_DIM // 2:]
    cs, sn = cos[:, None, :], sin[:, None, :]
    return jnp.concatenate([t1 * cs - t2 * sn, t1 * sn + t2 * cs], -1).astype(t.dtype)


def short_conv(t, w):
    r = CONV_W // 2
    n = t.shape[1]
    tp = jnp.pad(t, ((0, 0), (r, r), (0, 0)))
    out = tp[:, 0:n] * w[0]
    for j in range(1, CONV_W):
        out = out + tp[:, j:j + n] * w[j]
    return out


def deltanet_prep(p, conv_w, A_log, dt_bias):
    B, S, _ = p.shape
    qkv, z, a, b = _split(p, [3 * DN_DIM, DN_DIM, 2 * DN_HEADS, 2 * DN_HEADS])
    qkv = jax.nn.silu(short_conv(qkv, conv_w)).astype(jnp.float32)
    q, k, v = [t.reshape(B, S, DN_HEADS, HEAD_DIM) for t in jnp.split(qkv, 3, -1)]
    q = l2_normalize(q) * HEAD_DIM ** -0.5
    k = l2_normalize(k)
    a = a.astype(jnp.float32).reshape(B, S, 2, DN_HEADS)
    b = b.astype(jnp.float32).reshape(B, S, 2, DN_HEADS)
    g = -jnp.exp(A_log.astype(jnp.float32)) * jax.nn.softplus(a + dt_bias.astype(jnp.float32))
    beta = jax.nn.sigmoid(b)
    return q, k, v, g, beta, z


def gated_delta_chunked(q, k, v, g, beta, state0):
    B, S, H, dk = q.shape
    dv = v.shape[-1]
    C = DN_CHUNK
    N = S // C

    def chunks(t):
        t = t.reshape((B, N, C, H) + t.shape[3:])
        return jnp.moveaxis(t, (1, 3), (0, 2))

    qc, kc, vc, gc, bc = [chunks(t) for t in (q, k, v, g, beta)]
    gcum = jnp.cumsum(gc, -1)
    idx = jnp.arange(C)
    incl = idx[:, None] >= idx[None, :]
    strict = idx[:, None] > idx[None, :]
    decay = jnp.exp(jnp.where(incl, gcum[..., :, None] - gcum[..., None, :], NEG_INF))
    kb = kc * bc[..., None]
    a_mat = jnp.where(strict, jnp.einsum('nbhid,nbhjd->nbhij', kb, kc) * decay, 0.0)
    rhs = jnp.concatenate([vc * bc[..., None], kb * jnp.exp(gcum)[..., None]], -1)
    sol = lax.linalg.triangular_solve(a_mat, rhs, left_side=True, lower=True, unit_diagonal=True)
    u, w = sol[..., :dv], sol[..., dv:]
    intra = jnp.where(incl, jnp.einsum('nbhid,nbhjd->nbhij', qc, kc) * decay, 0.0)
    q_dec = qc * jnp.exp(gcum)[..., None]
    k_dec = kc * jnp.exp(gcum[..., -1:] - gcum)[..., None]
    g_end = jnp.exp(gcum[..., -1])

    def step(state, xs):
        q_i, a_i, u_i, w_i, k_i, ge_i = xs
        v_new = u_i - jnp.einsum('bhcd,bhde->bhce', w_i, state)
        o_i = jnp.einsum('bhcd,bhde->bhce', q_i, state) + jnp.einsum('bhij,bhje->bhie', a_i, v_new)
        state = state * ge_i[..., None, None] + jnp.einsum('bhcd,bhce->bhde', k_i, v_new)
        return state, o_i

    state, o = lax.scan(step, state0, (q_dec, intra, u, w, k_dec, g_end))
    o = jnp.moveaxis(o, (0, 2), (1, 3)).reshape(B, S, H, dv)
    return o, state


def _flip(t, direction):
    return jnp.flip(t, 1) if direction == 1 else t


def gated_head_out(o, z, norm_g):
    B, S = o.shape[:2]
    zh = z.reshape(B, S, DN_HEADS, HEAD_DIM).astype(jnp.float32)
    return (rms_norm(o, norm_g) * jax.nn.silu(zh)).reshape(B, S, DN_DIM).astype(z.dtype)


def deltanet_mixer(p, pc, conv_w, A_log, dt_bias, norm_g, with_ctx):
    q, k, v, g, beta, z = deltanet_prep(p, conv_w, A_log, dt_bias)
    qc, kc, vc, gc, bc, zc = deltanet_prep(pc, conv_w, A_log, dt_bias)
    zero = jnp.zeros((p.shape[0], DN_HEADS, HEAD_DIM, HEAD_DIM), jnp.float32)
    o_lat, o_ctx = [], []
    for d in range(2):
        oc, state = gated_delta_chunked(_flip(qc, d), _flip(kc, d), _flip(vc, d),
                                        _flip(gc[:, :, d], d), _flip(bc[:, :, d], d), zero)
        ol, _ = gated_delta_chunked(_flip(q, d), _flip(k, d), _flip(v, d),
                                    _flip(g[:, :, d], d), _flip(beta[:, :, d], d), state)
        o_lat.append(_flip(ol, d))
        o_ctx.append(_flip(oc, d))
    y = gated_head_out(o_lat[0] + o_lat[1], z, norm_g)
    yc = gated_head_out(o_ctx[0] + o_ctx[1], zc, norm_g) if with_ctx else None
    return y, yc


def attn_heads(p, n_q, n_kv, q_gain, k_gain, rope):
    B, S, _ = p.shape
    q, k, v = _split(p, [n_q * HEAD_DIM, n_kv * HEAD_DIM, n_kv * HEAD_DIM])
    q = rms_norm(q.reshape(B, S, n_q, HEAD_DIM), q_gain)
    k = rms_norm(k.reshape(B, S, n_kv, HEAD_DIM), k_gain)
    v = v.reshape(B, S, n_kv, HEAD_DIM)
    if rope is not None:
        q = apply_rope(q, *rope)
        k = apply_rope(k, *rope)
    return q, k, v


def context_attention(qc, kc, vc, sink=None):
    B, L, Hq, d = qc.shape
    Hkv = kc.shape[2]
    G = Hq // Hkv
    qg = qc.reshape(B, L, Hkv, G, d)
    s = jnp.einsum('bqhgd,bkhd->bhgqk', qg, kc, preferred_element_type=jnp.float32) * d ** -0.5
    if sink is not None:
        sink_col = jnp.broadcast_to(sink.reshape(1, Hkv, G, 1, 1).astype(jnp.float32), s.shape[:-1] + (1,))
        s = jnp.concatenate([s, sink_col], -1)
    p = jax.nn.softmax(s, -1)[..., :L].astype(vc.dtype)
    return jnp.einsum('bhgqk,bkhd->bqhgd', p, vc).reshape(B, L, Hq * d)


def global_attention(q, k, v, kc, vc):
    B, S, Hq, d = q.shape
    Hkv = k.shape[2]
    G = Hq // Hkv
    nb = S // Q_BLOCK
    keys = jnp.concatenate([kc, k], 1)
    vals = jnp.concatenate([vc, v], 1)
    qb = jnp.moveaxis(q.reshape(B, nb, Q_BLOCK, Hkv, G, d), 1, 0)

    def one_block(qi):
        s = jnp.einsum('bqhgd,bkhd->bhgqk', qi, keys, preferred_element_type=jnp.float32) * d ** -0.5
        p = jax.nn.softmax(s, -1).astype(vals.dtype)
        return jnp.einsum('bhgqk,bkhd->bqhgd', p, vals)

    o = lax.map(one_block, qb)
    return jnp.moveaxis(o, 0, 1).reshape(B, S, Hq * d)


def window_attention(q, k, v, kc, vc, sink):
    B, S, Hq, d = q.shape
    Hkv = k.shape[2]
    G = Hq // Hkv
    L = kc.shape[1]
    nb = S // Q_BLOCK
    band = 3 * Q_BLOCK

    def bands(t):
        tp = jnp.pad(t, ((0, 0), (Q_BLOCK, Q_BLOCK), (0, 0), (0, 0))).reshape(B, nb + 2, Q_BLOCK, Hkv, d)
        return jnp.concatenate([tp[:, :-2], tp[:, 1:-1], tp[:, 2:]], axis=2)

    kb, vb = bands(k), bands(v)
    qb = q.reshape(B, nb, Q_BLOCK, Hkv, G, d)
    scale = d ** -0.5
    s_loc = jnp.einsum('bnqhgd,bnkhd->bnhgqk', qb, kb, preferred_element_type=jnp.float32) * scale
    s_ctx = jnp.einsum('bnqhgd,bkhd->bnhgqk', qb, kc, preferred_element_type=jnp.float32) * scale
    qpos = jnp.arange(nb)[:, None] * Q_BLOCK + jnp.arange(Q_BLOCK)[None]
    kpos = jnp.arange(nb)[:, None] * Q_BLOCK - Q_BLOCK + jnp.arange(band)[None]
    valid = ((jnp.abs(qpos[:, :, None] - kpos[:, None, :]) <= WINDOW)
             & (kpos[:, None, :] >= 0) & (kpos[:, None, :] < S))
    s_loc = jnp.where(valid[None, :, None, None], s_loc, NEG_INF)
    sink_col = jnp.broadcast_to(sink.reshape(1, 1, Hkv, G, 1, 1).astype(jnp.float32), s_loc.shape[:-1] + (1,))
    p = jax.nn.softmax(jnp.concatenate([s_loc, s_ctx, sink_col], -1), -1).astype(v.dtype)
    o = (jnp.einsum('bnhgqk,bnkhd->bnqhgd', p[..., :band], vb)
         + jnp.einsum('bnhgqk,bkhd->bnqhgd', p[..., band:band + L], vc))
    return o.reshape(B, S, Hq * d)


def fourier_mix(u):
    B, S, _ = u.shape
    uf = u.astype(jnp.float32).reshape(B, S, FT_GROUPS, FT_GROUP_DIM)
    y = jnp.fft.fft2(uf, axes=(1, 3), norm='ortho').real
    return y.reshape(B, S, FT_COLS).astype(u.dtype)


def hybrid_mixer(p, pc, rope, conv_w, A_log, dt_bias, dn_norm_g, ga_qn, ga_kn, wa_qn, wa_kn, sink, with_ctx):
    p_dn, p_ga, p_wa, p_ft = _split(p, [DN_COLS, GA_COLS, WA_COLS, FT_COLS])
    c_dn, c_ga, c_wa, c_ft = _split(pc, [DN_COLS, GA_COLS, WA_COLS, FT_COLS])
    y_dn, yc_dn = deltanet_mixer(p_dn, c_dn, conv_w, A_log, dt_bias, dn_norm_g, with_ctx)
    q1, k1, v1 = attn_heads(p_ga, GA_HEADS, GA_KV_HEADS, ga_qn, ga_kn, rope)
    qc1, kc1, vc1 = attn_heads(c_ga, GA_HEADS, GA_KV_HEADS, ga_qn, ga_kn, None)
    y_ga = global_attention(q1, k1, v1, kc1, vc1)
    q2, k2, v2 = attn_heads(p_wa, WA_HEADS, WA_KV_HEADS, wa_qn, wa_kn, rope)
    qc2, kc2, vc2 = attn_heads(c_wa, WA_HEADS, WA_KV_HEADS, wa_qn, wa_kn, None)
    y_wa = window_attention(q2, k2, v2, kc2, vc2, sink)
    y_ft = fourier_mix(p_ft)
    y = jnp.concatenate([y_dn, y_ga, y_wa, y_ft], -1)
    if not with_ctx:
        return y, None
    yc = jnp.concatenate([yc_dn, context_attention(qc1, kc1, vc1),
                          context_attention(qc2, kc2, vc2, sink), fourier_mix(c_ft)], -1)
    return y, yc


def setup_inputs(seed: int = 0) -> dict:
    key = jax.random.key(seed)
    ks = jax.random.split(key, 22)
    f32 = jnp.float32

    def normal(k, shape, scale):
        return jax.random.normal(k, shape, f32) * scale

    def gain(k, shape):
        return 1.0 + 0.02 * jax.random.normal(k, shape, f32)

    dt = jnp.exp(jax.random.uniform(ks[11], (DEPTH, 2, DN_HEADS), f32, math.log(1e-3), math.log(1e-1)))
    return {
        'x': normal(ks[0], (BATCH, SEQ, D_MODEL), 1.0),
        'c': normal(ks[1], (BATCH, D_MODEL), 1.0),
        'ctx': normal(ks[2], (BATCH, CTX_LEN, D_MODEL), 1.0),
        'c_ctx': normal(ks[3], (D_MODEL,), 1.0),
        'norm1_g': gain(ks[4], (DEPTH, D_MODEL)),
        'norm2_g': gain(ks[5], (DEPTH, D_MODEL)),
        'w_ada': normal(ks[6], (DEPTH, D_MODEL, N_MOD * D_MODEL), 0.5 * D_MODEL ** -0.5),
        'b_ada': normal(ks[7], (DEPTH, N_MOD * D_MODEL), 0.01),
        'w_in': normal(ks[8], (DEPTH, D_MODEL, IN_COLS), D_MODEL ** -0.5),
        'dn_conv_w': normal(ks[9], (DEPTH, CONV_W, 3 * DN_DIM), CONV_W ** -0.5),
        'dn_A_log': jnp.log(jax.random.uniform(ks[10], (DEPTH, 2, DN_HEADS), f32, 1.0, 16.0)),
        'dn_dt_bias': dt + jnp.log(-jnp.expm1(-dt)),
        'dn_norm_g': gain(ks[12], (DEPTH, HEAD_DIM)),
        'ga_q_norm': gain(ks[13], (DEPTH, HEAD_DIM)),
        'ga_k_norm': gain(ks[14], (DEPTH, HEAD_DIM)),
        'wa_q_norm': gain(ks[15], (DEPTH, HEAD_DIM)),
        'wa_k_norm': gain(ks[16], (DEPTH, HEAD_DIM)),
        'wa_sink': normal(ks[17], (DEPTH, WA_HEADS), 0.5),
        'w_out': normal(ks[18], (DEPTH, D_MIX, D_MODEL), D_MIX ** -0.5),
        'w_ffn_gate': normal(ks[19], (DEPTH, D_MODEL, FFN_HIDDEN), D_MODEL ** -0.5),
        'w_ffn_up': normal(ks[20], (DEPTH, D_MODEL, FFN_HIDDEN), D_MODEL ** -0.5),
        'w_ffn_down': normal(ks[21], (DEPTH, FFN_HIDDEN, D_MODEL), FFN_HIDDEN ** -0.5),
    }


def reference(x, c, ctx, c_ctx, norm1_g, norm2_g, w_ada, b_ada, w_in, dn_conv_w, dn_A_log, dn_dt_bias,
              dn_norm_g, ga_q_norm, ga_k_norm, wa_q_norm, wa_k_norm, wa_sink, w_out,
              w_ffn_gate, w_ffn_up, w_ffn_down):
    rope = axial_rope(x.shape[1])
    xc = ctx
    silu_c = jax.nn.silu(c)
    silu_cc = jax.nn.silu(c_ctx)
    for l in range(DEPTH):
        with_ctx = l < DEPTH - 1
        mod = (silu_c @ w_ada[l] + b_ada[l])[:, None, :]
        mod_c = silu_cc @ w_ada[l] + b_ada[l]
        sh1, sc1, gt1, sh2, sc2, gt2 = jnp.split(mod, N_MOD, -1)
        sh1c, sc1c, gt1c, sh2c, sc2c, gt2c = jnp.split(mod_c, N_MOD, -1)
        h = modulate(rms_norm(x, norm1_g[l]), sh1, sc1)
        hc = modulate(rms_norm(xc, norm1_g[l]), sh1c, sc1c)
        y, yc = hybrid_mixer(h @ w_in[l], hc @ w_in[l], rope, dn_conv_w[l], dn_A_log[l], dn_dt_bias[l],
                             dn_norm_g[l], ga_q_norm[l], ga_k_norm[l], wa_q_norm[l], wa_k_norm[l],
                             wa_sink[l], with_ctx)
        x = x + gt1 * (y @ w_out[l])
        x = x + gt2 * swiglu(modulate(rms_norm(x, norm2_g[l]), sh2, sc2),
                             w_ffn_gate[l], w_ffn_up[l], w_ffn_down[l])
        if with_ctx:
            xc = xc + gt1c * (yc @ w_out[l])
            xc = xc + gt2c * swiglu(modulate(rms_norm(xc, norm2_g[l]), sh2c, sc2c),
                                    w_ffn_gate[l], w_ffn_up[l], w_ffn_down[l])
    return x
```

```python
import functools
import math

import jax
import jax.numpy as jnp
import numpy as np
from jax import lax
from jax.experimental import pallas as pl
from jax.experimental.pallas import tpu as pltpu

F32 = jnp.float32
BF16 = jnp.bfloat16
HIGHEST = lax.Precision.HIGHEST

D_MODEL = 1024
HEAD_DIM = 64
N_HEADS = 4
N_KV = 2
GRID_W = 64
CHUNK = 64
WINDOW = 128
ROPE_THETA = 10000.0
FFN_HIDDEN = 2816
N_MOD = 6
EPS = 1e-6
NEG_INF = -1e30

LANES = 128
TM = 256
MOD_ROWS = 16
FFN_CHUNK = 256

C_QKV, C_Z, C_GA, C_WA, C_FT, C_AB, C_END = 0, 768, 1024, 1536, 2048, 2304, 2432


def _cparams(sem, vmem_mb):
    return pltpu.CompilerParams(dimension_semantics=sem, vmem_limit_bytes=vmem_mb * 1024 * 1024)


def _silu(x):
    return x * (1.0 / (1.0 + jnp.exp(-x)))


def _head_sumsq(t, ones_bd):
    return jnp.dot((t * t).astype(BF16), ones_bd, preferred_element_type=F32)


def _mod_kernel(c_ref, w_ref, b_ref, o_ref):
    a = _silu(c_ref[...])
    o_ref[0] = jnp.dot(a, w_ref[0], preferred_element_type=F32, precision=HIGHEST) + b_ref[0]


def _modulation(cc, w_ada, b_ada):
    depth = w_ada.shape[0]
    return pl.pallas_call(
        _mod_kernel,
        grid=(depth, N_MOD),
        in_specs=[pl.BlockSpec((MOD_ROWS, D_MODEL), lambda l, n: (0, 0)),
                  pl.BlockSpec((1, D_MODEL, D_MODEL), lambda l, n: (l, 0, n)),
                  pl.BlockSpec((1, 1, D_MODEL), lambda l, n: (l, 0, n))],
        out_specs=pl.BlockSpec((1, MOD_ROWS, D_MODEL), lambda l, n: (l, 0, n)),
        out_shape=jax.ShapeDtypeStruct((depth, MOD_ROWS, N_MOD * D_MODEL), F32),
        compiler_params=_cparams(("parallel", "parallel"), 32),
        name="modulation",
    )(cc, w_ada, b_ada.reshape(depth, 1, N_MOD * D_MODEL))


def _inproj_kernel(x_ref, mod_ref, g1_ref, w_ref, cos_ref, sin_ref, gains_ref, ones_ref, ftw_ref,
                   qkv_ref, z_ref, ab_ref, gaq_ref, gak_ref, gav_ref, waq_ref, wak_ref, wav_ref, ft_ref):
    x = x_ref[0]
    ms = jnp.mean(x * x, -1, keepdims=True)
    y = x * lax.rsqrt(ms + EPS) * g1_ref[...]
    h = (y * (1.0 + mod_ref[0, 1:2, :]) + mod_ref[0, 0:1, :]).astype(BF16)

    def proj(c0, c1):
        return jnp.dot(h, w_ref[:, c0:c1], preferred_element_type=F32)

    qkv_ref[0] = proj(C_QKV, C_Z)
    z_ref[0] = proj(C_Z, C_GA)
    ab_ref[0] = proj(C_AB, C_END)

    ones_bd = ones_ref[...]
    cos_t = cos_ref[...]
    sin_t = sin_ref[...]
    lane = lax.broadcasted_iota(jnp.int32, cos_t.shape, 1)
    first_half = (lane & (HEAD_DIM // 2)) == 0

    def norm_rope(t, gain_row, scale):
        yv = t * lax.rsqrt(_head_sumsq(t, ones_bd) * (1.0 / HEAD_DIM) + EPS) * gain_row
        partner = jnp.where(first_half, pltpu.roll(yv, LANES - HEAD_DIM // 2, 1), pltpu.roll(yv, HEAD_DIM // 2, 1))
        return (yv * cos_t + partner * sin_t) * scale

    def heads_out(ref, t, n):
        for i in range(n):
            ref[0, i] = t[:, i * HEAD_DIM:(i + 1) * HEAD_DIM].astype(BF16)

    for c0, q_ref, k_ref, v_ref, gq, gk in ((C_GA, gaq_ref, gak_ref, gav_ref, 0, 1),
                                            (C_WA, waq_ref, wak_ref, wav_ref, 2, 3)):
        p = proj(c0, c0 + 512)
        scale = HEAD_DIM ** -0.5
        q = jnp.concatenate([norm_rope(p[:, 0:128], gains_ref[gq:gq + 1, :], scale),
                             norm_rope(p[:, 128:256], gains_ref[gq:gq + 1, :], scale)], -1)
        k = norm_rope(p[:, 256:384], gains_ref[gk:gk + 1, :], 1.0)
        heads_out(q_ref, q, N_HEADS)
        heads_out(k_ref, k, N_KV)
        heads_out(v_ref, p[:, 384:512], N_KV)

    pf = proj(C_FT, C_AB).astype(BF16)
    ft_ref[0] = jnp.dot(pf, ftw_ref[...], preferred_element_type=F32).astype(BF16)


def _in_projection(xa, mod_l, g1, w_in_r, cos_t, sin_t, gains, ones_bd, ftw, n_ctx_blocks):
    B, R, _ = xa.shape
    nb = R // TM
    ctx_row = MOD_ROWS // 2

    def mod_idx(b, j):
        return (jnp.where(j < n_ctx_blocks, ctx_row, b), 0, 0)

    const2 = lambda b, j: (0, 0)
    tok = lambda w: pl.BlockSpec((1, TM, w), lambda b, j: (b, j, 0))
    hd = lambda n: pl.BlockSpec((1, n, TM, HEAD_DIM), lambda b, j: (b, 0, j, 0))
    sds = jax.ShapeDtypeStruct
    return pl.pallas_call(
        _inproj_kernel,
        grid=(B, nb),
        in_specs=[tok(D_MODEL),
                  pl.BlockSpec((1, N_MOD, D_MODEL), mod_idx),
                  pl.BlockSpec((1, D_MODEL), const2),
                  pl.BlockSpec((D_MODEL, C_END), const2),
                  pl.BlockSpec((TM, LANES), lambda b, j: (j, 0)),
                  pl.BlockSpec((TM, LANES), lambda b, j: (j, 0)),
                  pl.BlockSpec((8, LANES), const2),
                  pl.BlockSpec((LANES, LANES), const2),
                  pl.BlockSpec((256, 512), const2)],
        out_specs=[tok(768), tok(256), tok(LANES),
                   hd(N_HEADS), hd(N_KV), hd(N_KV), hd(N_HEADS), hd(N_KV), hd(N_KV),
                   tok(512)],
        out_shape=[sds((B, R, 768), F32), sds((B, R, 256), F32), sds((B, R, LANES), F32),
                   sds((B, N_HEADS, R, HEAD_DIM), BF16), sds((B, N_KV, R, HEAD_DIM), BF16),
                   sds((B, N_KV, R, HEAD_DIM), BF16),
                   sds((B, N_HEADS, R, HEAD_DIM), BF16), sds((B, N_KV, R, HEAD_DIM), BF16),
                   sds((B, N_KV, R, HEAD_DIM), BF16),
                   sds((B, R, 512), BF16)],
        compiler_params=_cparams(("parallel", "parallel"), 48),
        name="in_projection",
    )(xa, mod_l, g1, w_in_r, cos_t, sin_t, gains, ones_bd, ftw)


def _dnprep_kernel(cur_ref, prev_ref, next_ref, ab_ref, cw_ref, na_ref, dtb_ref, ones_ref,
                   q_ref, k_ref, v_ref, gb_ref, *, ctx_blocks, n_blocks):
    j = pl.program_id(1)
    cur = cur_ref[0]
    tp = cur.shape[0]
    row = lax.broadcasted_iota(jnp.int32, cur.shape, 0)
    has_prev = jnp.where((j == 0) | (j == ctx_blocks), 0.0, 1.0)
    has_next = jnp.where((j == ctx_blocks - 1) | (j == n_blocks - 1), 0.0, 1.0)
    prev_row = prev_ref[0, 7:8, :] * has_prev
    next_row = next_ref[0, 0:1, :] * has_next
    up = jnp.where(row == 0, prev_row, pltpu.roll(cur, 1, 0))
    dn = jnp.where(row == tp - 1, next_row, pltpu.roll(cur, tp - 1, 0))
    a = _silu(up * cw_ref[0:1, :] + cur * cw_ref[1:2, :] + dn * cw_ref[2:3, :])

    ones_bd = ones_ref[...]
    for s in range(2):
        tq = a[:, s * LANES:(s + 1) * LANES]
        q_ref[0, :, s * LANES:(s + 1) * LANES] = tq * lax.rsqrt(_head_sumsq(tq, ones_bd) + EPS) * HEAD_DIM ** -0.5
        tk = a[:, 256 + s * LANES:256 + (s + 1) * LANES]
        k_ref[0, :, s * LANES:(s + 1) * LANES] = tk * lax.rsqrt(_head_sumsq(tk, ones_bd) + EPS)
    v_ref[0] = a[:, 512:768]

    ab = ab_ref[0]
    t = ab + dtb_ref[...]
    softplus = jnp.maximum(t, 0.0) + jnp.log(1.0 + jnp.exp(-jnp.abs(t)))
    g = na_ref[...] * softplus
    beta = 1.0 / (1.0 + jnp.exp(-ab))
    lane = lax.broadcasted_iota(jnp.int32, ab.shape, 1)
    gb_ref[0] = jnp.where(lane < 2 * N_HEADS, g, beta)


def _dn_prep(qkv, ab, conv_w8, neg_a, dt_b, ones_bd, n_ctx_blocks):
    B, R, _ = qkv.shape
    nb = R // TM
    hb = TM // 8
    const2 = lambda b, j: (0, 0)
    tok = lambda w: pl.BlockSpec((1, TM, w), lambda b, j: (b, j, 0))
    sds = jax.ShapeDtypeStruct
    return pl.pallas_call(
        functools.partial(_dnprep_kernel, ctx_blocks=n_ctx_blocks, n_blocks=nb),
        grid=(B, nb),
        in_specs=[tok(768),
                  pl.BlockSpec((1, 8, 768), lambda b, j: (b, jnp.maximum(j * hb - 1, 0), 0)),
                  pl.BlockSpec((1, 8, 768), lambda b, j: (b, jnp.minimum((j + 1) * hb, R // 8 - 1), 0)),
                  tok(LANES),
                  pl.BlockSpec((8, 768), const2),
                  pl.BlockSpec((1, LANES), const2),
                  pl.BlockSpec((1, LANES), const2),
                  pl.BlockSpec((LANES, LANES), const2)],
        out_specs=[tok(256), tok(256), tok(256), tok(LANES)],
        out_shape=[sds((B, R, 256), F32), sds((B, R, 256), F32), sds((B, R, 256), F32), sds((B, R, LANES), F32)],
        compiler_params=_cparams(("parallel", "parallel"), 32),
        name="deltanet_prep",
    )(qkv, qkv, qkv, ab, conv_w8, neg_a, dt_b, ones_bd)


def _dn_chunk(q, k, v, gc_col, b_col, gc_row, g_last, state, incl, strict, eye, blocks):
    nn = (((1,), (0,)), ((), ()))
    nt = (((1,), (1,)), ((), ()))
    tn = (((0,), (0,)), ((), ()))
    dot = lambda a, b, dims: lax.dot_general(a.astype(BF16), b.astype(BF16), dims, preferred_element_type=F32)

    decay = jnp.exp(jnp.where(incl, gc_col - gc_row, NEG_INF))
    e_gc = jnp.exp(gc_col)
    kb = k * b_col
    kq = dot(jnp.concatenate([kb, q], 0), k, nt)
    a_mat = jnp.where(strict, kq[:CHUNK] * decay, 0.0)
    intra = jnp.where(incl, kq[CHUNK:] * decay, 0.0)
    t = eye - jnp.where(blocks[0], a_mat, 0.0)
    for mask in blocks[1:-1]:
        t = t - dot(t, dot(jnp.where(mask, a_mat, 0.0), t, nn), nn)
    rhs = jnp.concatenate([v * b_col, kb * e_gc], 1)
    half = dot(t, rhs, nn)
    sol = half - dot(t, dot(jnp.where(blocks[-1], a_mat, 0.0), half, nn), nn)
    u, w = sol[:, :HEAD_DIM], sol[:, HEAD_DIM:]
    wq = dot(jnp.concatenate([w, q * e_gc], 0), state, nn)
    v_new = u - wq[:CHUNK]
    o = wq[CHUNK:] + dot(intra, v_new, nn)
    k_dec = k * jnp.exp(g_last - gc_col)
    new_state = state * jnp.exp(g_last) + dot(k_dec, v_new, tn)
    return o, new_state


def _dnscan_kernel(qf_ref, kf_ref, vf_ref, gbf_ref, gtf_ref, qb_ref, kb_ref, vb_ref, gbb_ref, gtb_ref,
                   of_ref, ob_ref, state_ref):
    i = pl.program_id(1)

    @pl.when(i == 0)
    def _():
        state_ref[...] = jnp.zeros_like(state_ref)

    ii = lax.broadcasted_iota(jnp.int32, (CHUNK, CHUNK), 0)
    jj = lax.broadcasted_iota(jnp.int32, (CHUNK, CHUNK), 1)
    eye = jnp.where(ii == jj, 1.0, 0.0)
    for d, (q_ref, k_ref, v_ref, gb_ref, gt_ref, o_ref) in enumerate(
            ((qf_ref, kf_ref, vf_ref, gbf_ref, gtf_ref, of_ref),
             (qb_ref, kb_ref, vb_ref, gbb_ref, gtb_ref, ob_ref))):
        incl = (ii >= jj) if d == 0 else (ii <= jj)
        strict = (ii > jj) if d == 0 else (ii < jj)
        cum = jnp.where(incl, 1.0, 0.0)
        late, early = (ii, jj) if d == 0 else (jj, ii)
        blocks = [(late // (2 * s) == early // (2 * s)) & ((late & s) != 0) & ((early & s) == 0)
                  for s in (1, 2, 4, 8, 16, 32)]
        gb = gb_ref[0]
        gt = gt_ref[0, 0]
        gc_cols = jnp.dot(cum, gb, preferred_element_type=F32, precision=HIGHEST)
        gc_rows = lax.dot_general(gt, cum, (((1,), (1,)), ((), ())), preferred_element_type=F32,
                                  precision=HIGHEST)
        last = CHUNK - 1 if d == 0 else 0
        for h in range(N_HEADS):
            c = d * N_HEADS + h
            sl = slice(h * HEAD_DIM, (h + 1) * HEAD_DIM)
            gc_col = gc_cols[:, c:c + 1]
            o, new_state = _dn_chunk(q_ref[0, :, sl], k_ref[0, :, sl], v_ref[0, :, sl],
                                     gc_col, gb[:, 2 * N_HEADS + c:2 * N_HEADS + c + 1],
                                     gc_rows[c:c + 1, :], gc_col[last:last + 1, :],
                                     state_ref[c], incl, strict, eye, blocks)
            o_ref[0, :, sl] = o
            state_ref[c] = new_state


def _dn_scan(q, k, v, gb, gt, n_ctx_chunks):
    B, R, _ = q.shape
    nc = R // CHUNK

    def fwd(b, i):
        return (b, i, 0)

    def bwd_chunk(i):
        return jnp.where(i < n_ctx_chunks, n_ctx_chunks - 1 - i, nc - 1 - (i - n_ctx_chunks))

    def bwd(b, i):
        return (b, bwd_chunk(i), 0)

    tokf = lambda w: pl.BlockSpec((1, CHUNK, w), fwd)
    tokb = lambda w: pl.BlockSpec((1, CHUNK, w), bwd)
    gtf = pl.BlockSpec((1, 1, 16, CHUNK), lambda b, i: (b, i, 0, 0))
    gtb = pl.BlockSpec((1, 1, 16, CHUNK), lambda b, i: (b, bwd_chunk(i), 0, 0))
    sds = jax.ShapeDtypeStruct
    return pl.pallas_call(
        _dnscan_kernel,
        grid=(B, nc),
        in_specs=[tokf(256), tokf(256), tokf(256), tokf(LANES), gtf,
                  tokb(256), tokb(256), tokb(256), tokb(LANES), gtb],
        out_specs=[tokf(256), tokb(256)],
        out_shape=[sds((B, R, 256), F32), sds((B, R, 256), F32)],
        scratch_shapes=[pltpu.VMEM((2 * N_HEADS, HEAD_DIM, HEAD_DIM), F32)],
        compiler_params=_cparams(("parallel", "arbitrary"), 32),
        name="deltanet_scan",
    )(q, k, v, gb, gt, q, k, v, gb, gt)


def _softmax_pv(s_list, v_list, sink):
    m = functools.reduce(jnp.maximum, [s.max(-1, keepdims=True) for s in s_list])
    if sink is not None:
        m = jnp.maximum(m, sink)
    ps = [jnp.exp(s - m) for s in s_list]
    den = functools.reduce(lambda a, b: a + b, [p.sum(-1, keepdims=True) for p in ps])
    if sink is not None:
        den = den + jnp.exp(sink - m)
    o = functools.reduce(lambda a, b: a + b,
                         [jnp.dot(p.astype(BF16), v, preferred_element_type=F32) for p, v in zip(ps, v_list)])
    return o / den


def _scores(q, k):
    return lax.dot_general(q, k, (((1,), (1,)), ((), ())), preferred_element_type=F32)


def _store_heads(o_ref, o, tq):
    o_ref[0] = jnp.concatenate([o[:tq], o[tq:]], -1).astype(BF16)


def _gattn_kernel(q_ref, k_ref, v_ref, o_ref, *, n_ctx, ctx_blocks):
    j = pl.program_id(2)
    tq = q_ref.shape[2]
    q = q_ref[0].reshape(2 * tq, HEAD_DIM)

    @pl.when(j < ctx_blocks)
    def _():
        k = k_ref[0, 0, :n_ctx, :]
        _store_heads(o_ref, _softmax_pv([_scores(q, k)], [v_ref[0, 0, :n_ctx, :]], None), tq)

    @pl.when(j >= ctx_blocks)
    def _():
        _store_heads(o_ref, _softmax_pv([_scores(q, k_ref[0, 0])], [v_ref[0, 0]], None), tq)


def _wattn_kernel(sink_ref, q_ref, k_ref, v_ref, o_ref, *, n_ctx, ctx_blocks):
    h = pl.program_id(1)
    j = pl.program_id(2)
    tq = q_ref.shape[2]
    rows = k_ref.shape[2]
    n_lat = rows - n_ctx
    band = tq + 2 * WINDOW
    q = q_ref[0].reshape(2 * tq, HEAD_DIM)
    row = lax.broadcasted_iota(jnp.int32, (2 * tq, 1), 0)
    sink = jnp.where(row < tq, sink_ref[2 * h], sink_ref[2 * h + 1])
    kc = k_ref[0, 0, :n_ctx, :]
    vc = v_ref[0, 0, :n_ctx, :]
    s_ctx = _scores(q, kc)

    @pl.when(j < ctx_blocks)
    def _():
        _store_heads(o_ref, _softmax_pv([s_ctx], [vc], sink), tq)

    @pl.when(j >= ctx_blocks)
    def _():
        q0 = (j - ctx_blocks) * tq
        start = pl.multiple_of(jnp.clip(q0 - WINDOW, 0, n_lat - band), WINDOW)
        kb = k_ref[0, 0, pl.ds(n_ctx + start, band), :]
        vb = v_ref[0, 0, pl.ds(n_ctx + start, band), :]
        ri = lax.broadcasted_iota(jnp.int32, (2 * tq, band), 0)
        qpos = q0 + jnp.where(ri < tq, ri, ri - tq)
        kpos = start + lax.broadcasted_iota(jnp.int32, (2 * tq, band), 1)
        s_loc = jnp.where(jnp.abs(qpos - kpos) <= WINDOW, _scores(q, kb), NEG_INF)
        _store_heads(o_ref, _softmax_pv([s_loc, s_ctx], [vb, vc], sink), tq)


def _attention(q, k, v, n_ctx, sink=None):
    B, _, R, _ = q.shape
    nb = R // TM
    kw = dict(n_ctx=n_ctx, ctx_blocks=n_ctx // TM)
    specs = [pl.BlockSpec((1, 2, TM, HEAD_DIM), lambda b, h, j: (b, h, j, 0)),
             pl.BlockSpec((1, 1, R, HEAD_DIM), lambda b, h, j: (b, h, 0, 0)),
             pl.BlockSpec((1, 1, R, HEAD_DIM), lambda b, h, j: (b, h, 0, 0))]
    args = (q, k, v)
    if sink is None:
        body = functools.partial(_gattn_kernel, **kw)
        name = "global_attention"
    else:
        body = functools.partial(_wattn_kernel, **kw)
        specs = [pl.BlockSpec(memory_space=pltpu.SMEM)] + specs
        args = (sink,) + args
        name = "window_attention"
    return pl.pallas_call(
        body,
        grid=(B, N_KV, nb),
        in_specs=specs,
        out_specs=pl.BlockSpec((1, TM, LANES), lambda b, h, j: (b, j, h)),
        out_shape=jax.ShapeDtypeStruct((B, R, N_HEADS * HEAD_DIM), BF16),
        compiler_params=_cparams(("parallel", "parallel", "parallel"), 48),
        name=name,
    )(*args)


def _dft_kernel(z_ref, cl_ref, sl_ref, cc_ref, sc_ref, o_ref, *, n_ctx, ctx_blocks):
    j = pl.program_id(1)
    half = z_ref.shape[2] // 2

    def mix(c, s, z):
        return (jnp.dot(c, z[:, :half], preferred_element_type=F32)
                + jnp.dot(s, z[:, half:], preferred_element_type=F32)).astype(BF16)

    @pl.when(j < ctx_blocks)
    def _():
        o_ref[0] = mix(cc_ref[...], sc_ref[...], z_ref[0, :n_ctx, :])

    @pl.when(j >= ctx_blocks)
    def _():
        o_ref[0] = mix(cl_ref[...], sl_ref[...], z_ref[0, n_ctx:, :])


def _position_dft(z, cos_lat, msin_lat, cos_ctx, msin_ctx, n_ctx):
    B, R, _ = z.shape
    nb = R // TM
    n_lat = R - n_ctx
    cb = n_ctx // TM
    assert cb == 1
    lat = pl.BlockSpec((TM, n_lat), lambda b, j: (jnp.maximum(j - cb, 0), 0))
    ctx = pl.BlockSpec((n_ctx, n_ctx), lambda b, j: (0, 0))
    return pl.pallas_call(
        functools.partial(_dft_kernel, n_ctx=n_ctx, ctx_blocks=cb),
        grid=(B, nb),
        in_specs=[pl.BlockSpec((1, R, 512), lambda b, j: (b, 0, 0)), lat, lat, ctx, ctx],
        out_specs=pl.BlockSpec((1, TM, 256), lambda b, j: (b, j, 0)),
        out_shape=jax.ShapeDtypeStruct((B, R, 256), BF16),
        compiler_params=_cparams(("parallel", "parallel"), 48),
        name="position_dft",
    )(z, cos_lat, msin_lat, cos_ctx, msin_ctx)


def _outproj_kernel(of_ref, ob_ref, z_ref, ga_ref, wa_ref, ft_ref, x_ref, mod_ref, ng_ref, ones_ref, w_ref,
                    o_ref):
    o = of_ref[0] + ob_ref[0]
    z = z_ref[0]
    ones_bd = ones_ref[...]
    acc = None
    for s in range(2):
        t = o[:, s * LANES:(s + 1) * LANES]
        y = t * lax.rsqrt(_head_sumsq(t, ones_bd) * (1.0 / HEAD_DIM) + EPS) * ng_ref[...]
        y = (y * _silu(z[:, s * LANES:(s + 1) * LANES])).astype(BF16)
        part = jnp.dot(y, w_ref[s * LANES:(s + 1) * LANES, :], preferred_element_type=F32)
        acc = part if acc is None else acc + part
    for n, ref in enumerate((ga_ref, wa_ref, ft_ref)):
        acc = acc + jnp.dot(ref[0], w_ref[256 * (n + 1):256 * (n + 2), :], preferred_element_type=F32)
    o_ref[0] = x_ref[0] + mod_ref[0, 2:3, :] * acc


def _out_projection(of, ob, z, y_ga, y_wa, y_ft, xa, mod_l, ng, ones_bd, w_out, n_ctx_blocks):
    B, R, _ = xa.shape
    nb = R // TM
    ctx_row = MOD_ROWS // 2
    const2 = lambda b, j: (0, 0)
    tok = lambda w: pl.BlockSpec((1, TM, w), lambda b, j: (b, j, 0))
    return pl.pallas_call(
        _outproj_kernel,
        grid=(B, nb),
        in_specs=[tok(256)] * 6 + [
            tok(D_MODEL),
            pl.BlockSpec((1, N_MOD, D_MODEL), lambda b, j: (jnp.where(j < n_ctx_blocks, ctx_row, b), 0, 0)),
            pl.BlockSpec((1, LANES), const2),
            pl.BlockSpec((LANES, LANES), const2),
            pl.BlockSpec((D_MODEL, D_MODEL), const2)],
        out_specs=tok(D_MODEL),
        out_shape=jax.ShapeDtypeStruct((B, R, D_MODEL), F32),
        compiler_params=_cparams(("parallel", "parallel"), 32),
        name="out_projection",
    )(of, ob, z, y_ga, y_wa, y_ft, xa, mod_l, ng, ones_bd, w_out)


def _ffn_kernel(x_ref, mod_ref, g2_ref, wg_ref, wu_ref, wd_ref, o_ref):
    x = x_ref[0]
    ms = jnp.mean(x * x, -1, keepdims=True)
    y = x * lax.rsqrt(ms + EPS) * g2_ref[...]
    h = (y * (1.0 + mod_ref[0, 4:5, :]) + mod_ref[0, 3:4, :]).astype(BF16)
    acc = None
    for c in range(FFN_HIDDEN // FFN_CHUNK):
        cs = slice(c * FFN_CHUNK, (c + 1) * FFN_CHUNK)
        g = jnp.dot(h, wg_ref[:, cs], preferred_element_type=F32)
        u = jnp.dot(h, wu_ref[:, cs], preferred_element_type=F32)
        a = (_silu(g) * u).astype(BF16)
        part = jnp.dot(a, wd_ref[cs, :], preferred_element_type=F32)
        acc = part if acc is None else acc + part
    o_ref[0] = x + mod_ref[0, 5:6, :] * acc


def _ffn(xa, mod_l, g2, wg, wu, wd, n_ctx_blocks):
    B, R, _ = xa.shape
    nb = R // TM
    ctx_row = MOD_ROWS // 2
    const2 = lambda b, j: (0, 0)
    tok = pl.BlockSpec((1, TM, D_MODEL), lambda b, j: (b, j, 0))
    return pl.pallas_call(
        _ffn_kernel,
        grid=(B, nb),
        in_specs=[tok,
                  pl.BlockSpec((1, N_MOD, D_MODEL), lambda b, j: (jnp.where(j < n_ctx_blocks, ctx_row, b), 0, 0)),
                  pl.BlockSpec((1, D_MODEL), const2),
                  pl.BlockSpec((D_MODEL, FFN_HIDDEN), const2),
                  pl.BlockSpec((D_MODEL, FFN_HIDDEN), const2),
                  pl.BlockSpec((FFN_HIDDEN, D_MODEL), const2)],
        out_specs=tok,
        out_shape=jax.ShapeDtypeStruct((B, R, D_MODEL), F32),
        compiler_params=_cparams(("parallel", "parallel"), 56),
        name="swiglu_ffn",
    )(xa, mod_l, g2, wg, wu, wd)


def _rope_tables(n_ctx, n_lat):
    t = jnp.arange(n_lat, dtype=jnp.int32)
    row = (t // GRID_W).astype(F32)
    col = (t % GRID_W).astype(F32)
    n_freq = HEAD_DIM // 4
    inv_freq = ROPE_THETA ** (-jnp.arange(n_freq, dtype=F32) / n_freq)
    ang = jnp.concatenate([row[:, None] * inv_freq, col[:, None] * inv_freq], -1)
    cos = jnp.tile(jnp.cos(ang), (1, 4))
    sin = jnp.sin(ang)
    sin = jnp.concatenate([-sin, sin, -sin, sin], -1)
    cos = jnp.concatenate([jnp.ones((n_ctx, LANES), F32), cos], 0)
    sin = jnp.concatenate([jnp.zeros((n_ctx, LANES), F32), sin], 0)
    return cos, sin


def _dft_matrices(n, n_chan):
    idx = jnp.arange(n, dtype=jnp.int32)
    ang = ((idx[:, None] * idx[None, :]) % n).astype(F32) * (2.0 * math.pi / n)
    scale = 1.0 / math.sqrt(n * n_chan)
    return (jnp.cos(ang) * scale).astype(BF16), (-jnp.sin(ang) * scale).astype(BF16)


def _channel_dft_weights():
    n = HEAD_DIM
    idx = np.arange(n)
    ang = 2.0 * np.pi * ((idx[:, None] * idx[None, :]) % n) / n
    eye = np.eye(4)
    w = np.concatenate([np.kron(eye, np.cos(ang)), np.kron(eye, np.sin(ang))], 1)
    return jnp.asarray(w, F32).astype(BF16)


def kernel(x, c, ctx, c_ctx, norm1_g, norm2_g, w_ada, b_ada, w_in, dn_conv_w, dn_A_log, dn_dt_bias, dn_norm_g,
           ga_q_norm, ga_k_norm, wa_q_norm, wa_k_norm, wa_sink, w_out, w_ffn_gate, w_ffn_up, w_ffn_down):
    B, S, _ = x.shape
    L = ctx.shape[1]
    depth = w_in.shape[0]
    assert L == TM and S % TM == 0 and B <= MOD_ROWS // 2
    ctx_blocks = L // TM

    cc = jnp.zeros((MOD_ROWS, D_MODEL), F32).at[:B].set(c).at[MOD_ROWS // 2].set(c_ctx)
    mod = _modulation(cc, w_ada, b_ada).reshape(depth, MOD_ROWS, N_MOD, D_MODEL)

    cos_t, sin_t = _rope_tables(L, S)
    cos_lat, msin_lat = _dft_matrices(S, HEAD_DIM)
    cos_ctx, msin_ctx = _dft_matrices(L, HEAD_DIM)
    ftw = _channel_dft_weights()
    ones_bd = jnp.asarray(np.kron(np.eye(2), np.ones((HEAD_DIM, HEAD_DIM))), F32).astype(BF16)
    tile2 = lambda g: jnp.tile(g, 2)

    xa = jnp.concatenate([ctx, x], 1)
    R = L + S
    for l in range(depth):
        w = w_in[l]
        w_r = jnp.concatenate([w[:, :1024], w[:, 1040:2320], w[:, 1024:1040],
                               jnp.zeros((D_MODEL, C_END - 2320), F32)], 1).astype(BF16)
        gains = jnp.zeros((8, LANES), F32)
        for n, g in enumerate((ga_q_norm[l], ga_k_norm[l], wa_q_norm[l], wa_k_norm[l])):
            gains = gains.at[n].set(tile2(g))
        (qkv, z, ab, gaq, gak, gav, waq, wak, wav, ftz) = _in_projection(
            xa, mod[l], norm1_g[l][None, :], w_r, cos_t, sin_t, gains, ones_bd, ftw, ctx_blocks)

        conv_w8 = jnp.zeros((8, 768), F32).at[:3].set(dn_conv_w[l])
        neg_a = jnp.zeros((1, LANES), F32).at[0, :8].set(-jnp.exp(dn_A_log[l].astype(F32)).reshape(-1))
        dt_b = jnp.zeros((1, LANES), F32).at[0, :8].set(dn_dt_bias[l].astype(F32).reshape(-1))
        dq, dk, dv, gb = _dn_prep(qkv, ab, conv_w8, neg_a, dt_b, ones_bd, ctx_blocks)
        gt = gb[:, :, :16].reshape(B, R // CHUNK, CHUNK, 16).transpose(0, 1, 3, 2)
        o_f, o_b = _dn_scan(dq, dk, dv, gb, gt, L // CHUNK)

        y_ga = _attention(gaq, gak, gav, L)
        y_wa = _attention(waq, wak, wav, L, sink=wa_sink[l].astype(F32))
        y_ft = _position_dft(ftz, cos_lat, msin_lat, cos_ctx, msin_ctx, L)

        xa = _out_projection(o_f, o_b, z, y_ga, y_wa, y_ft, xa, mod[l], tile2(dn_norm_g[l])[None, :], ones_bd,
                             w_out[l].astype(BF16), ctx_blocks)
        xa = _ffn(xa, mod[l], norm2_g[l][None, :], w_ffn_gate[l].astype(BF16), w_ffn_up[l].astype(BF16),
                  w_ffn_down[l].astype(BF16), ctx_blocks)
    return xa[:, L:, :]
```

```python
import functools
import math

import jax
import jax.numpy as jnp
import numpy as np
from jax import lax
from jax.experimental import pallas as pl
from jax.experimental.pallas import tpu as pltpu

F32 = jnp.float32
BF16 = jnp.bfloat16
HIGHEST = lax.Precision.HIGHEST

D_MODEL = 1024
HEAD_DIM = 64
N_HEADS = 4
N_KV = 2
GRID_W = 64
CHUNK = 64
WINDOW = 128
ROPE_THETA = 10000.0
FFN_HIDDEN = 2816
N_MOD = 6
EPS = 1e-6
NEG_INF = -1e30

LANES = 128
TM = 256
MOD_ROWS = 16
FFN_CHUNK = 256
DN_BATCH = 2
ATTN_ROW_GROUPS = 4

C_QKV, C_Z, C_GA, C_WA, C_FT, C_AB, C_END = 0, 768, 1024, 1536, 2048, 2304, 2432


def _cparams(sem, vmem_mb):
    return pltpu.CompilerParams(dimension_semantics=sem, vmem_limit_bytes=vmem_mb * 1024 * 1024)


def _silu(x):
    return x * (1.0 / (1.0 + jnp.exp(-x)))


def _head_sumsq(t, ones_bd):
    return jnp.dot((t * t).astype(BF16), ones_bd, preferred_element_type=F32)


def _mod_kernel(c_ref, w_ref, b_ref, o_ref):
    a = _silu(c_ref[...])
    o_ref[0] = jnp.dot(a, w_ref[0], preferred_element_type=F32, precision=HIGHEST) + b_ref[0]


def _modulation(cc, w_ada, b_ada):
    depth = w_ada.shape[0]
    return pl.pallas_call(
        _mod_kernel,
        grid=(depth, N_MOD),
        in_specs=[pl.BlockSpec((MOD_ROWS, D_MODEL), lambda l, n: (0, 0)),
                  pl.BlockSpec((1, D_MODEL, D_MODEL), lambda l, n: (l, 0, n)),
                  pl.BlockSpec((1, 1, D_MODEL), lambda l, n: (l, 0, n))],
        out_specs=pl.BlockSpec((1, MOD_ROWS, D_MODEL), lambda l, n: (l, 0, n)),
        out_shape=jax.ShapeDtypeStruct((depth, MOD_ROWS, N_MOD * D_MODEL), F32),
        compiler_params=_cparams(("parallel", "parallel"), 32),
        name="modulation",
    )(cc, w_ada, b_ada.reshape(depth, 1, N_MOD * D_MODEL))


def _inproj_kernel(x_ref, mod_ref, g1_ref, w_ref, cos_ref, sin_ref, gains_ref, ones_ref, ftw_ref,
                   qkv_ref, z_ref, ab_ref, gaq_ref, gak_ref, gav_ref, waq_ref, wak_ref, wav_ref, ft_ref):
    x = x_ref[0]
    ms = jnp.mean(x * x, -1, keepdims=True)
    y = x * lax.rsqrt(ms + EPS) * g1_ref[...]
    h = (y * (1.0 + mod_ref[0, 1:2, :]) + mod_ref[0, 0:1, :]).astype(BF16)

    def proj(c0, c1):
        return jnp.dot(h, w_ref[:, c0:c1], preferred_element_type=F32)

    qkv_ref[0] = proj(C_QKV, C_Z)
    z_ref[0] = proj(C_Z, C_GA)
    ab_ref[0] = proj(C_AB, C_END)

    ones_bd = ones_ref[...]
    cos_t = cos_ref[...]
    sin_t = sin_ref[...]
    lane = lax.broadcasted_iota(jnp.int32, cos_t.shape, 1)
    first_half = (lane & (HEAD_DIM // 2)) == 0

    def norm_rope(t, gain_row, scale):
        yv = t * lax.rsqrt(_head_sumsq(t, ones_bd) * (1.0 / HEAD_DIM) + EPS) * gain_row
        partner = jnp.where(first_half, pltpu.roll(yv, LANES - HEAD_DIM // 2, 1), pltpu.roll(yv, HEAD_DIM // 2, 1))
        return (yv * cos_t + partner * sin_t) * scale

    def heads_out(ref, t, n):
        for i in range(n):
            ref[0, i] = t[:, i * HEAD_DIM:(i + 1) * HEAD_DIM].astype(BF16)

    for c0, q_ref, k_ref, v_ref, gq, gk in ((C_GA, gaq_ref, gak_ref, gav_ref, 0, 1),
                                            (C_WA, waq_ref, wak_ref, wav_ref, 2, 3)):
        p = proj(c0, c0 + 512)
        scale = HEAD_DIM ** -0.5
        q = jnp.concatenate([norm_rope(p[:, 0:128], gains_ref[gq:gq + 1, :], scale),
                             norm_rope(p[:, 128:256], gains_ref[gq:gq + 1, :], scale)], -1)
        k = norm_rope(p[:, 256:384], gains_ref[gk:gk + 1, :], 1.0)
        heads_out(q_ref, q, N_HEADS)
        heads_out(k_ref, k, N_KV)
        heads_out(v_ref, p[:, 384:512], N_KV)

    pf = proj(C_FT, C_AB).astype(BF16)
    ft_ref[0] = jnp.dot(pf, ftw_ref[...], preferred_element_type=F32).astype(BF16)


def _in_projection(xa, mod_l, g1, w_in_r, cos_t, sin_t, gains, ones_bd, ftw, n_ctx_blocks):
    B, R, _ = xa.shape
    nb = R // TM
    ctx_row = MOD_ROWS // 2

    def mod_idx(b, j):
        return (jnp.where(j < n_ctx_blocks, ctx_row, b), 0, 0)

    const2 = lambda b, j: (0, 0)
    tok = lambda w: pl.BlockSpec((1, TM, w), lambda b, j: (b, j, 0))
    hd = lambda n: pl.BlockSpec((1, n, TM, HEAD_DIM), lambda b, j: (b, 0, j, 0))
    sds = jax.ShapeDtypeStruct
    return pl.pallas_call(
        _inproj_kernel,
        grid=(B, nb),
        in_specs=[tok(D_MODEL),
                  pl.BlockSpec((1, N_MOD, D_MODEL), mod_idx),
                  pl.BlockSpec((1, D_MODEL), const2),
                  pl.BlockSpec((D_MODEL, C_END), const2),
                  pl.BlockSpec((TM, LANES), lambda b, j: (j, 0)),
                  pl.BlockSpec((TM, LANES), lambda b, j: (j, 0)),
                  pl.BlockSpec((8, LANES), const2),
                  pl.BlockSpec((LANES, LANES), const2),
                  pl.BlockSpec((256, 512), const2)],
        out_specs=[tok(768), tok(256), tok(LANES),
                   hd(N_HEADS), hd(N_KV), hd(N_KV), hd(N_HEADS), hd(N_KV), hd(N_KV),
                   tok(512)],
        out_shape=[sds((B, R, 768), F32), sds((B, R, 256), F32), sds((B, R, LANES), F32),
                   sds((B, N_HEADS, R, HEAD_DIM), BF16), sds((B, N_KV, R, HEAD_DIM), BF16),
                   sds((B, N_KV, R, HEAD_DIM), BF16),
                   sds((B, N_HEADS, R, HEAD_DIM), BF16), sds((B, N_KV, R, HEAD_DIM), BF16),
                   sds((B, N_KV, R, HEAD_DIM), BF16),
                   sds((B, R, 512), BF16)],
        compiler_params=_cparams(("parallel", "parallel"), 48),
        name="in_projection",
    )(xa, mod_l, g1, w_in_r, cos_t, sin_t, gains, ones_bd, ftw)


def _dnprep_kernel(cur_ref, prev_ref, next_ref, ab_ref, cw_ref, na_ref, dtb_ref, ones_ref,
                   q_ref, k_ref, v_ref, gb_ref, *, ctx_blocks, n_blocks):
    j = pl.program_id(1)
    cur = cur_ref[0]
    tp = cur.shape[0]
    row = lax.broadcasted_iota(jnp.int32, cur.shape, 0)
    has_prev = jnp.where((j == 0) | (j == ctx_blocks), 0.0, 1.0)
    has_next = jnp.where((j == ctx_blocks - 1) | (j == n_blocks - 1), 0.0, 1.0)
    prev_row = prev_ref[0, 7:8, :] * has_prev
    next_row = next_ref[0, 0:1, :] * has_next
    up = jnp.where(row == 0, prev_row, pltpu.roll(cur, 1, 0))
    dn = jnp.where(row == tp - 1, next_row, pltpu.roll(cur, tp - 1, 0))
    a = _silu(up * cw_ref[0:1, :] + cur * cw_ref[1:2, :] + dn * cw_ref[2:3, :])

    ones_bd = ones_ref[...]

    def heads_out(ref, t, first):
        for i in range(2):
            ref[0, first + i] = t[:, i * HEAD_DIM:(i + 1) * HEAD_DIM]

    for s in range(2):
        tq = a[:, s * LANES:(s + 1) * LANES]
        heads_out(q_ref, tq * lax.rsqrt(_head_sumsq(tq, ones_bd) + EPS) * HEAD_DIM ** -0.5, 2 * s)
        tk = a[:, 256 + s * LANES:256 + (s + 1) * LANES]
        heads_out(k_ref, tk * lax.rsqrt(_head_sumsq(tk, ones_bd) + EPS), 2 * s)
        heads_out(v_ref, a[:, 512 + s * LANES:512 + (s + 1) * LANES], 2 * s)

    ab = ab_ref[0]
    t = ab + dtb_ref[...]
    softplus = jnp.maximum(t, 0.0) + jnp.log(1.0 + jnp.exp(-jnp.abs(t)))
    g = na_ref[...] * softplus
    beta = 1.0 / (1.0 + jnp.exp(-ab))
    lane = lax.broadcasted_iota(jnp.int32, ab.shape, 1)
    gb_ref[0] = jnp.where(lane < 2 * N_HEADS, g, beta)


def _dn_prep(qkv, ab, conv_w8, neg_a, dt_b, ones_bd, n_ctx_blocks):
    B, R, _ = qkv.shape
    nb = R // TM
    hb = TM // 8
    const2 = lambda b, j: (0, 0)
    tok = lambda w: pl.BlockSpec((1, TM, w), lambda b, j: (b, j, 0))
    hd = pl.BlockSpec((1, N_HEADS, TM, HEAD_DIM), lambda b, j: (b, 0, j, 0))
    sds = jax.ShapeDtypeStruct
    return pl.pallas_call(
        functools.partial(_dnprep_kernel, ctx_blocks=n_ctx_blocks, n_blocks=nb),
        grid=(B, nb),
        in_specs=[tok(768),
                  pl.BlockSpec((1, 8, 768), lambda b, j: (b, jnp.maximum(j * hb - 1, 0), 0)),
                  pl.BlockSpec((1, 8, 768), lambda b, j: (b, jnp.minimum((j + 1) * hb, R // 8 - 1), 0)),
                  tok(LANES),
                  pl.BlockSpec((8, 768), const2),
                  pl.BlockSpec((1, LANES), const2),
                  pl.BlockSpec((1, LANES), const2),
                  pl.BlockSpec((LANES, LANES), const2)],
        out_specs=[hd, hd, hd, tok(LANES)],
        out_shape=[sds((B, N_HEADS, R, HEAD_DIM), F32)] * 3 + [sds((B, R, LANES), F32)],
        compiler_params=_cparams(("parallel", "parallel"), 32),
        name="deltanet_prep",
    )(qkv, qkv, qkv, ab, conv_w8, neg_a, dt_b, ones_bd)


def _bdot(a, b, dims):
    return lax.dot_general(a.astype(BF16), b.astype(BF16), dims, preferred_element_type=F32)


_NN = (((2,), (1,)), ((0,), (0,)))
_NT = (((2,), (2,)), ((0,), (0,)))
_TN = (((1,), (1,)), ((0,), (0,)))


def _dnscan_kernel(qf_ref, kf_ref, vf_ref, gbf_ref, gtf_ref, qb_ref, kb_ref, vb_ref, gbb_ref, gtb_ref,
                   of_ref, ob_ref, state_ref, *, bb):
    i = pl.program_id(1)
    n = 2 * N_HEADS * bb

    @pl.when(i == 0)
    def _():
        state_ref[...] = jnp.zeros_like(state_ref)

    ii = lax.broadcasted_iota(jnp.int32, (CHUNK, CHUNK), 0)
    jj = lax.broadcasted_iota(jnp.int32, (CHUNK, CHUNK), 1)
    gc_cols, b_cols, gc_rows, g_lasts = [], [], [], []
    for d, (gb_ref, gt_ref) in enumerate(((gbf_ref, gtf_ref), (gbb_ref, gtb_ref))):
        cum = jnp.where((ii >= jj) if d == 0 else (ii <= jj), 1.0, 0.0)
        last = CHUNK - 1 if d == 0 else 0
        for b in range(bb):
            gb = gb_ref[b]
            gt = gt_ref[b, 0]
            cols = jnp.dot(cum, gb, preferred_element_type=F32, precision=HIGHEST)
            rows = lax.dot_general(gt, cum, (((1,), (1,)), ((), ())), preferred_element_type=F32,
                                   precision=HIGHEST)
            for h in range(N_HEADS):
                c = d * N_HEADS + h
                gc_cols.append(cols[:, c:c + 1])
                g_lasts.append(cols[last:last + 1, c:c + 1])
                b_cols.append(gb[:, 2 * N_HEADS + c:2 * N_HEADS + c + 1])
                gc_rows.append(rows[c:c + 1, :])
    gc_col = jnp.stack(gc_cols)
    b_col = jnp.stack(b_cols)
    gc_row = jnp.stack(gc_rows)
    g_last = jnp.stack(g_lasts)

    shape3 = (n, CHUNK, CHUNK)
    backward = lax.broadcasted_iota(jnp.int32, shape3, 0) >= n // 2
    i3 = lax.broadcasted_iota(jnp.int32, shape3, 1)
    j3 = lax.broadcasted_iota(jnp.int32, shape3, 2)
    late = jnp.where(backward, j3, i3)
    early = jnp.where(backward, i3, j3)
    incl = late >= early
    strict = late > early
    eye = jnp.where(ii == jj, 1.0, 0.0)
    xor = ii ^ jj
    blocks = [(xor >> sh) == 1 for sh in range(6)]

    def stack(f_ref, b_ref):
        return jnp.concatenate([f_ref[...].reshape(n // 2, CHUNK, HEAD_DIM),
                                b_ref[...].reshape(n // 2, CHUNK, HEAD_DIM)], 0)

    q = stack(qf_ref, qb_ref)
    k = stack(kf_ref, kb_ref)
    v = stack(vf_ref, vb_ref)

    decay = jnp.exp(jnp.where(incl, gc_col - gc_row, NEG_INF))
    e_gc = jnp.exp(gc_col)
    kb = k * b_col
    kq = _bdot(jnp.concatenate([kb, q], 1), k, _NT)
    a_mat = jnp.where(strict, kq[:, :CHUNK] * decay, 0.0)
    intra = jnp.where(incl, kq[:, CHUNK:] * decay, 0.0)
    t = eye - jnp.where(blocks[0], a_mat, 0.0)
    for mask in blocks[1:-1]:
        t = t - _bdot(t, _bdot(jnp.where(mask, a_mat, 0.0), t, _NN), _NN)
    rhs = jnp.concatenate([v * b_col, kb * e_gc], 2)
    half = _bdot(t, rhs, _NN)
    sol = half - _bdot(t, _bdot(jnp.where(blocks[-1], a_mat, 0.0), half, _NN), _NN)
    u, w = sol[:, :, :HEAD_DIM], sol[:, :, HEAD_DIM:]
    state = state_ref[...]
    wq = _bdot(jnp.concatenate([w, q * e_gc], 1), state, _NN)
    v_new = u - wq[:, :CHUNK]
    o = wq[:, CHUNK:] + _bdot(intra, v_new, _NN)
    k_dec = k * jnp.exp(g_last - gc_col)
    state_ref[...] = state * jnp.exp(g_last) + _bdot(k_dec, v_new, _TN)

    for d, o_ref in enumerate((of_ref, ob_ref)):
        for b in range(bb):
            for h in range(N_HEADS):
                o_ref[b, :, h * HEAD_DIM:(h + 1) * HEAD_DIM] = o[(d * bb + b) * N_HEADS + h]


def _dn_scan(q, k, v, gb, gt, n_ctx_chunks):
    B, _, R, _ = q.shape
    nc = R // CHUNK
    bb = DN_BATCH if B % DN_BATCH == 0 else 1

    def bwd_chunk(i):
        return jnp.where(i < n_ctx_chunks, n_ctx_chunks - 1 - i, nc - 1 - (i - n_ctx_chunks))

    hdf = pl.BlockSpec((bb, N_HEADS, CHUNK, HEAD_DIM), lambda b, i: (b, 0, i, 0))
    hdb = pl.BlockSpec((bb, N_HEADS, CHUNK, HEAD_DIM), lambda b, i: (b, 0, bwd_chunk(i), 0))
    tokf = lambda w: pl.BlockSpec((bb, CHUNK, w), lambda b, i: (b, i, 0))
    tokb = lambda w: pl.BlockSpec((bb, CHUNK, w), lambda b, i: (b, bwd_chunk(i), 0))
    gtf = pl.BlockSpec((bb, 1, 16, CHUNK), lambda b, i: (b, i, 0, 0))
    gtb = pl.BlockSpec((bb, 1, 16, CHUNK), lambda b, i: (b, bwd_chunk(i), 0, 0))
    sds = jax.ShapeDtypeStruct
    return pl.pallas_call(
        functools.partial(_dnscan_kernel, bb=bb),
        grid=(B // bb, nc),
        in_specs=[hdf, hdf, hdf, tokf(LANES), gtf, hdb, hdb, hdb, tokb(LANES), gtb],
        out_specs=[tokf(256), tokb(256)],
        out_shape=[sds((B, R, 256), F32), sds((B, R, 256), F32)],
        scratch_shapes=[pltpu.VMEM((2 * N_HEADS * bb, HEAD_DIM, HEAD_DIM), F32)],
        compiler_params=_cparams(("parallel", "arbitrary"), 32),
        name="deltanet_scan",
    )(q, k, v, gb, gt, q, k, v, gb, gt)


def _softmax_pv(s_list, v_list, sink):
    m = functools.reduce(jnp.maximum, [s.max(-1, keepdims=True) for s in s_list])
    if sink is not None:
        m = jnp.maximum(m, sink)
    ps = [jnp.exp(s - m) for s in s_list]
    den = functools.reduce(lambda a, b: a + b, [p.sum(-1, keepdims=True) for p in ps])
    if sink is not None:
        den = den + jnp.exp(sink - m)
    o = functools.reduce(lambda a, b: a + b,
                         [jnp.dot(p.astype(BF16), v, preferred_element_type=F32) for p, v in zip(ps, v_list)])
    return o / den


def _scores(q, k):
    return lax.dot_general(q, k, (((1,), (1,)), ((), ())), preferred_element_type=F32)


def _store_heads(o_ref, o, tq):
    o_ref[0] = jnp.concatenate([o[:tq], o[tq:]], -1).astype(BF16)


def _gattn_kernel(q_ref, k_ref, v_ref, o_ref, *, n_ctx, ctx_blocks):
    j = pl.program_id(2)
    tq = q_ref.shape[2]
    q = q_ref[0].reshape(2 * tq, HEAD_DIM)

    @pl.when(j < ctx_blocks)
    def _():
        k = k_ref[0, 0, :n_ctx, :]
        _store_heads(o_ref, _softmax_pv([_scores(q, k)], [v_ref[0, 0, :n_ctx, :]], None), tq)

    @pl.when(j >= ctx_blocks)
    def _():
        k = k_ref[0, 0]
        v = v_ref[0, 0]
        rg = 2 * tq // ATTN_ROW_GROUPS
        o = [_softmax_pv([_scores(q[g * rg:(g + 1) * rg], k)], [v], None) for g in range(ATTN_ROW_GROUPS)]
        _store_heads(o_ref, jnp.concatenate(o, 0), tq)


def _wattn_kernel(sink_ref, q_ref, k_ref, v_ref, o_ref, *, n_ctx, ctx_blocks):
    h = pl.program_id(1)
    j = pl.program_id(2)
    tq = q_ref.shape[2]
    rows = k_ref.shape[2]
    n_lat = rows - n_ctx
    band = tq + 2 * WINDOW
    q = q_ref[0].reshape(2 * tq, HEAD_DIM)
    row = lax.broadcasted_iota(jnp.int32, (2 * tq, 1), 0)
    sink = jnp.where(row < tq, sink_ref[2 * h], sink_ref[2 * h + 1])
    kc = k_ref[0, 0, :n_ctx, :]
    vc = v_ref[0, 0, :n_ctx, :]
    s_ctx = _scores(q, kc)

    @pl.when(j < ctx_blocks)
    def _():
        _store_heads(o_ref, _softmax_pv([s_ctx], [vc], sink), tq)

    @pl.when(j >= ctx_blocks)
    def _():
        q0 = (j - ctx_blocks) * tq
        start = pl.multiple_of(jnp.clip(q0 - WINDOW, 0, n_lat - band), WINDOW)
        kb = k_ref[0, 0, pl.ds(n_ctx + start, band), :]
        vb = v_ref[0, 0, pl.ds(n_ctx + start, band), :]
        ri = lax.broadcasted_iota(jnp.int32, (2 * tq, band), 0)
        qpos = q0 + jnp.where(ri < tq, ri, ri - tq)
        kpos = start + lax.broadcasted_iota(jnp.int32, (2 * tq, band), 1)
        s_loc = jnp.where(jnp.abs(qpos - kpos) <= WINDOW, _scores(q, kb), NEG_INF)
        _store_heads(o_ref, _softmax_pv([s_loc, s_ctx], [vb, vc], sink), tq)


def _attention(q, k, v, n_ctx, sink=None):
    B, _, R, _ = q.shape
    nb = R // TM
    kw = dict(n_ctx=n_ctx, ctx_blocks=n_ctx // TM)
    specs = [pl.BlockSpec((1, 2, TM, HEAD_DIM), lambda b, h, j: (b, h, j, 0)),
             pl.BlockSpec((1, 1, R, HEAD_DIM), lambda b, h, j: (b, h, 0, 0)),
             pl.BlockSpec((1, 1, R, HEAD_DIM), lambda b, h, j: (b, h, 0, 0))]
    args = (q, k, v)
    if sink is None:
        body = functools.partial(_gattn_kernel, **kw)
        name = "global_attention"
    else:
        body = functools.partial(_wattn_kernel, **kw)
        specs = [pl.BlockSpec(memory_space=pltpu.SMEM)] + specs
        args = (sink,) + args
        name = "window_attention"
    return pl.pallas_call(
        body,
        grid=(B, N_KV, nb),
        in_specs=specs,
        out_specs=pl.BlockSpec((1, TM, LANES), lambda b, h, j: (b, j, h)),
        out_shape=jax.ShapeDtypeStruct((B, R, N_HEADS * HEAD_DIM), BF16),
        compiler_params=_cparams(("parallel", "parallel", "parallel"), 48),
        name=name,
    )(*args)


def _dft_kernel(z_ref, cl_ref, sl_ref, cc_ref, sc_ref, o_ref, *, n_ctx, ctx_blocks):
    j = pl.program_id(1)
    half = z_ref.shape[2] // 2

    def mix(c, s, z):
        return (jnp.dot(c, z[:, :half], preferred_element_type=F32)
                + jnp.dot(s, z[:, half:], preferred_element_type=F32)).astype(BF16)

    @pl.when(j < ctx_blocks)
    def _():
        o_ref[0] = mix(cc_ref[...], sc_ref[...], z_ref[0, :n_ctx, :])

    @pl.when(j >= ctx_blocks)
    def _():
        o_ref[0] = mix(cl_ref[...], sl_ref[...], z_ref[0, n_ctx:, :])


def _position_dft(z, cos_lat, msin_lat, cos_ctx, msin_ctx, n_ctx):
    B, R, _ = z.shape
    nb = R // TM
    n_lat = R - n_ctx
    cb = n_ctx // TM
    assert cb == 1
    lat = pl.BlockSpec((TM, n_lat), lambda b, j: (jnp.maximum(j - cb, 0), 0))
    ctx = pl.BlockSpec((n_ctx, n_ctx), lambda b, j: (0, 0))
    return pl.pallas_call(
        functools.partial(_dft_kernel, n_ctx=n_ctx, ctx_blocks=cb),
        grid=(B, nb),
        in_specs=[pl.BlockSpec((1, R, 512), lambda b, j: (b, 0, 0)), lat, lat, ctx, ctx],
        out_specs=pl.BlockSpec((1, TM, 256), lambda b, j: (b, j, 0)),
        out_shape=jax.ShapeDtypeStruct((B, R, 256), BF16),
        compiler_params=_cparams(("parallel", "parallel"), 48),
        name="position_dft",
    )(z, cos_lat, msin_lat, cos_ctx, msin_ctx)


def _outproj_kernel(of_ref, ob_ref, z_ref, ga_ref, wa_ref, ft_ref, x_ref, mod_ref, ng_ref, ones_ref, w_ref,
                    o_ref):
    o = of_ref[0] + ob_ref[0]
    z = z_ref[0]
    ones_bd = ones_ref[...]
    acc = None
    for s in range(2):
        t = o[:, s * LANES:(s + 1) * LANES]
        y = t * lax.rsqrt(_head_sumsq(t, ones_bd) * (1.0 / HEAD_DIM) + EPS) * ng_ref[...]
        y = (y * _silu(z[:, s * LANES:(s + 1) * LANES])).astype(BF16)
        part = jnp.dot(y, w_ref[s * LANES:(s + 1) * LANES, :], preferred_element_type=F32)
        acc = part if acc is None else acc + part
    for n, ref in enumerate((ga_ref, wa_ref, ft_ref)):
        acc = acc + jnp.dot(ref[0], w_ref[256 * (n + 1):256 * (n + 2), :], preferred_element_type=F32)
    o_ref[0] = x_ref[0] + mod_ref[0, 2:3, :] * acc


def _out_projection(of, ob, z, y_ga, y_wa, y_ft, xa, mod_l, ng, ones_bd, w_out, n_ctx_blocks):
    B, R, _ = xa.shape
    nb = R // TM
    ctx_row = MOD_ROWS // 2
    const2 = lambda b, j: (0, 0)
    tok = lambda w: pl.BlockSpec((1, TM, w), lambda b, j: (b, j, 0))
    return pl.pallas_call(
        _outproj_kernel,
        grid=(B, nb),
        in_specs=[tok(256)] * 6 + [
            tok(D_MODEL),
            pl.BlockSpec((1, N_MOD, D_MODEL), lambda b, j: (jnp.where(j < n_ctx_blocks, ctx_row, b), 0, 0)),
            pl.BlockSpec((1, LANES), const2),
            pl.BlockSpec((LANES, LANES), const2),
            pl.BlockSpec((D_MODEL, D_MODEL), const2)],
        out_specs=tok(D_MODEL),
        out_shape=jax.ShapeDtypeStruct((B, R, D_MODEL), F32),
        compiler_params=_cparams(("parallel", "parallel"), 32),
        name="out_projection",
    )(of, ob, z, y_ga, y_wa, y_ft, xa, mod_l, ng, ones_bd, w_out)


def _ffn_kernel(x_ref, mod_ref, g2_ref, wg_ref, wu_ref, wd_ref, o_ref):
    x = x_ref[0]
    ms = jnp.mean(x * x, -1, keepdims=True)
    y = x * lax.rsqrt(ms + EPS) * g2_ref[...]
    h = (y * (1.0 + mod_ref[0, 4:5, :]) + mod_ref[0, 3:4, :]).astype(BF16)
    acc = None
    for c in range(FFN_HIDDEN // FFN_CHUNK):
        cs = slice(c * FFN_CHUNK, (c + 1) * FFN_CHUNK)
        g = jnp.dot(h, wg_ref[:, cs], preferred_element_type=F32)
        u = jnp.dot(h, wu_ref[:, cs], preferred_element_type=F32)
        a = (_silu(g) * u).astype(BF16)
        part = jnp.dot(a, wd_ref[cs, :], preferred_element_type=F32)
        acc = part if acc is None else acc + part
    o_ref[0] = x + mod_ref[0, 5:6, :] * acc


def _ffn(xa, mod_l, g2, wg, wu, wd, n_ctx_blocks):
    B, R, _ = xa.shape
    nb = R // TM
    ctx_row = MOD_ROWS // 2
    const2 = lambda b, j: (0, 0)
    tok = pl.BlockSpec((1, TM, D_MODEL), lambda b, j: (b, j, 0))
    return pl.pallas_call(
        _ffn_kernel,
        grid=(B, nb),
        in_specs=[tok,
                  pl.BlockSpec((1, N_MOD, D_MODEL), lambda b, j: (jnp.where(j < n_ctx_blocks, ctx_row, b), 0, 0)),
                  pl.BlockSpec((1, D_MODEL), const2),
                  pl.BlockSpec((D_MODEL, FFN_HIDDEN), const2),
                  pl.BlockSpec((D_MODEL, FFN_HIDDEN), const2),
                  pl.BlockSpec((FFN_HIDDEN, D_MODEL), const2)],
        out_specs=tok,
        out_shape=jax.ShapeDtypeStruct((B, R, D_MODEL), F32),
        compiler_params=_cparams(("parallel", "parallel"), 56),
        name="swiglu_ffn",
    )(xa, mod_l, g2, wg, wu, wd)


def _rope_tables(n_ctx, n_lat):
    t = jnp.arange(n_lat, dtype=jnp.int32)
    row = (t // GRID_W).astype(F32)
    col = (t % GRID_W).astype(F32)
    n_freq = HEAD_DIM // 4
    inv_freq = ROPE_THETA ** (-jnp.arange(n_freq, dtype=F32) / n_freq)
    ang = jnp.concatenate([row[:, None] * inv_freq, col[:, None] * inv_freq], -1)
    cos = jnp.tile(jnp.cos(ang), (1, 4))
    sin = jnp.sin(ang)
    sin = jnp.concatenate([-sin, sin, -sin, sin], -1)
    cos = jnp.concatenate([jnp.ones((n_ctx, LANES), F32), cos], 0)
    sin = jnp.concatenate([jnp.zeros((n_ctx, LANES), F32), sin], 0)
    return cos, sin


def _dft_matrices(n, n_chan):
    idx = jnp.arange(n, dtype=jnp.int32)
    ang = ((idx[:, None] * idx[None, :]) % n).astype(F32) * (2.0 * math.pi / n)
    scale = 1.0 / math.sqrt(n * n_chan)
    return (jnp.cos(ang) * scale).astype(BF16), (-jnp.sin(ang) * scale).astype(BF16)


def _channel_dft_weights():
    n = HEAD_DIM
    idx = np.arange(n)
    ang = 2.0 * np.pi * ((idx[:, None] * idx[None, :]) % n) / n
    eye = np.eye(4)
    w = np.concatenate([np.kron(eye, np.cos(ang)), np.kron(eye, np.sin(ang))], 1)
    return jnp.asarray(w, F32).astype(BF16)


def kernel(x, c, ctx, c_ctx, norm1_g, norm2_g, w_ada, b_ada, w_in, dn_conv_w, dn_A_log, dn_dt_bias, dn_norm_g,
           ga_q_norm, ga_k_norm, wa_q_norm, wa_k_norm, wa_sink, w_out, w_ffn_gate, w_ffn_up, w_ffn_down):
    B, S, _ = x.shape
    L = ctx.shape[1]
    depth = w_in.shape[0]
    assert L == TM and S % TM == 0 and B <= MOD_ROWS // 2
    ctx_blocks = L // TM

    cc = jnp.zeros((MOD_ROWS, D_MODEL), F32).at[:B].set(c).at[MOD_ROWS // 2].set(c_ctx)
    mod = _modulation(cc, w_ada, b_ada).reshape(depth, MOD_ROWS, N_MOD, D_MODEL)

    cos_t, sin_t = _rope_tables(L, S)
    cos_lat, msin_lat = _dft_matrices(S, HEAD_DIM)
    cos_ctx, msin_ctx = _dft_matrices(L, HEAD_DIM)
    ftw = _channel_dft_weights()
    ones_bd = jnp.asarray(np.kron(np.eye(2), np.ones((HEAD_DIM, HEAD_DIM))), F32).astype(BF16)
    tile2 = lambda g: jnp.tile(g, 2)

    xa = jnp.concatenate([ctx, x], 1)
    R = L + S
    for l in range(depth):
        w = w_in[l]
        w_r = jnp.concatenate([w[:, :1024], w[:, 1040:2320], w[:, 1024:1040],
                               jnp.zeros((D_MODEL, C_END - 2320), F32)], 1).astype(BF16)
        gains = jnp.zeros((8, LANES), F32)
        for n, g in enumerate((ga_q_norm[l], ga_k_norm[l], wa_q_norm[l], wa_k_norm[l])):
            gains = gains.at[n].set(tile2(g))
        (qkv, z, ab, gaq, gak, gav, waq, wak, wav, ftz) = _in_projection(
            xa, mod[l], norm1_g[l][None, :], w_r, cos_t, sin_t, gains, ones_bd, ftw, ctx_blocks)

        conv_w8 = jnp.zeros((8, 768), F32).at[:3].set(dn_conv_w[l])
        neg_a = jnp.zeros((1, LANES), F32).at[0, :8].set(-jnp.exp(dn_A_log[l].astype(F32)).reshape(-1))
        dt_b = jnp.zeros((1, LANES), F32).at[0, :8].set(dn_dt_bias[l].astype(F32).reshape(-1))
        dq, dk, dv, gb = _dn_prep(qkv, ab, conv_w8, neg_a, dt_b, ones_bd, ctx_blocks)
        gt = gb[:, :, :16].reshape(B, R // CHUNK, CHUNK, 16).transpose(0, 1, 3, 2)
        o_f, o_b = _dn_scan(dq, dk, dv, gb, gt, L // CHUNK)

        y_ga = _attention(gaq, gak, gav, L)
        y_wa = _attention(waq, wak, wav, L, sink=wa_sink[l].astype(F32))
        y_ft = _position_dft(ftz, cos_lat, msin_lat, cos_ctx, msin_ctx, L)

        xa = _out_projection(o_f, o_b, z, y_ga, y_wa, y_ft, xa, mod[l], tile2(dn_norm_g[l])[None, :], ones_bd,
                             w_out[l].astype(BF16), ctx_blocks)
        xa = _ffn(xa, mod[l], norm2_g[l][None, :], w_ffn_gate[l].astype(BF16), w_ffn_up[l].astype(BF16),
                  w_ffn_down[l].astype(BF16), ctx_blocks)
    return xa[:, L:, :]
```

```python
import functools
import math

import jax
import jax.numpy as jnp
import numpy as np
from jax import lax
from jax.experimental import pallas as pl
from jax.experimental.pallas import tpu as pltpu

F32 = jnp.float32
BF16 = jnp.bfloat16
HIGHEST = lax.Precision.HIGHEST

D_MODEL = 1024
HEAD_DIM = 64
N_HEADS = 4
N_KV = 2
GRID_W = 64
CHUNK = 64
WINDOW = 128
ROPE_THETA = 10000.0
FFN_HIDDEN = 2816
N_MOD = 6
EPS = 1e-6
NEG_INF = -1e30

LANES = 128
TM = 512
TQ_GLOBAL = 256
TQ_WINDOW = 512
TQ_DFT = 256
MOD_ROWS = 16
FFN_CHUNK = 256
DN_BATCH = 4
ATTN_ROW_GROUPS = 2

C_QKV, C_Z, C_GA, C_WA, C_FT, C_AB, C_END = 0, 768, 1024, 1536, 2048, 2304, 2432


def _cparams(sem, vmem_mb):
    return pltpu.CompilerParams(dimension_semantics=sem, vmem_limit_bytes=vmem_mb * 1024 * 1024)


def _silu(x):
    return x * (1.0 / (1.0 + jnp.exp(-x)))


def _head_sumsq(t, ones_bd):
    return jnp.dot((t * t).astype(BF16), ones_bd, preferred_element_type=F32)


def _mod_spec(n_lat_blocks):
    ctx_row = MOD_ROWS // 2
    return pl.BlockSpec((1, N_MOD, D_MODEL), lambda b, j: (jnp.where(j == n_lat_blocks, ctx_row, b), 0, 0))


def _mod_kernel(c_ref, w_ref, b_ref, o_ref):
    a = _silu(c_ref[...])
    o_ref[0] = jnp.dot(a, w_ref[0], preferred_element_type=F32, precision=HIGHEST) + b_ref[0]


def _modulation(cc, w_ada, b_ada):
    depth = w_ada.shape[0]
    return pl.pallas_call(
        _mod_kernel,
        grid=(depth, N_MOD),
        in_specs=[pl.BlockSpec((MOD_ROWS, D_MODEL), lambda l, n: (0, 0)),
                  pl.BlockSpec((1, D_MODEL, D_MODEL), lambda l, n: (l, 0, n)),
                  pl.BlockSpec((1, 1, D_MODEL), lambda l, n: (l, 0, n))],
        out_specs=pl.BlockSpec((1, MOD_ROWS, D_MODEL), lambda l, n: (l, 0, n)),
        out_shape=jax.ShapeDtypeStruct((depth, MOD_ROWS, N_MOD * D_MODEL), F32),
        compiler_params=_cparams(("parallel", "parallel"), 32),
        name="modulation",
    )(cc, w_ada, b_ada.reshape(depth, 1, N_MOD * D_MODEL))


def _inproj_kernel(x_ref, mod_ref, g1_ref, w_ref, cos_ref, sin_ref, gains_ref, ones_ref, ftw_ref,
                   qkv_ref, z_ref, ab_ref, gaq_ref, gak_ref, gav_ref, waq_ref, wak_ref, wav_ref, ft_ref):
    x = x_ref[0]
    ms = jnp.mean(x * x, -1, keepdims=True)
    y = x * lax.rsqrt(ms + EPS) * g1_ref[...]
    h = (y * (1.0 + mod_ref[0, 1:2, :]) + mod_ref[0, 0:1, :]).astype(BF16)

    def proj(c0, c1):
        return jnp.dot(h, w_ref[:, c0:c1], preferred_element_type=F32)

    qkv_ref[0] = proj(C_QKV, C_Z)
    z_ref[0] = proj(C_Z, C_GA)
    ab_ref[0] = proj(C_AB, C_END)

    ones_bd = ones_ref[...]
    cos_t = cos_ref[...]
    sin_t = sin_ref[...]
    lane = lax.broadcasted_iota(jnp.int32, cos_t.shape, 1)
    first_half = (lane & (HEAD_DIM // 2)) == 0

    def norm_rope(t, gain_row, scale):
        yv = t * lax.rsqrt(_head_sumsq(t, ones_bd) * (1.0 / HEAD_DIM) + EPS) * gain_row
        partner = jnp.where(first_half, pltpu.roll(yv, LANES - HEAD_DIM // 2, 1), pltpu.roll(yv, HEAD_DIM // 2, 1))
        return (yv * cos_t + partner * sin_t) * scale

    def heads_out(ref, t, n):
        for i in range(n):
            ref[0, i] = t[:, i * HEAD_DIM:(i + 1) * HEAD_DIM].astype(BF16)

    for c0, q_ref, k_ref, v_ref, gq, gk in ((C_GA, gaq_ref, gak_ref, gav_ref, 0, 1),
                                            (C_WA, waq_ref, wak_ref, wav_ref, 2, 3)):
        p = proj(c0, c0 + 512)
        scale = HEAD_DIM ** -0.5
        q = jnp.concatenate([norm_rope(p[:, 0:128], gains_ref[gq:gq + 1, :], scale),
                             norm_rope(p[:, 128:256], gains_ref[gq:gq + 1, :], scale)], -1)
        k = norm_rope(p[:, 256:384], gains_ref[gk:gk + 1, :], 1.0)
        heads_out(q_ref, q, N_HEADS)
        heads_out(k_ref, k, N_KV)
        heads_out(v_ref, p[:, 384:512], N_KV)

    pf = proj(C_FT, C_AB).astype(BF16)
    ft_ref[0] = jnp.dot(pf, ftw_ref[...], preferred_element_type=F32).astype(BF16)


def _in_projection(xa, mod_l, g1, w_in_r, cos_t, sin_t, gains, ones_bd, ftw, n_lat):
    B, R, _ = xa.shape
    nb = pl.cdiv(R, TM)
    const2 = lambda b, j: (0, 0)
    tok = lambda w: pl.BlockSpec((1, TM, w), lambda b, j: (b, j, 0))
    hd = lambda n: pl.BlockSpec((1, n, TM, HEAD_DIM), lambda b, j: (b, 0, j, 0))
    sds = jax.ShapeDtypeStruct
    return pl.pallas_call(
        _inproj_kernel,
        grid=(B, nb),
        in_specs=[tok(D_MODEL),
                  _mod_spec(n_lat // TM),
                  pl.BlockSpec((1, D_MODEL), const2),
                  pl.BlockSpec((D_MODEL, C_END), const2),
                  pl.BlockSpec((TM, LANES), lambda b, j: (j, 0)),
                  pl.BlockSpec((TM, LANES), lambda b, j: (j, 0)),
                  pl.BlockSpec((8, LANES), const2),
                  pl.BlockSpec((LANES, LANES), const2),
                  pl.BlockSpec((256, 512), const2)],
        out_specs=[tok(768), tok(256), tok(LANES),
                   hd(N_HEADS), hd(N_KV), hd(N_KV), hd(N_HEADS), hd(N_KV), hd(N_KV),
                   tok(512)],
        out_shape=[sds((B, R, 768), F32), sds((B, R, 256), F32), sds((B, R, LANES), F32),
                   sds((B, N_HEADS, R, HEAD_DIM), BF16), sds((B, N_KV, R, HEAD_DIM), BF16),
                   sds((B, N_KV, R, HEAD_DIM), BF16),
                   sds((B, N_HEADS, R, HEAD_DIM), BF16), sds((B, N_KV, R, HEAD_DIM), BF16),
                   sds((B, N_KV, R, HEAD_DIM), BF16),
                   sds((B, R, 512), BF16)],
        compiler_params=_cparams(("parallel", "parallel"), 48),
        name="in_projection",
    )(xa, mod_l, g1, w_in_r, cos_t, sin_t, gains, ones_bd, ftw)


def _dnprep_kernel(cur_ref, prev_ref, next_ref, ab_ref, cw_ref, na_ref, dtb_ref, ones_ref,
                   q_ref, k_ref, v_ref, gb_ref, *, lat_blocks, n_ctx):
    j = pl.program_id(1)
    cur = cur_ref[0]
    tp = cur.shape[0]
    row = lax.broadcasted_iota(jnp.int32, cur.shape, 0)
    is_ctx = j == lat_blocks
    has_prev = jnp.where((j == 0) | is_ctx, 0.0, 1.0)
    has_next = jnp.where(j >= lat_blocks - 1, 0.0, 1.0)
    last_row = jnp.where(is_ctx, n_ctx - 1, tp - 1)
    prev_row = prev_ref[0, 7:8, :] * has_prev
    next_row = next_ref[0, 0:1, :] * has_next
    up = jnp.where(row == 0, prev_row, pltpu.roll(cur, 1, 0))
    dn = jnp.where(row == last_row, next_row, pltpu.roll(cur, tp - 1, 0))
    a = _silu(up * cw_ref[0:1, :] + cur * cw_ref[1:2, :] + dn * cw_ref[2:3, :])

    ones_bd = ones_ref[...]

    def heads_out(ref, t, first):
        for i in range(2):
            ref[0, first + i] = t[:, i * HEAD_DIM:(i + 1) * HEAD_DIM]

    for s in range(2):
        tq = a[:, s * LANES:(s + 1) * LANES]
        heads_out(q_ref, tq * lax.rsqrt(_head_sumsq(tq, ones_bd) + EPS) * HEAD_DIM ** -0.5, 2 * s)
        tk = a[:, 256 + s * LANES:256 + (s + 1) * LANES]
        heads_out(k_ref, tk * lax.rsqrt(_head_sumsq(tk, ones_bd) + EPS), 2 * s)
        heads_out(v_ref, a[:, 512 + s * LANES:512 + (s + 1) * LANES], 2 * s)

    ab = ab_ref[0]
    t = ab + dtb_ref[...]
    softplus = jnp.maximum(t, 0.0) + jnp.log(1.0 + jnp.exp(-jnp.abs(t)))
    g = na_ref[...] * softplus
    beta = 1.0 / (1.0 + jnp.exp(-ab))
    lane = lax.broadcasted_iota(jnp.int32, (CHUNK, LANES), 1)
    ii = lax.broadcasted_iota(jnp.int32, (CHUNK, CHUNK), 0)
    jj = lax.broadcasted_iota(jnp.int32, (CHUNK, CHUNK), 1)
    prefix = jnp.where(ii >= jj, 1.0, 0.0)
    suffix = jnp.where(ii <= jj, 1.0, 0.0)
    for c in range(tp // CHUNK):
        rows = slice(c * CHUNK, (c + 1) * CHUNK)
        gc = g[rows]
        fwd = jnp.dot(prefix, gc, preferred_element_type=F32, precision=HIGHEST)
        bwd = jnp.dot(suffix, gc, preferred_element_type=F32, precision=HIGHEST)
        gb_ref[0, rows, :] = jnp.where(lane < N_HEADS, fwd, jnp.where(lane < 2 * N_HEADS, bwd, beta[rows]))


def _dn_prep(qkv, ab, conv_w8, neg_a, dt_b, ones_bd, n_lat):
    B, R, _ = qkv.shape
    nb = pl.cdiv(R, TM)
    hb = TM // 8
    const2 = lambda b, j: (0, 0)
    tok = lambda w: pl.BlockSpec((1, TM, w), lambda b, j: (b, j, 0))
    hd = pl.BlockSpec((1, N_HEADS, TM, HEAD_DIM), lambda b, j: (b, 0, j, 0))
    sds = jax.ShapeDtypeStruct
    return pl.pallas_call(
        functools.partial(_dnprep_kernel, lat_blocks=n_lat // TM, n_ctx=R - n_lat),
        grid=(B, nb),
        in_specs=[tok(768),
                  pl.BlockSpec((1, 8, 768), lambda b, j: (b, jnp.maximum(j * hb - 1, 0), 0)),
                  pl.BlockSpec((1, 8, 768), lambda b, j: (b, jnp.minimum((j + 1) * hb, R // 8 - 1), 0)),
                  tok(LANES),
                  pl.BlockSpec((8, 768), const2),
                  pl.BlockSpec((1, LANES), const2),
                  pl.BlockSpec((1, LANES), const2),
                  pl.BlockSpec((LANES, LANES), const2)],
        out_specs=[hd, hd, hd, tok(LANES)],
        out_shape=[sds((B, N_HEADS, R, HEAD_DIM), F32)] * 3 + [sds((B, R, LANES), F32)],
        compiler_params=_cparams(("parallel", "parallel"), 32),
        name="deltanet_prep",
    )(qkv, qkv, qkv, ab, conv_w8, neg_a, dt_b, ones_bd)


def _bdot(a, b, dims):
    return lax.dot_general(a.astype(BF16), b.astype(BF16), dims, preferred_element_type=F32)


_NN = (((2,), (1,)), ((0,), (0,)))
_NT = (((2,), (2,)), ((0,), (0,)))
_TN = (((1,), (1,)), ((0,), (0,)))


def _dnscan_kernel(qf_ref, kf_ref, vf_ref, gbf_ref, gtf_ref, qb_ref, kb_ref, vb_ref, gbb_ref, gtb_ref,
                   of_ref, ob_ref, state_ref, *, bb):
    i = pl.program_id(1)
    n = 2 * N_HEADS * bb

    @pl.when(i == 0)
    def _():
        state_ref[...] = jnp.zeros_like(state_ref)

    gc_cols, b_cols, gc_rows, g_lasts = [], [], [], []
    for d, (gb_ref, gt_ref) in enumerate(((gbf_ref, gtf_ref), (gbb_ref, gtb_ref))):
        last = CHUNK - 1 if d == 0 else 0
        for b in range(bb):
            gb = gb_ref[b]
            gt = gt_ref[b, 0]
            for h in range(N_HEADS):
                c = d * N_HEADS + h
                gc_cols.append(gb[:, c:c + 1])
                g_lasts.append(gb[last:last + 1, c:c + 1])
                b_cols.append(gb[:, 2 * N_HEADS + c:2 * N_HEADS + c + 1])
                gc_rows.append(gt[c:c + 1, :])
    gc_col = jnp.stack(gc_cols)
    b_col = jnp.stack(b_cols)
    gc_row = jnp.stack(gc_rows)
    g_last = jnp.stack(g_lasts)

    shape3 = (n, CHUNK, CHUNK)
    backward = lax.broadcasted_iota(jnp.int32, shape3, 0) >= n // 2
    i3 = lax.broadcasted_iota(jnp.int32, shape3, 1)
    j3 = lax.broadcasted_iota(jnp.int32, shape3, 2)
    late = jnp.where(backward, j3, i3)
    early = jnp.where(backward, i3, j3)
    incl = late >= early
    strict = late > early
    ii = lax.broadcasted_iota(jnp.int32, (CHUNK, CHUNK), 0)
    jj = lax.broadcasted_iota(jnp.int32, (CHUNK, CHUNK), 1)
    eye = jnp.where(ii == jj, 1.0, 0.0)
    xor = ii ^ jj
    blocks = [(xor >> sh) == 1 for sh in range(6)]

    def stack(f_ref, b_ref):
        return jnp.concatenate([f_ref[...].reshape(n // 2, CHUNK, HEAD_DIM),
                                b_ref[...].reshape(n // 2, CHUNK, HEAD_DIM)], 0)

    q = stack(qf_ref, qb_ref)
    k = stack(kf_ref, kb_ref)
    v = stack(vf_ref, vb_ref)

    decay = jnp.exp(jnp.where(incl, gc_col - gc_row, NEG_INF))
    e_gc = jnp.exp(gc_col)
    kb = k * b_col
    kq = _bdot(jnp.concatenate([kb, q], 1), k, _NT)
    a_mat = jnp.where(strict, kq[:, :CHUNK] * decay, 0.0)
    intra = jnp.where(incl, kq[:, CHUNK:] * decay, 0.0)
    t = eye - jnp.where(blocks[0], a_mat, 0.0)
    for mask in blocks[1:-1]:
        t = t - _bdot(t, _bdot(jnp.where(mask, a_mat, 0.0), t, _NN), _NN)
    rhs = jnp.concatenate([v * b_col, kb * e_gc], 2)
    half = _bdot(t, rhs, _NN)
    sol = half - _bdot(t, _bdot(jnp.where(blocks[-1], a_mat, 0.0), half, _NN), _NN)
    u, w = sol[:, :, :HEAD_DIM], sol[:, :, HEAD_DIM:]
    state = state_ref[...]
    wq = _bdot(jnp.concatenate([w, q * e_gc], 1), state, _NN)
    v_new = u - wq[:, :CHUNK]
    o = wq[:, CHUNK:] + _bdot(intra, v_new, _NN)
    k_dec = k * jnp.exp(g_last - gc_col)
    state_ref[...] = state * jnp.exp(g_last) + _bdot(k_dec, v_new, _TN)

    for d, o_ref in enumerate((of_ref, ob_ref)):
        for b in range(bb):
            for h in range(N_HEADS):
                o_ref[b, :, h * HEAD_DIM:(h + 1) * HEAD_DIM] = o[(d * bb + b) * N_HEADS + h]


def _dn_scan(q, k, v, gb, gt, n_lat):
    B, _, R, _ = q.shape
    nc = R // CHUNK
    nlc = n_lat // CHUNK
    ncc = nc - nlc
    bb = DN_BATCH if B % DN_BATCH == 0 else 1

    def fwd_chunk(i):
        return jnp.where(i < ncc, nlc + i, i - ncc)

    def bwd_chunk(i):
        return jnp.where(i < ncc, nc - 1 - i, nlc - 1 - (i - ncc))

    hdf = pl.BlockSpec((bb, N_HEADS, CHUNK, HEAD_DIM), lambda b, i: (b, 0, fwd_chunk(i), 0))
    hdb = pl.BlockSpec((bb, N_HEADS, CHUNK, HEAD_DIM), lambda b, i: (b, 0, bwd_chunk(i), 0))
    tokf = lambda w: pl.BlockSpec((bb, CHUNK, w), lambda b, i: (b, fwd_chunk(i), 0))
    tokb = lambda w: pl.BlockSpec((bb, CHUNK, w), lambda b, i: (b, bwd_chunk(i), 0))
    gtf = pl.BlockSpec((bb, 1, 8, CHUNK), lambda b, i: (b, fwd_chunk(i), 0, 0))
    gtb = pl.BlockSpec((bb, 1, 8, CHUNK), lambda b, i: (b, bwd_chunk(i), 0, 0))
    sds = jax.ShapeDtypeStruct
    return pl.pallas_call(
        functools.partial(_dnscan_kernel, bb=bb),
        grid=(B // bb, nc),
        in_specs=[hdf, hdf, hdf, tokf(LANES), gtf, hdb, hdb, hdb, tokb(LANES), gtb],
        out_specs=[tokf(256), tokb(256)],
        out_shape=[sds((B, R, 256), F32), sds((B, R, 256), F32)],
        scratch_shapes=[pltpu.VMEM((2 * N_HEADS * bb, HEAD_DIM, HEAD_DIM), F32)],
        compiler_params=_cparams(("parallel", "arbitrary"), 32),
        name="deltanet_scan",
    )(q, k, v, gb, gt, q, k, v, gb, gt)


def _softmax_pv(s_list, v_list, sink):
    m = functools.reduce(jnp.maximum, [s.max(-1, keepdims=True) for s in s_list])
    if sink is not None:
        m = jnp.maximum(m, sink)
    ps = [jnp.exp(s - m) for s in s_list]
    den = functools.reduce(lambda a, b: a + b, [p.sum(-1, keepdims=True) for p in ps])
    if sink is not None:
        den = den + jnp.exp(sink - m)
    o = functools.reduce(lambda a, b: a + b,
                         [jnp.dot(p.astype(BF16), v, preferred_element_type=F32) for p, v in zip(ps, v_list)])
    return o / den


def _scores(q, k):
    return lax.dot_general(q, k, (((1,), (1,)), ((), ())), preferred_element_type=F32)


def _store_heads(o_ref, o, rows):
    o_ref[0, :rows, :] = jnp.concatenate([o[:rows], o[rows:]], -1).astype(BF16)


def _gattn_kernel(q_ref, k_ref, v_ref, o_ref, *, n_lat):
    j = pl.program_id(2)
    tq = q_ref.shape[2]
    n_ctx = k_ref.shape[2] - n_lat

    @pl.when(j == n_lat // tq)
    def _():
        q = q_ref[0, :, :n_ctx, :].reshape(2 * n_ctx, HEAD_DIM)
        k = k_ref[0, 0, n_lat:, :]
        _store_heads(o_ref, _softmax_pv([_scores(q, k)], [v_ref[0, 0, n_lat:, :]], None), n_ctx)

    @pl.when(j < n_lat // tq)
    def _():
        q = q_ref[0].reshape(2 * tq, HEAD_DIM)
        k = k_ref[0, 0]
        v = v_ref[0, 0]
        rg = 2 * tq // ATTN_ROW_GROUPS
        o = [_softmax_pv([_scores(q[g * rg:(g + 1) * rg], k)], [v], None) for g in range(ATTN_ROW_GROUPS)]
        _store_heads(o_ref, jnp.concatenate(o, 0), tq)


def _wattn_kernel(sink_ref, q_ref, k_ref, v_ref, o_ref, *, n_lat):
    h = pl.program_id(1)
    j = pl.program_id(2)
    tq = q_ref.shape[2]
    n_ctx = k_ref.shape[2] - n_lat
    band = tq + 2 * WINDOW
    kc = k_ref[0, 0, n_lat:, :]
    vc = v_ref[0, 0, n_lat:, :]

    def sink_col(rows):
        row = lax.broadcasted_iota(jnp.int32, (2 * rows, 1), 0)
        return jnp.where(row < rows, sink_ref[2 * h], sink_ref[2 * h + 1])

    @pl.when(j == n_lat // tq)
    def _():
        q = q_ref[0, :, :n_ctx, :].reshape(2 * n_ctx, HEAD_DIM)
        _store_heads(o_ref, _softmax_pv([_scores(q, kc)], [vc], sink_col(n_ctx)), n_ctx)

    @pl.when(j < n_lat // tq)
    def _():
        q = q_ref[0].reshape(2 * tq, HEAD_DIM)
        q0 = j * tq
        start = pl.multiple_of(jnp.clip(q0 - WINDOW, 0, n_lat - band), WINDOW)
        kb = k_ref[0, 0, pl.ds(start, band), :]
        vb = v_ref[0, 0, pl.ds(start, band), :]
        ri = lax.broadcasted_iota(jnp.int32, (2 * tq, band), 0)
        qpos = q0 + jnp.where(ri < tq, ri, ri - tq)
        kpos = start + lax.broadcasted_iota(jnp.int32, (2 * tq, band), 1)
        s_loc = jnp.where(jnp.abs(qpos - kpos) <= WINDOW, _scores(q, kb), NEG_INF)
        _store_heads(o_ref, _softmax_pv([s_loc, _scores(q, kc)], [vb, vc], sink_col(tq)), tq)


def _attention(q, k, v, n_lat, sink=None):
    B, _, R, _ = q.shape
    tq = TQ_GLOBAL if sink is None else TQ_WINDOW
    specs = [pl.BlockSpec((1, 2, tq, HEAD_DIM), lambda b, h, j: (b, h, j, 0)),
             pl.BlockSpec((1, 1, R, HEAD_DIM), lambda b, h, j: (b, h, 0, 0)),
             pl.BlockSpec((1, 1, R, HEAD_DIM), lambda b, h, j: (b, h, 0, 0))]
    args = (q, k, v)
    if sink is None:
        body = functools.partial(_gattn_kernel, n_lat=n_lat)
        name = "global_attention"
    else:
        body = functools.partial(_wattn_kernel, n_lat=n_lat)
        specs = [pl.BlockSpec(memory_space=pltpu.SMEM)] + specs
        args = (sink,) + args
        name = "window_attention"
    return pl.pallas_call(
        body,
        grid=(B, N_KV, pl.cdiv(R, tq)),
        in_specs=specs,
        out_specs=pl.BlockSpec((1, tq, LANES), lambda b, h, j: (b, j, h)),
        out_shape=jax.ShapeDtypeStruct((B, R, N_HEADS * HEAD_DIM), BF16),
        compiler_params=_cparams(("parallel", "parallel", "parallel"), 48),
        name=name,
    )(*args)


def _dft_kernel(z_ref, cl_ref, sl_ref, cc_ref, sc_ref, o_ref, *, n_lat):
    j = pl.program_id(1)
    half = z_ref.shape[2] // 2

    def mix(c, s, z):
        return (jnp.dot(c, z[:, :half], preferred_element_type=F32)
                + jnp.dot(s, z[:, half:], preferred_element_type=F32)).astype(BF16)

    @pl.when(j == n_lat // TQ_DFT)
    def _():
        o_ref[0] = mix(cc_ref[...], sc_ref[...], z_ref[0, n_lat:, :])

    @pl.when(j < n_lat // TQ_DFT)
    def _():
        o_ref[0] = mix(cl_ref[...], sl_ref[...], z_ref[0, :n_lat, :])


def _position_dft(z, cos_lat, msin_lat, cos_ctx, msin_ctx, n_lat):
    B, R, _ = z.shape
    n_ctx = R - n_lat
    assert n_ctx == TQ_DFT
    lat_blocks = n_lat // TQ_DFT
    lat = pl.BlockSpec((TQ_DFT, n_lat), lambda b, j: (jnp.minimum(j, lat_blocks - 1), 0))
    ctx = pl.BlockSpec((n_ctx, n_ctx), lambda b, j: (0, 0))
    return pl.pallas_call(
        functools.partial(_dft_kernel, n_lat=n_lat),
        grid=(B, R // TQ_DFT),
        in_specs=[pl.BlockSpec((1, R, 512), lambda b, j: (b, 0, 0)), lat, lat, ctx, ctx],
        out_specs=pl.BlockSpec((1, TQ_DFT, 256), lambda b, j: (b, j, 0)),
        out_shape=jax.ShapeDtypeStruct((B, R, 256), BF16),
        compiler_params=_cparams(("parallel", "parallel"), 48),
        name="position_dft",
    )(z, cos_lat, msin_lat, cos_ctx, msin_ctx)


def _outproj_kernel(of_ref, ob_ref, z_ref, ga_ref, wa_ref, ft_ref, x_ref, mod_ref, ng_ref, ones_ref, w_ref,
                    o_ref):
    o = of_ref[0] + ob_ref[0]
    z = z_ref[0]
    ones_bd = ones_ref[...]
    acc = None
    for s in range(2):
        t = o[:, s * LANES:(s + 1) * LANES]
        y = t * lax.rsqrt(_head_sumsq(t, ones_bd) * (1.0 / HEAD_DIM) + EPS) * ng_ref[...]
        y = (y * _silu(z[:, s * LANES:(s + 1) * LANES])).astype(BF16)
        part = jnp.dot(y, w_ref[s * LANES:(s + 1) * LANES, :], preferred_element_type=F32)
        acc = part if acc is None else acc + part
    for n, ref in enumerate((ga_ref, wa_ref, ft_ref)):
        acc = acc + jnp.dot(ref[0], w_ref[256 * (n + 1):256 * (n + 2), :], preferred_element_type=F32)
    o_ref[0] = x_ref[0] + mod_ref[0, 2:3, :] * acc


def _out_projection(of, ob, z, y_ga, y_wa, y_ft, xa, mod_l, ng, ones_bd, w_out, n_lat, rows_out):
    B = xa.shape[0]
    const2 = lambda b, j: (0, 0)
    tok = lambda w: pl.BlockSpec((1, TM, w), lambda b, j: (b, j, 0))
    return pl.pallas_call(
        _outproj_kernel,
        grid=(B, pl.cdiv(rows_out, TM)),
        in_specs=[tok(256)] * 6 + [
            tok(D_MODEL),
            _mod_spec(n_lat // TM),
            pl.BlockSpec((1, LANES), const2),
            pl.BlockSpec((LANES, LANES), const2),
            pl.BlockSpec((D_MODEL, D_MODEL), const2)],
        out_specs=tok(D_MODEL),
        out_shape=jax.ShapeDtypeStruct((B, rows_out, D_MODEL), F32),
        compiler_params=_cparams(("parallel", "parallel"), 40),
        name="out_projection",
    )(of, ob, z, y_ga, y_wa, y_ft, xa, mod_l, ng, ones_bd, w_out)


def _ffn_kernel(x_ref, mod_ref, g2_ref, wg_ref, wu_ref, wd_ref, o_ref):
    x = x_ref[0]
    ms = jnp.mean(x * x, -1, keepdims=True)
    y = x * lax.rsqrt(ms + EPS) * g2_ref[...]
    h = (y * (1.0 + mod_ref[0, 4:5, :]) + mod_ref[0, 3:4, :]).astype(BF16)
    acc = None
    for c in range(FFN_HIDDEN // FFN_CHUNK):
        cs = slice(c * FFN_CHUNK, (c + 1) * FFN_CHUNK)
        g = jnp.dot(h, wg_ref[:, cs], preferred_element_type=F32)
        u = jnp.dot(h, wu_ref[:, cs], preferred_element_type=F32)
        a = (_silu(g) * u).astype(BF16)
        part = jnp.dot(a, wd_ref[cs, :], preferred_element_type=F32)
        acc = part if acc is None else acc + part
    o_ref[0] = x + mod_ref[0, 5:6, :] * acc


def _ffn(xa, mod_l, g2, wg, wu, wd, n_lat):
    B, rows, _ = xa.shape
    const2 = lambda b, j: (0, 0)
    tok = pl.BlockSpec((1, TM, D_MODEL), lambda b, j: (b, j, 0))
    resident = lambda shape: pl.BlockSpec(shape, const2, pipeline_mode=pl.Buffered(1))
    return pl.pallas_call(
        _ffn_kernel,
        grid=(B, pl.cdiv(rows, TM)),
        in_specs=[tok,
                  _mod_spec(n_lat // TM),
                  pl.BlockSpec((1, D_MODEL), const2),
                  resident((D_MODEL, FFN_HIDDEN)),
                  resident((D_MODEL, FFN_HIDDEN)),
                  resident((FFN_HIDDEN, D_MODEL))],
        out_specs=tok,
        out_shape=jax.ShapeDtypeStruct((B, rows, D_MODEL), F32),
        compiler_params=_cparams(("parallel", "parallel"), 48),
        name="swiglu_ffn",
    )(xa, mod_l, g2, wg, wu, wd)


def _rope_tables(n_lat, n_rows):
    t = jnp.arange(n_lat, dtype=jnp.int32)
    row = (t // GRID_W).astype(F32)
    col = (t % GRID_W).astype(F32)
    n_freq = HEAD_DIM // 4
    inv_freq = ROPE_THETA ** (-jnp.arange(n_freq, dtype=F32) / n_freq)
    ang = jnp.concatenate([row[:, None] * inv_freq, col[:, None] * inv_freq], -1)
    cos = jnp.tile(jnp.cos(ang), (1, 4))
    sin = jnp.sin(ang)
    sin = jnp.concatenate([-sin, sin, -sin, sin], -1)
    cos = jnp.concatenate([cos, jnp.ones((n_rows - n_lat, LANES), F32)], 0)
    sin = jnp.concatenate([sin, jnp.zeros((n_rows - n_lat, LANES), F32)], 0)
    return cos, sin


def _dft_matrices(n, n_chan):
    scale = 1.0 / math.sqrt(n * n_chan)
    p = 64 if n % 64 == 0 and n > 64 else 1
    q = n // p
    col = jnp.arange(n, dtype=jnp.int32)[None, :]

    def angles(rows, period):
        return ((jnp.arange(rows, dtype=jnp.int32)[:, None] * col) % period).astype(F32) * (2.0 * math.pi / period)

    ang_a = angles(p, p)
    ang_b = angles(q, n)
    ca, sa = jnp.cos(ang_a)[:, None, :], jnp.sin(ang_a)[:, None, :]
    cb, sb = jnp.cos(ang_b)[None, :, :] * scale, jnp.sin(ang_b)[None, :, :] * scale
    cos = (ca * cb - sa * sb).reshape(n, n)
    msin = (-(sa * cb + ca * sb)).reshape(n, n)
    return cos.astype(BF16), msin.astype(BF16)


def _channel_dft_weights():
    n = HEAD_DIM
    idx = np.arange(n)
    ang = 2.0 * np.pi * ((idx[:, None] * idx[None, :]) % n) / n
    eye = np.eye(4)
    w = np.concatenate([np.kron(eye, np.cos(ang)), np.kron(eye, np.sin(ang))], 1)
    return jnp.asarray(w, F32).astype(BF16)


def kernel(x, c, ctx, c_ctx, norm1_g, norm2_g, w_ada, b_ada, w_in, dn_conv_w, dn_A_log, dn_dt_bias, dn_norm_g,
           ga_q_norm, ga_k_norm, wa_q_norm, wa_k_norm, wa_sink, w_out, w_ffn_gate, w_ffn_up, w_ffn_down):
    B, S, _ = x.shape
    L = ctx.shape[1]
    depth = w_in.shape[0]
    R = S + L
    assert S % TM == 0 and L <= TM and S % TQ_WINDOW == 0 and L == TQ_GLOBAL and B <= MOD_ROWS // 2

    cc = jnp.zeros((MOD_ROWS, D_MODEL), F32).at[:B].set(c).at[MOD_ROWS // 2].set(c_ctx)
    mod = _modulation(cc, w_ada, b_ada).reshape(depth, MOD_ROWS, N_MOD, D_MODEL)

    cos_t, sin_t = _rope_tables(S, pl.cdiv(R, TM) * TM)
    cos_lat, msin_lat = _dft_matrices(S, HEAD_DIM)
    cos_ctx, msin_ctx = _dft_matrices(L, HEAD_DIM)
    ftw = _channel_dft_weights()
    ones_bd = jnp.asarray(np.kron(np.eye(2), np.ones((HEAD_DIM, HEAD_DIM))), F32).astype(BF16)
    tile2 = lambda g: jnp.tile(g, 2)

    xa = jnp.concatenate([x, ctx], 1)
    for l in range(depth):
        last = l == depth - 1
        w = w_in[l]
        w_r = jnp.concatenate([w[:, :1024], w[:, 1040:2320], w[:, 1024:1040],
                               jnp.zeros((D_MODEL, C_END - 2320), F32)], 1).astype(BF16)
        gains = jnp.zeros((8, LANES), F32)
        for n, g in enumerate((ga_q_norm[l], ga_k_norm[l], wa_q_norm[l], wa_k_norm[l])):
            gains = gains.at[n].set(tile2(g))
        (qkv, z, ab, gaq, gak, gav, waq, wak, wav, ftz) = _in_projection(
            xa, mod[l], norm1_g[l][None, :], w_r, cos_t, sin_t, gains, ones_bd, ftw, S)

        conv_w8 = jnp.zeros((8, 768), F32).at[:3].set(dn_conv_w[l])
        neg_a = jnp.zeros((1, LANES), F32).at[0, :8].set(-jnp.exp(dn_A_log[l].astype(F32)).reshape(-1))
        dt_b = jnp.zeros((1, LANES), F32).at[0, :8].set(dn_dt_bias[l].astype(F32).reshape(-1))
        dq, dk, dv, gb = _dn_prep(qkv, ab, conv_w8, neg_a, dt_b, ones_bd, S)
        gt = gb[:, :, :8].reshape(B, R // CHUNK, CHUNK, 8).transpose(0, 1, 3, 2)
        o_f, o_b = _dn_scan(dq, dk, dv, gb, gt, S)

        y_ga = _attention(gaq, gak, gav, S)
        y_wa = _attention(waq, wak, wav, S, sink=wa_sink[l].astype(F32))
        y_ft = _position_dft(ftz, cos_lat, msin_lat, cos_ctx, msin_ctx, S)

        rows_out = S if last else R
        xa = _out_projection(o_f, o_b, z, y_ga, y_wa, y_ft, xa, mod[l], tile2(dn_norm_g[l])[None, :], ones_bd,
                             w_out[l].astype(BF16), S, rows_out)
        xa = _ffn(xa, mod[l], norm2_g[l][None, :], w_ffn_gate[l].astype(BF16), w_ffn_up[l].astype(BF16),
                  w_ffn_down[l].astype(BF16), S)
    return xa
```

```python
import functools
import math

import jax
import jax.numpy as jnp
import numpy as np
from jax import lax
from jax.experimental import pallas as pl
from jax.experimental.pallas import tpu as pltpu

F32 = jnp.float32
BF16 = jnp.bfloat16
HIGHEST = lax.Precision.HIGHEST

D_MODEL = 1024
HEAD_DIM = 64
N_HEADS = 4
N_KV = 2
GRID_W = 64
CHUNK = 64
WINDOW = 128
ROPE_THETA = 10000.0
FFN_HIDDEN = 2816
N_MOD = 6
EPS = 1e-6
NEG_INF = -1e30

LANES = 128
TM = 512
TQ_GLOBAL = 256
TQ_WINDOW = 512
TQ_DFT = 256
MOD_ROWS = 16
FFN_CHUNK = 256
DN_BATCH = 4
LOG2E = math.log2(math.e)

C_QKV, C_Z, C_GA, C_WA, C_FT, C_AB, C_END = 0, 768, 1024, 1536, 2048, 2304, 2432


def _cparams(sem, vmem_mb):
    return pltpu.CompilerParams(dimension_semantics=sem, vmem_limit_bytes=vmem_mb * 1024 * 1024)


def _silu(x):
    return x * (1.0 / (1.0 + jnp.exp(-x)))


def _head_sumsq(t, ones_bd):
    return jnp.dot((t * t).astype(BF16), ones_bd, preferred_element_type=F32)


def _mod_spec(n_lat_blocks):
    ctx_row = MOD_ROWS // 2
    return pl.BlockSpec((1, N_MOD, D_MODEL), lambda b, j: (jnp.where(j == n_lat_blocks, ctx_row, b), 0, 0))


def _mod_kernel(c_ref, w_ref, b_ref, o_ref):
    a = _silu(c_ref[...])
    o_ref[0] = jnp.dot(a, w_ref[0], preferred_element_type=F32, precision=HIGHEST) + b_ref[0]


def _modulation(cc, w_ada, b_ada):
    depth = w_ada.shape[0]
    return pl.pallas_call(
        _mod_kernel,
        grid=(depth, N_MOD),
        in_specs=[pl.BlockSpec((MOD_ROWS, D_MODEL), lambda l, n: (0, 0)),
                  pl.BlockSpec((1, D_MODEL, D_MODEL), lambda l, n: (l, 0, n)),
                  pl.BlockSpec((1, 1, D_MODEL), lambda l, n: (l, 0, n))],
        out_specs=pl.BlockSpec((1, MOD_ROWS, D_MODEL), lambda l, n: (l, 0, n)),
        out_shape=jax.ShapeDtypeStruct((depth, MOD_ROWS, N_MOD * D_MODEL), F32),
        compiler_params=_cparams(("parallel", "parallel"), 32),
        name="modulation",
    )(cc, w_ada, b_ada.reshape(depth, 1, N_MOD * D_MODEL))


def _inproj_kernel(x_ref, mod_ref, g1_ref, w_ref, cos_ref, sin_ref, gains_ref, ones_ref, ftw_ref,
                   qkv_ref, z_ref, ab_ref, gaq_ref, gak_ref, gav_ref, waq_ref, wak_ref, wav_ref, ft_ref):
    x = x_ref[0]
    ms = jnp.mean(x * x, -1, keepdims=True)
    y = x * lax.rsqrt(ms + EPS) * g1_ref[...]
    h = (y * (1.0 + mod_ref[0, 1:2, :]) + mod_ref[0, 0:1, :]).astype(BF16)

    def proj(c0, c1):
        return jnp.dot(h, w_ref[:, c0:c1], preferred_element_type=F32)

    qkv_ref[0] = proj(C_QKV, C_Z)
    z_ref[0] = proj(C_Z, C_GA)
    ab_ref[0] = proj(C_AB, C_END)

    ones_bd = ones_ref[...]
    cos_t = cos_ref[...]
    sin_t = sin_ref[...]
    lane = lax.broadcasted_iota(jnp.int32, cos_t.shape, 1)
    first_half = (lane & (HEAD_DIM // 2)) == 0

    def norm_rope(t, gain_row, scale):
        yv = t * lax.rsqrt(_head_sumsq(t, ones_bd) * (1.0 / HEAD_DIM) + EPS) * gain_row
        partner = jnp.where(first_half, pltpu.roll(yv, LANES - HEAD_DIM // 2, 1), pltpu.roll(yv, HEAD_DIM // 2, 1))
        return (yv * cos_t + partner * sin_t) * scale

    def heads_out(ref, t, n):
        for i in range(n):
            ref[0, i] = t[:, i * HEAD_DIM:(i + 1) * HEAD_DIM].astype(BF16)

    for c0, q_ref, k_ref, v_ref, gq, gk in ((C_GA, gaq_ref, gak_ref, gav_ref, 0, 1),
                                            (C_WA, waq_ref, wak_ref, wav_ref, 2, 3)):
        p = proj(c0, c0 + 512)
        scale = HEAD_DIM ** -0.5 * LOG2E
        q = jnp.concatenate([norm_rope(p[:, 0:128], gains_ref[gq:gq + 1, :], scale),
                             norm_rope(p[:, 128:256], gains_ref[gq:gq + 1, :], scale)], -1)
        k = norm_rope(p[:, 256:384], gains_ref[gk:gk + 1, :], 1.0)
        heads_out(q_ref, q, N_HEADS)
        heads_out(k_ref, k, N_KV)
        heads_out(v_ref, p[:, 384:512], N_KV)

    pf = proj(C_FT, C_AB).astype(BF16)
    ft_ref[0] = jnp.dot(pf, ftw_ref[...], preferred_element_type=F32).astype(BF16)


def _in_projection(xa, mod_l, g1, w_in_r, cos_t, sin_t, gains, ones_bd, ftw, layer, n_lat):
    B, R, _ = xa.shape
    nb = pl.cdiv(R, TM)
    const2 = lambda b, j: (0, 0)
    tok = lambda w: pl.BlockSpec((1, TM, w), lambda b, j: (b, j, 0))
    hd = lambda n: pl.BlockSpec((1, n, TM, HEAD_DIM), lambda b, j: (b, 0, j, 0))
    sds = jax.ShapeDtypeStruct
    return pl.pallas_call(
        _inproj_kernel,
        grid=(B, nb),
        in_specs=[tok(D_MODEL),
                  _mod_spec(n_lat // TM),
                  pl.BlockSpec((1, D_MODEL), const2),
                  _layer_weight_spec(layer, (D_MODEL, C_END)),
                  pl.BlockSpec((TM, LANES), lambda b, j: (j, 0)),
                  pl.BlockSpec((TM, LANES), lambda b, j: (j, 0)),
                  pl.BlockSpec((8, LANES), const2),
                  pl.BlockSpec((LANES, LANES), const2),
                  pl.BlockSpec((256, 512), const2)],
        out_specs=[tok(768), tok(256), tok(LANES),
                   hd(N_HEADS), hd(N_KV), hd(N_KV), hd(N_HEADS), hd(N_KV), hd(N_KV),
                   tok(512)],
        out_shape=[sds((B, R, 768), F32), sds((B, R, 256), F32), sds((B, R, LANES), F32),
                   sds((B, N_HEADS, R, HEAD_DIM), BF16), sds((B, N_KV, R, HEAD_DIM), BF16),
                   sds((B, N_KV, R, HEAD_DIM), BF16),
                   sds((B, N_HEADS, R, HEAD_DIM), BF16), sds((B, N_KV, R, HEAD_DIM), BF16),
                   sds((B, N_KV, R, HEAD_DIM), BF16),
                   sds((B, R, 512), BF16)],
        compiler_params=_cparams(("parallel", "parallel"), 48),
        name="in_projection",
    )(xa, mod_l, g1, w_in_r, cos_t, sin_t, gains, ones_bd, ftw)


def _dnprep_kernel(cur_ref, prev_ref, next_ref, ab_ref, cw_ref, na_ref, dtb_ref, ones_ref,
                   q_ref, k_ref, v_ref, gb_ref, *, lat_blocks, n_ctx):
    j = pl.program_id(1)
    cur = cur_ref[0]
    tp = cur.shape[0]
    row = lax.broadcasted_iota(jnp.int32, cur.shape, 0)
    is_ctx = j == lat_blocks
    has_prev = jnp.where((j == 0) | is_ctx, 0.0, 1.0)
    has_next = jnp.where(j >= lat_blocks - 1, 0.0, 1.0)
    last_row = jnp.where(is_ctx, n_ctx - 1, tp - 1)
    prev_row = prev_ref[0, 7:8, :] * has_prev
    next_row = next_ref[0, 0:1, :] * has_next
    up = jnp.where(row == 0, prev_row, pltpu.roll(cur, 1, 0))
    dn = jnp.where(row == last_row, next_row, pltpu.roll(cur, tp - 1, 0))
    a = _silu(up * cw_ref[0:1, :] + cur * cw_ref[1:2, :] + dn * cw_ref[2:3, :])

    ones_bd = ones_ref[...]

    def heads_out(ref, t, first):
        for i in range(2):
            ref[0, first + i] = t[:, i * HEAD_DIM:(i + 1) * HEAD_DIM]

    for s in range(2):
        tq = a[:, s * LANES:(s + 1) * LANES]
        heads_out(q_ref, tq * lax.rsqrt(_head_sumsq(tq, ones_bd) + EPS) * HEAD_DIM ** -0.5, 2 * s)
        tk = a[:, 256 + s * LANES:256 + (s + 1) * LANES]
        heads_out(k_ref, tk * lax.rsqrt(_head_sumsq(tk, ones_bd) + EPS), 2 * s)
        heads_out(v_ref, a[:, 512 + s * LANES:512 + (s + 1) * LANES], 2 * s)

    ab = ab_ref[0]
    t = ab + dtb_ref[...]
    softplus = jnp.maximum(t, 0.0) + jnp.log(1.0 + jnp.exp(-jnp.abs(t)))
    g = na_ref[...] * softplus
    beta = 1.0 / (1.0 + jnp.exp(-ab))
    lane = lax.broadcasted_iota(jnp.int32, (CHUNK, LANES), 1)
    ii = lax.broadcasted_iota(jnp.int32, (CHUNK, CHUNK), 0)
    jj = lax.broadcasted_iota(jnp.int32, (CHUNK, CHUNK), 1)
    prefix = jnp.where(ii >= jj, 1.0, 0.0)
    suffix = jnp.where(ii <= jj, 1.0, 0.0)
    for c in range(tp // CHUNK):
        rows = slice(c * CHUNK, (c + 1) * CHUNK)
        gc = g[rows]
        fwd = jnp.dot(prefix, gc, preferred_element_type=F32, precision=HIGHEST)
        bwd = jnp.dot(suffix, gc, preferred_element_type=F32, precision=HIGHEST)
        gb_ref[0, rows, :] = jnp.where(lane < N_HEADS, fwd, jnp.where(lane < 2 * N_HEADS, bwd, beta[rows]))


def _dn_prep(qkv, ab, conv_w8, neg_a, dt_b, ones_bd, n_lat):
    B, R, _ = qkv.shape
    nb = pl.cdiv(R, TM)
    hb = TM // 8
    const2 = lambda b, j: (0, 0)
    tok = lambda w: pl.BlockSpec((1, TM, w), lambda b, j: (b, j, 0))
    hd = pl.BlockSpec((1, N_HEADS, TM, HEAD_DIM), lambda b, j: (b, 0, j, 0))
    sds = jax.ShapeDtypeStruct
    return pl.pallas_call(
        functools.partial(_dnprep_kernel, lat_blocks=n_lat // TM, n_ctx=R - n_lat),
        grid=(B, nb),
        in_specs=[tok(768),
                  pl.BlockSpec((1, 8, 768), lambda b, j: (b, jnp.maximum(j * hb - 1, 0), 0)),
                  pl.BlockSpec((1, 8, 768), lambda b, j: (b, jnp.minimum((j + 1) * hb, R // 8 - 1), 0)),
                  tok(LANES),
                  pl.BlockSpec((8, 768), const2),
                  pl.BlockSpec((1, LANES), const2),
                  pl.BlockSpec((1, LANES), const2),
                  pl.BlockSpec((LANES, LANES), const2)],
        out_specs=[hd, hd, hd, tok(LANES)],
        out_shape=[sds((B, N_HEADS, R, HEAD_DIM), F32)] * 3 + [sds((B, R, LANES), F32)],
        compiler_params=_cparams(("parallel", "parallel"), 32),
        name="deltanet_prep",
    )(qkv, qkv, qkv, ab, conv_w8, neg_a, dt_b, ones_bd)


def _bdot(a, b, dims):
    return lax.dot_general(a.astype(BF16), b.astype(BF16), dims, preferred_element_type=F32)


_NN = (((2,), (1,)), ((0,), (0,)))
_NT = (((2,), (2,)), ((0,), (0,)))
_TN = (((1,), (1,)), ((0,), (0,)))


def _dnscan_kernel(qf_ref, kf_ref, vf_ref, gbf_ref, gtf_ref, qb_ref, kb_ref, vb_ref, gbb_ref, gtb_ref,
                   of_ref, ob_ref, state_ref, *, bb):
    i = pl.program_id(1)
    n = 2 * N_HEADS * bb

    @pl.when(i == 0)
    def _():
        state_ref[...] = jnp.zeros_like(state_ref)

    gc_cols, b_cols, gc_rows, g_lasts = [], [], [], []
    for d, (gb_ref, gt_ref) in enumerate(((gbf_ref, gtf_ref), (gbb_ref, gtb_ref))):
        last = CHUNK - 1 if d == 0 else 0
        for b in range(bb):
            gb = gb_ref[b]
            gt = gt_ref[b, 0]
            for h in range(N_HEADS):
                c = d * N_HEADS + h
                gc_cols.append(gb[:, c:c + 1])
                g_lasts.append(gb[last:last + 1, c:c + 1])
                b_cols.append(gb[:, 2 * N_HEADS + c:2 * N_HEADS + c + 1])
                gc_rows.append(gt[c:c + 1, :])
    gc_col = jnp.stack(gc_cols)
    b_col = jnp.stack(b_cols)
    gc_row = jnp.stack(gc_rows)
    g_last = jnp.stack(g_lasts)

    shape3 = (n, CHUNK, CHUNK)
    backward = lax.broadcasted_iota(jnp.int32, shape3, 0) >= n // 2
    i3 = lax.broadcasted_iota(jnp.int32, shape3, 1)
    j3 = lax.broadcasted_iota(jnp.int32, shape3, 2)
    late = jnp.where(backward, j3, i3)
    early = jnp.where(backward, i3, j3)
    incl = late >= early
    strict = late > early
    ii = lax.broadcasted_iota(jnp.int32, (CHUNK, CHUNK), 0)
    jj = lax.broadcasted_iota(jnp.int32, (CHUNK, CHUNK), 1)
    eye = jnp.where(ii == jj, 1.0, 0.0)
    xor = ii ^ jj
    blocks = [(xor >> sh) == 1 for sh in range(6)]

    def stack(f_ref, b_ref):
        return jnp.concatenate([f_ref[...].reshape(n // 2, CHUNK, HEAD_DIM),
                                b_ref[...].reshape(n // 2, CHUNK, HEAD_DIM)], 0)

    q = stack(qf_ref, qb_ref)
    k = stack(kf_ref, kb_ref)
    v = stack(vf_ref, vb_ref)

    decay = jnp.exp(jnp.where(incl, gc_col - gc_row, NEG_INF))
    e_gc = jnp.exp(gc_col)
    kb = k * b_col
    kq = _bdot(jnp.concatenate([kb, q], 1), k, _NT)
    a_mat = jnp.where(strict, kq[:, :CHUNK] * decay, 0.0)
    intra = jnp.where(incl, kq[:, CHUNK:] * decay, 0.0)
    t = eye - jnp.where(blocks[0], a_mat, 0.0)
    for mask in blocks[1:-1]:
        t = t - _bdot(t, _bdot(jnp.where(mask, a_mat, 0.0), t, _NN), _NN)
    rhs = jnp.concatenate([v * b_col, kb * e_gc], 2)
    half = _bdot(t, rhs, _NN)
    sol = half - _bdot(t, _bdot(jnp.where(blocks[-1], a_mat, 0.0), half, _NN), _NN)
    u, w = sol[:, :, :HEAD_DIM], sol[:, :, HEAD_DIM:]
    state = state_ref[...]
    wq = _bdot(jnp.concatenate([w, q * e_gc], 1), state, _NN)
    v_new = u - wq[:, :CHUNK]
    o = wq[:, CHUNK:] + _bdot(intra, v_new, _NN)
    k_dec = k * jnp.exp(g_last - gc_col)
    state_ref[...] = state * jnp.exp(g_last) + _bdot(k_dec, v_new, _TN)

    for d, o_ref in enumerate((of_ref, ob_ref)):
        for b in range(bb):
            for h in range(N_HEADS):
                o_ref[b, :, h * HEAD_DIM:(h + 1) * HEAD_DIM] = o[(d * bb + b) * N_HEADS + h]


def _dn_scan(q, k, v, gb, gt, n_lat):
    B, _, R, _ = q.shape
    nc = R // CHUNK
    nlc = n_lat // CHUNK
    ncc = nc - nlc
    bb = DN_BATCH if B % DN_BATCH == 0 else 1

    def fwd_chunk(i):
        return jnp.where(i < ncc, nlc + i, i - ncc)

    def bwd_chunk(i):
        return jnp.where(i < ncc, nc - 1 - i, nlc - 1 - (i - ncc))

    hdf = pl.BlockSpec((bb, N_HEADS, CHUNK, HEAD_DIM), lambda b, i: (b, 0, fwd_chunk(i), 0))
    hdb = pl.BlockSpec((bb, N_HEADS, CHUNK, HEAD_DIM), lambda b, i: (b, 0, bwd_chunk(i), 0))
    tokf = lambda w: pl.BlockSpec((bb, CHUNK, w), lambda b, i: (b, fwd_chunk(i), 0))
    tokb = lambda w: pl.BlockSpec((bb, CHUNK, w), lambda b, i: (b, bwd_chunk(i), 0))
    gtf = pl.BlockSpec((bb, 1, 8, CHUNK), lambda b, i: (b, fwd_chunk(i), 0, 0))
    gtb = pl.BlockSpec((bb, 1, 8, CHUNK), lambda b, i: (b, bwd_chunk(i), 0, 0))
    sds = jax.ShapeDtypeStruct
    return pl.pallas_call(
        functools.partial(_dnscan_kernel, bb=bb),
        grid=(B // bb, nc),
        in_specs=[hdf, hdf, hdf, tokf(LANES), gtf, hdb, hdb, hdb, tokb(LANES), gtb],
        out_specs=[tokf(256), tokb(256)],
        out_shape=[sds((B, R, 256), F32), sds((B, R, 256), F32)],
        scratch_shapes=[pltpu.VMEM((2 * N_HEADS * bb, HEAD_DIM, HEAD_DIM), F32)],
        compiler_params=_cparams(("parallel", "arbitrary"), 32),
        name="deltanet_scan",
    )(q, k, v, gb, gt, q, k, v, gb, gt)


def _softmax_pv(s_list, v_list, sink):
    m = functools.reduce(jnp.maximum, [s.max(-1, keepdims=True) for s in s_list])
    if sink is not None:
        m = jnp.maximum(m, sink)
    ps = [jnp.exp2(s - m) for s in s_list]
    den = functools.reduce(lambda a, b: a + b, [p.sum(-1, keepdims=True) for p in ps])
    if sink is not None:
        den = den + jnp.exp2(sink - m)
    o = functools.reduce(lambda a, b: a + b,
                         [jnp.dot(p.astype(BF16), v, preferred_element_type=F32) for p, v in zip(ps, v_list)])
    return o / den


def _scores(q, k):
    return lax.dot_general(q, k, (((1,), (1,)), ((), ())), preferred_element_type=F32)


def _store_heads(o_ref, heads, rows):
    o_ref[0, :rows, :] = jnp.concatenate(heads, -1).astype(BF16)


def _gattn_kernel(q_ref, k_ref, v_ref, o_ref, *, n_lat):
    j = pl.program_id(2)
    tq = q_ref.shape[2]
    n_ctx = k_ref.shape[2] - n_lat

    @pl.when(j == n_lat // tq)
    def _():
        k = k_ref[0, 0, n_lat:, :]
        v = v_ref[0, 0, n_lat:, :]
        _store_heads(o_ref, [_softmax_pv([_scores(q_ref[0, g, :n_ctx, :], k)], [v], None) for g in range(2)],
                     n_ctx)

    @pl.when(j < n_lat // tq)
    def _():
        k = k_ref[0, 0]
        v = v_ref[0, 0]
        _store_heads(o_ref, [_softmax_pv([_scores(q_ref[0, g], k)], [v], None) for g in range(2)], tq)


def _wattn_kernel(sink_ref, bias_ref, q_ref, k_ref, v_ref, o_ref, *, n_lat):
    h = pl.program_id(1)
    j = pl.program_id(2)
    tq = q_ref.shape[2]
    n_ctx = k_ref.shape[2] - n_lat
    band = tq + 2 * WINDOW
    kc = k_ref[0, 0, n_lat:, :]
    vc = v_ref[0, 0, n_lat:, :]
    sinks = [sink_ref[2 * h + g] * LOG2E for g in range(2)]

    @pl.when(j == n_lat // tq)
    def _():
        _store_heads(o_ref, [_softmax_pv([_scores(q_ref[0, g, :n_ctx, :], kc)], [vc], sinks[g])
                             for g in range(2)], n_ctx)

    @pl.when(j < n_lat // tq)
    def _():
        q0 = j * tq
        start = pl.multiple_of(jnp.clip(q0 - WINDOW, 0, n_lat - band), WINDOW)
        kb = k_ref[0, 0, pl.ds(start, band), :]
        vb = v_ref[0, 0, pl.ds(start, band), :]
        bias = bias_ref[(q0 - start) // WINDOW]
        heads = []
        for g in range(2):
            q = q_ref[0, g]
            heads.append(_softmax_pv([_scores(q, kb) + bias, _scores(q, kc)], [vb, vc], sinks[g]))
        _store_heads(o_ref, heads, tq)


def _window_bias(tq):
    band = tq + 2 * WINDOW
    off = jnp.arange(3, dtype=jnp.int32)[:, None, None] * WINDOW
    r = jnp.arange(tq, dtype=jnp.int32)[None, :, None]
    c = jnp.arange(band, dtype=jnp.int32)[None, None, :]
    return jnp.where(jnp.abs(r + off - c) <= WINDOW, 0.0, NEG_INF).astype(F32)


def _attention(q, k, v, n_lat, sink=None):
    B, _, R, _ = q.shape
    tq = TQ_GLOBAL if sink is None else TQ_WINDOW
    specs = [pl.BlockSpec((1, 2, tq, HEAD_DIM), lambda b, h, j: (b, h, j, 0)),
             pl.BlockSpec((1, 1, R, HEAD_DIM), lambda b, h, j: (b, h, 0, 0)),
             pl.BlockSpec((1, 1, R, HEAD_DIM), lambda b, h, j: (b, h, 0, 0))]
    args = (q, k, v)
    if sink is None:
        body = functools.partial(_gattn_kernel, n_lat=n_lat)
        name = "global_attention"
    else:
        body = functools.partial(_wattn_kernel, n_lat=n_lat)
        bias = _window_bias(tq)
        specs = [pl.BlockSpec(memory_space=pltpu.SMEM),
                 pl.BlockSpec(bias.shape, lambda b, h, j: (0, 0, 0), pipeline_mode=pl.Buffered(1))] + specs
        args = (sink, bias) + args
        name = "window_attention"
    return pl.pallas_call(
        body,
        grid=(B, N_KV, pl.cdiv(R, tq)),
        in_specs=specs,
        out_specs=pl.BlockSpec((1, tq, LANES), lambda b, h, j: (b, j, h)),
        out_shape=jax.ShapeDtypeStruct((B, R, N_HEADS * HEAD_DIM), BF16),
        compiler_params=_cparams(("parallel", "parallel", "parallel"), 48),
        name=name,
    )(*args)


def _dft_kernel(z_ref, cl_ref, sl_ref, cc_ref, sc_ref, o_ref, *, n_lat):
    j = pl.program_id(1)
    half = z_ref.shape[2] // 2

    def mix(c, s, z):
        return (jnp.dot(c, z[:, :half], preferred_element_type=F32)
                + jnp.dot(s, z[:, half:], preferred_element_type=F32)).astype(BF16)

    @pl.when(j == n_lat // TQ_DFT)
    def _():
        o_ref[0] = mix(cc_ref[...], sc_ref[...], z_ref[0, n_lat:, :])

    @pl.when(j < n_lat // TQ_DFT)
    def _():
        o_ref[0] = mix(cl_ref[...], sl_ref[...], z_ref[0, :n_lat, :])


def _position_dft(z, cos_lat, msin_lat, cos_ctx, msin_ctx, n_lat):
    B, R, _ = z.shape
    n_ctx = R - n_lat
    assert n_ctx == TQ_DFT
    lat_blocks = n_lat // TQ_DFT
    lat = pl.BlockSpec((TQ_DFT, n_lat), lambda b, j: (jnp.minimum(j, lat_blocks - 1), 0))
    ctx = pl.BlockSpec((n_ctx, n_ctx), lambda b, j: (0, 0))
    return pl.pallas_call(
        functools.partial(_dft_kernel, n_lat=n_lat),
        grid=(B, R // TQ_DFT),
        in_specs=[pl.BlockSpec((1, R, 512), lambda b, j: (b, 0, 0)), lat, lat, ctx, ctx],
        out_specs=pl.BlockSpec((1, TQ_DFT, 256), lambda b, j: (b, j, 0)),
        out_shape=jax.ShapeDtypeStruct((B, R, 256), BF16),
        compiler_params=_cparams(("parallel", "parallel"), 48),
        name="position_dft",
    )(z, cos_lat, msin_lat, cos_ctx, msin_ctx)


def _block_tail_kernel(of_ref, ob_ref, z_ref, ga_ref, wa_ref, ft_ref, x_ref, mod_ref, ng_ref, ones_ref, w_ref,
                       g2_ref, wg_ref, wu_ref, wd_ref, o_ref):
    o = of_ref[0] + ob_ref[0]
    z = z_ref[0]
    ones_bd = ones_ref[...]
    acc = None
    for s in range(2):
        t = o[:, s * LANES:(s + 1) * LANES]
        y = t * lax.rsqrt(_head_sumsq(t, ones_bd) * (1.0 / HEAD_DIM) + EPS) * ng_ref[...]
        y = (y * _silu(z[:, s * LANES:(s + 1) * LANES])).astype(BF16)
        part = jnp.dot(y, w_ref[s * LANES:(s + 1) * LANES, :], preferred_element_type=F32)
        acc = part if acc is None else acc + part
    for n, ref in enumerate((ga_ref, wa_ref, ft_ref)):
        acc = acc + jnp.dot(ref[0], w_ref[256 * (n + 1):256 * (n + 2), :], preferred_element_type=F32)
    x = x_ref[0] + mod_ref[0, 2:3, :] * acc

    ms = jnp.mean(x * x, -1, keepdims=True)
    y = x * lax.rsqrt(ms + EPS) * g2_ref[...]
    h = (y * (1.0 + mod_ref[0, 4:5, :]) + mod_ref[0, 3:4, :]).astype(BF16)
    acc = None
    for c in range(FFN_HIDDEN // FFN_CHUNK):
        cs = slice(c * FFN_CHUNK, (c + 1) * FFN_CHUNK)
        g = jnp.dot(h, wg_ref[:, cs], preferred_element_type=F32)
        u = jnp.dot(h, wu_ref[:, cs], preferred_element_type=F32)
        a = (_silu(g) * u).astype(BF16)
        part = jnp.dot(a, wd_ref[cs, :], preferred_element_type=F32)
        acc = part if acc is None else acc + part
    o_ref[0] = x + mod_ref[0, 5:6, :] * acc


def _layer_weight_spec(layer, shape):
    return pl.BlockSpec((None,) + shape, lambda b, j: (layer, 0, 0), pipeline_mode=pl.Buffered(1))


def _block_tail(of, ob, z, y_ga, y_wa, y_ft, xa, mod_l, ng, ones_bd, w_out, g2, wg, wu, wd, layer, n_lat,
                rows_out):
    B = xa.shape[0]
    const2 = lambda b, j: (0, 0)
    tok = lambda w: pl.BlockSpec((1, TM, w), lambda b, j: (b, j, 0))
    resident = functools.partial(_layer_weight_spec, layer)
    return pl.pallas_call(
        _block_tail_kernel,
        grid=(B, pl.cdiv(rows_out, TM)),
        in_specs=[tok(256)] * 6 + [
            tok(D_MODEL),
            _mod_spec(n_lat // TM),
            pl.BlockSpec((1, LANES), const2),
            pl.BlockSpec((LANES, LANES), const2),
            resident((D_MODEL, D_MODEL)),
            pl.BlockSpec((1, D_MODEL), const2),
            resident((D_MODEL, FFN_HIDDEN)),
            resident((D_MODEL, FFN_HIDDEN)),
            resident((FFN_HIDDEN, D_MODEL))],
        out_specs=tok(D_MODEL),
        out_shape=jax.ShapeDtypeStruct((B, rows_out, D_MODEL), F32),
        compiler_params=_cparams(("parallel", "parallel"), 52),
        name="out_projection_ffn",
    )(of, ob, z, y_ga, y_wa, y_ft, xa, mod_l, ng, ones_bd, w_out, g2, wg, wu, wd)


def _rope_tables(n_lat, n_rows):
    t = jnp.arange(n_lat, dtype=jnp.int32)
    row = (t // GRID_W).astype(F32)
    col = (t % GRID_W).astype(F32)
    n_freq = HEAD_DIM // 4
    inv_freq = ROPE_THETA ** (-jnp.arange(n_freq, dtype=F32) / n_freq)
    ang = jnp.concatenate([row[:, None] * inv_freq, col[:, None] * inv_freq], -1)
    cos = jnp.tile(jnp.cos(ang), (1, 4))
    sin = jnp.sin(ang)
    sin = jnp.concatenate([-sin, sin, -sin, sin], -1)
    cos = jnp.concatenate([cos, jnp.ones((n_rows - n_lat, LANES), F32)], 0)
    sin = jnp.concatenate([sin, jnp.zeros((n_rows - n_lat, LANES), F32)], 0)
    return cos, sin


def _dft_matrices(n, n_chan):
    scale = 1.0 / math.sqrt(n * n_chan)
    p = 64 if n % 64 == 0 and n > 64 else 1
    q = n // p
    col = jnp.arange(n, dtype=jnp.int32)[None, :]

    def angles(rows, period):
        return ((jnp.arange(rows, dtype=jnp.int32)[:, None] * col) % period).astype(F32) * (2.0 * math.pi / period)

    ang_a = angles(p, p)
    ang_b = angles(q, n)
    ca, sa = jnp.cos(ang_a)[:, None, :], jnp.sin(ang_a)[:, None, :]
    cb, sb = jnp.cos(ang_b)[None, :, :] * scale, jnp.sin(ang_b)[None, :, :] * scale
    cos = (ca * cb - sa * sb).reshape(n, n)
    msin = (-(sa * cb + ca * sb)).reshape(n, n)
    return cos.astype(BF16), msin.astype(BF16)


def _channel_dft_weights():
    n = HEAD_DIM
    idx = np.arange(n)
    ang = 2.0 * np.pi * ((idx[:, None] * idx[None, :]) % n) / n
    eye = np.eye(4)
    w = np.concatenate([np.kron(eye, np.cos(ang)), np.kron(eye, np.sin(ang))], 1)
    return jnp.asarray(w, F32).astype(BF16)


def kernel(x, c, ctx, c_ctx, norm1_g, norm2_g, w_ada, b_ada, w_in, dn_conv_w, dn_A_log, dn_dt_bias, dn_norm_g,
           ga_q_norm, ga_k_norm, wa_q_norm, wa_k_norm, wa_sink, w_out, w_ffn_gate, w_ffn_up, w_ffn_down):
    B, S, _ = x.shape
    L = ctx.shape[1]
    depth = w_in.shape[0]
    R = S + L
    assert S % TM == 0 and L <= TM and B <= MOD_ROWS // 2
    assert all(S % t == 0 and L <= t for t in (TQ_WINDOW, TQ_GLOBAL))

    cc = jnp.zeros((MOD_ROWS, D_MODEL), F32).at[:B].set(c).at[MOD_ROWS // 2].set(c_ctx)
    mod = _modulation(cc, w_ada, b_ada).reshape(depth, MOD_ROWS, N_MOD, D_MODEL)

    cos_t, sin_t = _rope_tables(S, pl.cdiv(R, TM) * TM)
    cos_lat, msin_lat = _dft_matrices(S, HEAD_DIM)
    cos_ctx, msin_ctx = _dft_matrices(L, HEAD_DIM)
    ftw = _channel_dft_weights()
    ones_bd = jnp.asarray(np.kron(np.eye(2), np.ones((HEAD_DIM, HEAD_DIM))), F32).astype(BF16)
    tile2 = lambda g: jnp.tile(g, 2)

    w_in_r = jnp.concatenate([w_in[:, :, :1024], w_in[:, :, 1040:2320], w_in[:, :, 1024:1040],
                              jnp.zeros((depth, D_MODEL, C_END - 2320), F32)], 2).astype(BF16)
    w_out_b, w_gate_b, w_up_b, w_down_b = [t.astype(BF16) for t in (w_out, w_ffn_gate, w_ffn_up, w_ffn_down)]

    xa = jnp.concatenate([x, ctx], 1)
    for l in range(depth):
        last = l == depth - 1
        gains = jnp.zeros((8, LANES), F32)
        for n, g in enumerate((ga_q_norm[l], ga_k_norm[l], wa_q_norm[l], wa_k_norm[l])):
            gains = gains.at[n].set(tile2(g))
        (qkv, z, ab, gaq, gak, gav, waq, wak, wav, ftz) = _in_projection(
            xa, mod[l], norm1_g[l][None, :], w_in_r, cos_t, sin_t, gains, ones_bd, ftw, l, S)

        conv_w8 = jnp.zeros((8, 768), F32).at[:3].set(dn_conv_w[l])
        neg_a = jnp.zeros((1, LANES), F32).at[0, :8].set(-jnp.exp(dn_A_log[l].astype(F32)).reshape(-1))
        dt_b = jnp.zeros((1, LANES), F32).at[0, :8].set(dn_dt_bias[l].astype(F32).reshape(-1))
        dq, dk, dv, gb = _dn_prep(qkv, ab, conv_w8, neg_a, dt_b, ones_bd, S)
        gt = gb[:, :, :8].reshape(B, R // CHUNK, CHUNK, 8).transpose(0, 1, 3, 2)
        o_f, o_b = _dn_scan(dq, dk, dv, gb, gt, S)

        y_ga = _attention(gaq, gak, gav, S)
        y_wa = _attention(waq, wak, wav, S, sink=wa_sink[l].astype(F32))
        y_ft = _position_dft(ftz, cos_lat, msin_lat, cos_ctx, msin_ctx, S)

        rows_out = S if last else R
        xa = _block_tail(o_f, o_b, z, y_ga, y_wa, y_ft, xa, mod[l], tile2(dn_norm_g[l])[None, :], ones_bd,
                         w_out_b, norm2_g[l][None, :], w_gate_b, w_up_b, w_down_b, l, S, rows_out)
    return xa
```

```python
import functools
import math

import jax
import jax.numpy as jnp
import numpy as np
from jax import lax
from jax.experimental import pallas as pl
from jax.experimental.pallas import tpu as pltpu

F32 = jnp.float32
BF16 = jnp.bfloat16
HIGHEST = lax.Precision.HIGHEST

D_MODEL = 1024
HEAD_DIM = 64
N_HEADS = 4
N_KV = 2
GRID_W = 64
CHUNK = 64
WINDOW = 128
ROPE_THETA = 10000.0
FFN_HIDDEN = 2816
N_MOD = 6
EPS = 1e-6
NEG_INF = -1e30

LANES = 128
TM = 512
TM_TAIL = 512
TQ_GLOBAL = 256
TQ_WINDOW = 512
TQ_DFT = 512
DFT_BATCH = 2
MOD_ROWS = 16
FFN_CHUNK = 256
DN_BATCH = 4
LOG2E = math.log2(math.e)

C_QKV, C_Z, C_GA, C_WA, C_FT, C_AB, C_END = 0, 768, 1024, 1536, 2048, 2304, 2432


def _cparams(sem, vmem_mb):
    return pltpu.CompilerParams(dimension_semantics=sem, vmem_limit_bytes=vmem_mb * 1024 * 1024)


def _silu(x):
    return x * (1.0 / (1.0 + jnp.exp(-x)))


def _head_sumsq(t, ones_bd):
    return jnp.dot((t * t).astype(BF16), ones_bd, preferred_element_type=F32)


def _mod_spec(n_lat_blocks):
    ctx_row = MOD_ROWS // 2
    return pl.BlockSpec((1, N_MOD, D_MODEL), lambda b, j: (jnp.where(j == n_lat_blocks, ctx_row, b), 0, 0))


def _mod_kernel(c_ref, w_ref, b_ref, o_ref):
    a = _silu(c_ref[...])
    o_ref[0] = jnp.dot(a, w_ref[0], preferred_element_type=F32, precision=HIGHEST) + b_ref[0]


def _modulation(cc, w_ada, b_ada):
    depth = w_ada.shape[0]
    return pl.pallas_call(
        _mod_kernel,
        grid=(depth, N_MOD),
        in_specs=[pl.BlockSpec((MOD_ROWS, D_MODEL), lambda l, n: (0, 0)),
                  pl.BlockSpec((1, D_MODEL, D_MODEL), lambda l, n: (l, 0, n)),
                  pl.BlockSpec((1, 1, D_MODEL), lambda l, n: (l, 0, n))],
        out_specs=pl.BlockSpec((1, MOD_ROWS, D_MODEL), lambda l, n: (l, 0, n)),
        out_shape=jax.ShapeDtypeStruct((depth, MOD_ROWS, N_MOD * D_MODEL), F32),
        compiler_params=_cparams(("parallel", "parallel"), 32),
        name="modulation",
    )(cc, w_ada, b_ada.reshape(depth, 1, N_MOD * D_MODEL))


def _inproj_kernel(x_ref, mod_ref, g1_ref, w_ref, cos_ref, sin_ref, gains_ref, ones_ref, ftw_ref,
                   qkv_ref, z_ref, ab_ref, gaq_ref, gak_ref, gav_ref, waq_ref, wak_ref, wav_ref, ft_ref):
    x = x_ref[0]
    ms = jnp.mean(x * x, -1, keepdims=True)
    y = x * lax.rsqrt(ms + EPS) * g1_ref[...]
    h = (y * (1.0 + mod_ref[0, 1:2, :]) + mod_ref[0, 0:1, :]).astype(BF16)

    def proj(c0, c1):
        return jnp.dot(h, w_ref[:, c0:c1], preferred_element_type=F32)

    qkv_ref[0] = proj(C_QKV, C_Z)
    z_ref[0] = proj(C_Z, C_GA)
    ab_ref[0] = proj(C_AB, C_END)

    ones_bd = ones_ref[...]
    cos_t = cos_ref[...]
    sin_t = sin_ref[...]
    lane = lax.broadcasted_iota(jnp.int32, cos_t.shape, 1)
    first_half = (lane & (HEAD_DIM // 2)) == 0

    def norm_rope(t, gain_row, scale):
        yv = t * lax.rsqrt(_head_sumsq(t, ones_bd) * (1.0 / HEAD_DIM) + EPS) * gain_row
        partner = jnp.where(first_half, pltpu.roll(yv, LANES - HEAD_DIM // 2, 1), pltpu.roll(yv, HEAD_DIM // 2, 1))
        return (yv * cos_t + partner * sin_t) * scale

    def heads_out(ref, t, n):
        for i in range(n):
            ref[0, i] = t[:, i * HEAD_DIM:(i + 1) * HEAD_DIM].astype(BF16)

    for c0, q_ref, k_ref, v_ref, gq, gk in ((C_GA, gaq_ref, gak_ref, gav_ref, 0, 1),
                                            (C_WA, waq_ref, wak_ref, wav_ref, 2, 3)):
        p = proj(c0, c0 + 512)
        scale = HEAD_DIM ** -0.5 * LOG2E
        q = jnp.concatenate([norm_rope(p[:, 0:128], gains_ref[gq:gq + 1, :], scale),
                             norm_rope(p[:, 128:256], gains_ref[gq:gq + 1, :], scale)], -1)
        k = norm_rope(p[:, 256:384], gains_ref[gk:gk + 1, :], 1.0)
        heads_out(q_ref, q, N_HEADS)
        heads_out(k_ref, k, N_KV)
        heads_out(v_ref, p[:, 384:512], N_KV)

    pf = proj(C_FT, C_AB).astype(BF16)
    ft_ref[0] = jnp.dot(pf, ftw_ref[...], preferred_element_type=F32).astype(BF16)


def _in_projection(xa, mod_l, g1, w_in_r, cos_t, sin_t, gains, ones_bd, ftw, layer, n_lat):
    B, R, _ = xa.shape
    nb = pl.cdiv(R, TM)
    const2 = lambda b, j: (0, 0)
    tok = lambda w: pl.BlockSpec((1, TM, w), lambda b, j: (b, j, 0))
    hd = lambda n: pl.BlockSpec((1, n, TM, HEAD_DIM), lambda b, j: (b, 0, j, 0))
    sds = jax.ShapeDtypeStruct
    return pl.pallas_call(
        _inproj_kernel,
        grid=(B, nb),
        in_specs=[tok(D_MODEL),
                  _mod_spec(n_lat // TM),
                  pl.BlockSpec((1, D_MODEL), const2),
                  _layer_weight_spec(layer, (D_MODEL, C_END)),
                  pl.BlockSpec((TM, LANES), lambda b, j: (j, 0)),
                  pl.BlockSpec((TM, LANES), lambda b, j: (j, 0)),
                  pl.BlockSpec((8, LANES), const2),
                  pl.BlockSpec((LANES, LANES), const2),
                  pl.BlockSpec((256, 512), const2)],
        out_specs=[tok(768), tok(256), tok(LANES),
                   hd(N_HEADS), hd(N_KV), hd(N_KV), hd(N_HEADS), hd(N_KV), hd(N_KV),
                   tok(512)],
        out_shape=[sds((B, R, 768), F32), sds((B, R, 256), F32), sds((B, R, LANES), F32),
                   sds((B, N_HEADS, R, HEAD_DIM), BF16), sds((B, N_KV, R, HEAD_DIM), BF16),
                   sds((B, N_KV, R, HEAD_DIM), BF16),
                   sds((B, N_HEADS, R, HEAD_DIM), BF16), sds((B, N_KV, R, HEAD_DIM), BF16),
                   sds((B, N_KV, R, HEAD_DIM), BF16),
                   sds((B, R, 512), BF16)],
        compiler_params=_cparams(("parallel", "parallel"), 48),
        name="in_projection",
    )(xa, mod_l, g1, w_in_r, cos_t, sin_t, gains, ones_bd, ftw)


def _dnprep_kernel(cur_ref, prev_ref, next_ref, ab_ref, cw_ref, na_ref, dtb_ref, ones_ref,
                   q_ref, k_ref, v_ref, gb_ref, *, lat_blocks, n_ctx):
    j = pl.program_id(1)
    cur = cur_ref[0]
    tp = cur.shape[0]
    row = lax.broadcasted_iota(jnp.int32, cur.shape, 0)
    is_ctx = j == lat_blocks
    has_prev = jnp.where((j == 0) | is_ctx, 0.0, 1.0)
    has_next = jnp.where(j >= lat_blocks - 1, 0.0, 1.0)
    last_row = jnp.where(is_ctx, n_ctx - 1, tp - 1)
    prev_row = prev_ref[0, 7:8, :] * has_prev
    next_row = next_ref[0, 0:1, :] * has_next
    up = jnp.where(row == 0, prev_row, pltpu.roll(cur, 1, 0))
    dn = jnp.where(row == last_row, next_row, pltpu.roll(cur, tp - 1, 0))
    a = _silu(up * cw_ref[0:1, :] + cur * cw_ref[1:2, :] + dn * cw_ref[2:3, :])

    ones_bd = ones_ref[...]

    def heads_out(ref, t, first):
        for i in range(2):
            ref[0, first + i] = t[:, i * HEAD_DIM:(i + 1) * HEAD_DIM]

    for s in range(2):
        tq = a[:, s * LANES:(s + 1) * LANES]
        heads_out(q_ref, tq * lax.rsqrt(_head_sumsq(tq, ones_bd) + EPS) * HEAD_DIM ** -0.5, 2 * s)
        tk = a[:, 256 + s * LANES:256 + (s + 1) * LANES]
        heads_out(k_ref, tk * lax.rsqrt(_head_sumsq(tk, ones_bd) + EPS), 2 * s)
        heads_out(v_ref, a[:, 512 + s * LANES:512 + (s + 1) * LANES], 2 * s)

    ab = ab_ref[0]
    t = ab + dtb_ref[...]
    softplus = jnp.maximum(t, 0.0) + jnp.log(1.0 + jnp.exp(-jnp.abs(t)))
    g = na_ref[...] * softplus
    beta = 1.0 / (1.0 + jnp.exp(-ab))
    lane = lax.broadcasted_iota(jnp.int32, (CHUNK, LANES), 1)
    ii = lax.broadcasted_iota(jnp.int32, (CHUNK, CHUNK), 0)
    jj = lax.broadcasted_iota(jnp.int32, (CHUNK, CHUNK), 1)
    prefix = jnp.where(ii >= jj, 1.0, 0.0)
    suffix = jnp.where(ii <= jj, 1.0, 0.0)
    for c in range(tp // CHUNK):
        rows = slice(c * CHUNK, (c + 1) * CHUNK)
        gc = g[rows]
        fwd = jnp.dot(prefix, gc, preferred_element_type=F32, precision=HIGHEST)
        bwd = jnp.dot(suffix, gc, preferred_element_type=F32, precision=HIGHEST)
        gb_ref[0, rows, :] = jnp.where(lane < N_HEADS, fwd, jnp.where(lane < 2 * N_HEADS, bwd, beta[rows]))


def _dn_prep(qkv, ab, conv_w8, neg_a, dt_b, ones_bd, n_lat):
    B, R, _ = qkv.shape
    nb = pl.cdiv(R, TM)
    hb = TM // 8
    const2 = lambda b, j: (0, 0)
    tok = lambda w: pl.BlockSpec((1, TM, w), lambda b, j: (b, j, 0))
    hd = pl.BlockSpec((1, N_HEADS, TM, HEAD_DIM), lambda b, j: (b, 0, j, 0))
    sds = jax.ShapeDtypeStruct
    return pl.pallas_call(
        functools.partial(_dnprep_kernel, lat_blocks=n_lat // TM, n_ctx=R - n_lat),
        grid=(B, nb),
        in_specs=[tok(768),
                  pl.BlockSpec((1, 8, 768), lambda b, j: (b, jnp.maximum(j * hb - 1, 0), 0)),
                  pl.BlockSpec((1, 8, 768), lambda b, j: (b, jnp.minimum((j + 1) * hb, R // 8 - 1), 0)),
                  tok(LANES),
                  pl.BlockSpec((8, 768), const2),
                  pl.BlockSpec((1, LANES), const2),
                  pl.BlockSpec((1, LANES), const2),
                  pl.BlockSpec((LANES, LANES), const2)],
        out_specs=[hd, hd, hd, tok(LANES)],
        out_shape=[sds((B, N_HEADS, R, HEAD_DIM), F32)] * 3 + [sds((B, R, LANES), F32)],
        compiler_params=_cparams(("parallel", "parallel"), 32),
        name="deltanet_prep",
    )(qkv, qkv, qkv, ab, conv_w8, neg_a, dt_b, ones_bd)


def _bdot(a, b, dims):
    return lax.dot_general(a.astype(BF16), b.astype(BF16), dims, preferred_element_type=F32)


_NN = (((2,), (1,)), ((0,), (0,)))
_NT = (((2,), (2,)), ((0,), (0,)))
_TN = (((1,), (1,)), ((0,), (0,)))


def _dnscan_kernel(qf_ref, kf_ref, vf_ref, gbf_ref, gtf_ref, qb_ref, kb_ref, vb_ref, gbb_ref, gtb_ref,
                   of_ref, ob_ref, state_ref, *, bb):
    i = pl.program_id(1)
    n = 2 * N_HEADS * bb

    @pl.when(i == 0)
    def _():
        state_ref[...] = jnp.zeros_like(state_ref)

    gc_cols, b_cols, gc_rows, g_lasts = [], [], [], []
    for d, (gb_ref, gt_ref) in enumerate(((gbf_ref, gtf_ref), (gbb_ref, gtb_ref))):
        last = CHUNK - 1 if d == 0 else 0
        for b in range(bb):
            gb = gb_ref[b]
            gt = gt_ref[b, 0]
            for h in range(N_HEADS):
                c = d * N_HEADS + h
                gc_cols.append(gb[:, c:c + 1])
                g_lasts.append(gb[last:last + 1, c:c + 1])
                b_cols.append(gb[:, 2 * N_HEADS + c:2 * N_HEADS + c + 1])
                gc_rows.append(gt[c:c + 1, :])
    gc_col = jnp.stack(gc_cols)
    b_col = jnp.stack(b_cols)
    gc_row = jnp.stack(gc_rows)
    g_last = jnp.stack(g_lasts)

    shape3 = (n, CHUNK, CHUNK)
    backward = lax.broadcasted_iota(jnp.int32, shape3, 0) >= n // 2
    i3 = lax.broadcasted_iota(jnp.int32, shape3, 1)
    j3 = lax.broadcasted_iota(jnp.int32, shape3, 2)
    late = jnp.where(backward, j3, i3)
    early = jnp.where(backward, i3, j3)
    incl = late >= early
    strict = late > early
    ii = lax.broadcasted_iota(jnp.int32, (CHUNK, CHUNK), 0)
    jj = lax.broadcasted_iota(jnp.int32, (CHUNK, CHUNK), 1)
    eye = jnp.where(ii == jj, 1.0, 0.0)
    xor = ii ^ jj
    blocks = [(xor >> sh) == 1 for sh in range(6)]

    def stack(f_ref, b_ref):
        return jnp.concatenate([f_ref[...].reshape(n // 2, CHUNK, HEAD_DIM),
                                b_ref[...].reshape(n // 2, CHUNK, HEAD_DIM)], 0)

    q = stack(qf_ref, qb_ref)
    k = stack(kf_ref, kb_ref)
    v = stack(vf_ref, vb_ref)

    decay = jnp.exp(jnp.where(incl, gc_col - gc_row, NEG_INF))
    e_gc = jnp.exp(gc_col)
    kb = k * b_col
    kq = _bdot(jnp.concatenate([kb, q], 1), k, _NT)
    a_mat = jnp.where(strict, kq[:, :CHUNK] * decay, 0.0)
    intra = jnp.where(incl, kq[:, CHUNK:] * decay, 0.0)
    t = eye - jnp.where(blocks[0], a_mat, 0.0)
    for mask in blocks[1:-1]:
        t = t - _bdot(t, _bdot(jnp.where(mask, a_mat, 0.0), t, _NN), _NN)
    rhs = jnp.concatenate([v * b_col, kb * e_gc], 2)
    half = _bdot(t, rhs, _NN)
    sol = half - _bdot(t, _bdot(jnp.where(blocks[-1], a_mat, 0.0), half, _NN), _NN)
    u, w = sol[:, :, :HEAD_DIM], sol[:, :, HEAD_DIM:]
    state = state_ref[...]
    wq = _bdot(jnp.concatenate([w, q * e_gc], 1), state, _NN)
    v_new = u - wq[:, :CHUNK]
    o = wq[:, CHUNK:] + _bdot(intra, v_new, _NN)
    k_dec = k * jnp.exp(g_last - gc_col)
    state_ref[...] = state * jnp.exp(g_last) + _bdot(k_dec, v_new, _TN)

    for d, o_ref in enumerate((of_ref, ob_ref)):
        for b in range(bb):
            for h in range(N_HEADS):
                o_ref[b, :, h * HEAD_DIM:(h + 1) * HEAD_DIM] = o[(d * bb + b) * N_HEADS + h]


def _dn_scan(q, k, v, gb, gt, n_lat):
    B, _, R, _ = q.shape
    nc = R // CHUNK
    nlc = n_lat // CHUNK
    ncc = nc - nlc
    bb = DN_BATCH if B % DN_BATCH == 0 else 1

    def fwd_chunk(i):
        return jnp.where(i < ncc, nlc + i, i - ncc)

    def bwd_chunk(i):
        return jnp.where(i < ncc, nc - 1 - i, nlc - 1 - (i - ncc))

    hdf = pl.BlockSpec((bb, N_HEADS, CHUNK, HEAD_DIM), lambda b, i: (b, 0, fwd_chunk(i), 0))
    hdb = pl.BlockSpec((bb, N_HEADS, CHUNK, HEAD_DIM), lambda b, i: (b, 0, bwd_chunk(i), 0))
    tokf = lambda w: pl.BlockSpec((bb, CHUNK, w), lambda b, i: (b, fwd_chunk(i), 0))
    tokb = lambda w: pl.BlockSpec((bb, CHUNK, w), lambda b, i: (b, bwd_chunk(i), 0))
    gtf = pl.BlockSpec((bb, 1, 8, CHUNK), lambda b, i: (b, fwd_chunk(i), 0, 0))
    gtb = pl.BlockSpec((bb, 1, 8, CHUNK), lambda b, i: (b, bwd_chunk(i), 0, 0))
    sds = jax.ShapeDtypeStruct
    return pl.pallas_call(
        functools.partial(_dnscan_kernel, bb=bb),
        grid=(B // bb, nc),
        in_specs=[hdf, hdf, hdf, tokf(LANES), gtf, hdb, hdb, hdb, tokb(LANES), gtb],
        out_specs=[tokf(256), tokb(256)],
        out_shape=[sds((B, R, 256), F32), sds((B, R, 256), F32)],
        scratch_shapes=[pltpu.VMEM((2 * N_HEADS * bb, HEAD_DIM, HEAD_DIM), F32)],
        compiler_params=_cparams(("parallel", "arbitrary"), 32),
        name="deltanet_scan",
    )(q, k, v, gb, gt, q, k, v, gb, gt)


def _softmax_pv(s_list, v_list, sink):
    m = functools.reduce(jnp.maximum, [s.max(-1, keepdims=True) for s in s_list])
    if sink is not None:
        m = jnp.maximum(m, sink)
    ps = [jnp.exp2(s - m) for s in s_list]
    den = functools.reduce(lambda a, b: a + b, [p.sum(-1, keepdims=True) for p in ps])
    if sink is not None:
        den = den + jnp.exp2(sink - m)
    o = functools.reduce(lambda a, b: a + b,
                         [jnp.dot(p.astype(BF16), v, preferred_element_type=F32) for p, v in zip(ps, v_list)])
    return o / den


def _scores(q, k):
    return lax.dot_general(q, k, (((1,), (1,)), ((), ())), preferred_element_type=F32)


def _store_heads(o_ref, heads, rows):
    o_ref[0, :rows, :] = jnp.concatenate(heads, -1).astype(BF16)


def _gattn_kernel(q_ref, k_ref, v_ref, o_ref, *, n_lat):
    j = pl.program_id(2)
    tq = q_ref.shape[2]
    n_ctx = k_ref.shape[2] - n_lat

    @pl.when(j == n_lat // tq)
    def _():
        k = k_ref[0, 0, n_lat:, :]
        v = v_ref[0, 0, n_lat:, :]
        _store_heads(o_ref, [_softmax_pv([_scores(q_ref[0, g, :n_ctx, :], k)], [v], None) for g in range(2)],
                     n_ctx)

    @pl.when(j < n_lat // tq)
    def _():
        k = k_ref[0, 0]
        v = v_ref[0, 0]
        _store_heads(o_ref, [_softmax_pv([_scores(q_ref[0, g], k)], [v], None) for g in range(2)], tq)


def _global_attention(q, k, v, n_lat):
    B, _, R, _ = q.shape
    tq = TQ_GLOBAL
    return pl.pallas_call(
        functools.partial(_gattn_kernel, n_lat=n_lat),
        grid=(B, N_KV, pl.cdiv(R, tq)),
        in_specs=[pl.BlockSpec((1, 2, tq, HEAD_DIM), lambda b, h, j: (b, h, j, 0)),
                  pl.BlockSpec((1, 1, R, HEAD_DIM), lambda b, h, j: (b, h, 0, 0)),
                  pl.BlockSpec((1, 1, R, HEAD_DIM), lambda b, h, j: (b, h, 0, 0))],
        out_specs=pl.BlockSpec((1, tq, LANES), lambda b, h, j: (b, j, h)),
        out_shape=jax.ShapeDtypeStruct((B, R, N_HEADS * HEAD_DIM), BF16),
        compiler_params=_cparams(("parallel", "parallel", "parallel"), 48),
        name="global_attention",
    )(q, k, v)


def _wattn_kernel(sink_ref, bias_ref, q_ref, k_ref, v_ref, o_ref, *, n_lat):
    h = pl.program_id(1)
    j = pl.program_id(2)
    tq = q_ref.shape[2]
    n_ctx = k_ref.shape[2] - n_lat
    band = tq + 2 * WINDOW
    kc = k_ref[0, 0, n_lat:, :]
    vc = v_ref[0, 0, n_lat:, :]
    sinks = [sink_ref[2 * h + g] * LOG2E for g in range(2)]

    @pl.when(j == n_lat // tq)
    def _():
        _store_heads(o_ref, [_softmax_pv([_scores(q_ref[0, g, :n_ctx, :], kc)], [vc], sinks[g])
                             for g in range(2)], n_ctx)

    @pl.when(j < n_lat // tq)
    def _():
        q0 = j * tq
        start = pl.multiple_of(jnp.clip(q0 - WINDOW, 0, n_lat - band), WINDOW)
        kb = k_ref[0, 0, pl.ds(start, band), :]
        vb = v_ref[0, 0, pl.ds(start, band), :]
        bias = bias_ref[(q0 - start) // WINDOW]
        heads = []
        for g in range(2):
            q = q_ref[0, g]
            heads.append(_softmax_pv([_scores(q, kb) + bias, _scores(q, kc)], [vb, vc], sinks[g]))
        _store_heads(o_ref, heads, tq)


def _window_bias(tq):
    band = tq + 2 * WINDOW
    off = jnp.arange(3, dtype=jnp.int32)[:, None, None] * WINDOW
    r = jnp.arange(tq, dtype=jnp.int32)[None, :, None]
    c = jnp.arange(band, dtype=jnp.int32)[None, None, :]
    return jnp.where(jnp.abs(r + off - c) <= WINDOW, 0.0, NEG_INF).astype(F32)


def _window_attention(q, k, v, sink, n_lat):
    B, _, R, _ = q.shape
    tq = TQ_WINDOW
    bias = _window_bias(tq)
    return pl.pallas_call(
        functools.partial(_wattn_kernel, n_lat=n_lat),
        grid=(B, N_KV, pl.cdiv(R, tq)),
        in_specs=[pl.BlockSpec(memory_space=pltpu.SMEM),
                  pl.BlockSpec(bias.shape, lambda b, h, j: (0, 0, 0), pipeline_mode=pl.Buffered(1)),
                  pl.BlockSpec((1, 2, tq, HEAD_DIM), lambda b, h, j: (b, h, j, 0)),
                  pl.BlockSpec((1, 1, R, HEAD_DIM), lambda b, h, j: (b, h, 0, 0)),
                  pl.BlockSpec((1, 1, R, HEAD_DIM), lambda b, h, j: (b, h, 0, 0))],
        out_specs=pl.BlockSpec((1, tq, LANES), lambda b, h, j: (b, j, h)),
        out_shape=jax.ShapeDtypeStruct((B, R, N_HEADS * HEAD_DIM), BF16),
        compiler_params=_cparams(("parallel", "parallel", "parallel"), 48),
        name="window_attention",
    )(sink, bias, q, k, v)


def _dft_kernel(z_ref, cl_ref, sl_ref, cc_ref, sc_ref, o_ref, *, n_lat):
    j = pl.program_id(1)
    half = z_ref.shape[2] // 2

    def mix(c, s, z):
        return (jnp.dot(c, z[:, :half], preferred_element_type=F32)
                + jnp.dot(s, z[:, half:], preferred_element_type=F32)).astype(BF16)

    n_ctx = z_ref.shape[1] - n_lat

    @pl.when(j == n_lat // TQ_DFT)
    def _():
        for b in range(z_ref.shape[0]):
            o_ref[b, :n_ctx, :] = mix(cc_ref[...], sc_ref[...], z_ref[b, n_lat:, :])

    @pl.when(j < n_lat // TQ_DFT)
    def _():
        for b in range(z_ref.shape[0]):
            o_ref[b] = mix(cl_ref[...], sl_ref[...], z_ref[b, :n_lat, :])


def _position_dft(z, cos_lat, msin_lat, cos_ctx, msin_ctx, n_lat):
    B, R, _ = z.shape
    n_ctx = R - n_lat
    bb = DFT_BATCH if B % DFT_BATCH == 0 else 1
    lat_blocks = n_lat // TQ_DFT
    lat = pl.BlockSpec((TQ_DFT, n_lat), lambda b, j: (jnp.minimum(j, lat_blocks - 1), 0))
    ctx = pl.BlockSpec((n_ctx, n_ctx), lambda b, j: (0, 0))
    return pl.pallas_call(
        functools.partial(_dft_kernel, n_lat=n_lat),
        grid=(B // bb, pl.cdiv(R, TQ_DFT)),
        in_specs=[pl.BlockSpec((bb, R, 512), lambda b, j: (b, 0, 0), pipeline_mode=pl.Buffered(1)),
                  lat, lat, ctx, ctx],
        out_specs=pl.BlockSpec((bb, TQ_DFT, 256), lambda b, j: (b, j, 0)),
        out_shape=jax.ShapeDtypeStruct((B, R, 256), BF16),
        compiler_params=_cparams(("parallel", "parallel"), 48),
        name="position_dft",
    )(z, cos_lat, msin_lat, cos_ctx, msin_ctx)


def _block_tail_kernel(of_ref, ob_ref, z_ref, ga_ref, wa_ref, ft_ref, x_ref, mod_ref, ng_ref, ones_ref, w_ref,
                       g2_ref, wg_ref, wu_ref, wd_ref, o_ref):
    o = of_ref[0] + ob_ref[0]
    z = z_ref[0]
    ones_bd = ones_ref[...]
    acc = None
    for s in range(2):
        t = o[:, s * LANES:(s + 1) * LANES]
        y = t * lax.rsqrt(_head_sumsq(t, ones_bd) * (1.0 / HEAD_DIM) + EPS) * ng_ref[...]
        y = (y * _silu(z[:, s * LANES:(s + 1) * LANES])).astype(BF16)
        part = jnp.dot(y, w_ref[s * LANES:(s + 1) * LANES, :], preferred_element_type=F32)
        acc = part if acc is None else acc + part
    for n, ref in enumerate((ga_ref, wa_ref, ft_ref)):
        acc = acc + jnp.dot(ref[0], w_ref[256 * (n + 1):256 * (n + 2), :], preferred_element_type=F32)
    x = x_ref[0] + mod_ref[0, 2:3, :] * acc

    ms = jnp.mean(x * x, -1, keepdims=True)
    y = x * lax.rsqrt(ms + EPS) * g2_ref[...]
    h = (y * (1.0 + mod_ref[0, 4:5, :]) + mod_ref[0, 3:4, :]).astype(BF16)
    acc = None
    for c in range(FFN_HIDDEN // FFN_CHUNK):
        cs = slice(c * FFN_CHUNK, (c + 1) * FFN_CHUNK)
        g = jnp.dot(h, wg_ref[:, cs], preferred_element_type=F32)
        u = jnp.dot(h, wu_ref[:, cs], preferred_element_type=F32)
        a = (_silu(g) * u).astype(BF16)
        part = jnp.dot(a, wd_ref[cs, :], preferred_element_type=F32)
        acc = part if acc is None else acc + part
    o_ref[0] = x + mod_ref[0, 5:6, :] * acc


def _layer_weight_spec(layer, shape):
    return pl.BlockSpec((None,) + shape, lambda b, j: (layer, 0, 0), pipeline_mode=pl.Buffered(1))


def _block_tail(of, ob, z, y_ga, y_wa, y_ft, xa, mod_l, ng, ones_bd, w_out, g2, wg, wu, wd, layer, n_lat,
                rows_out):
    B = xa.shape[0]
    const2 = lambda b, j: (0, 0)
    tok = lambda w: pl.BlockSpec((1, TM_TAIL, w), lambda b, j: (b, j, 0))
    resident = functools.partial(_layer_weight_spec, layer)
    return pl.pallas_call(
        _block_tail_kernel,
        grid=(B, pl.cdiv(rows_out, TM_TAIL)),
        in_specs=[tok(256)] * 6 + [
            tok(D_MODEL),
            _mod_spec(n_lat // TM_TAIL),
            pl.BlockSpec((1, LANES), const2),
            pl.BlockSpec((LANES, LANES), const2),
            resident((D_MODEL, D_MODEL)),
            pl.BlockSpec((1, D_MODEL), const2),
            resident((D_MODEL, FFN_HIDDEN)),
            resident((D_MODEL, FFN_HIDDEN)),
            resident((FFN_HIDDEN, D_MODEL))],
        out_specs=tok(D_MODEL),
        out_shape=jax.ShapeDtypeStruct((B, rows_out, D_MODEL), F32),
        compiler_params=_cparams(("parallel", "parallel"), 52),
        name="out_projection_ffn",
    )(of, ob, z, y_ga, y_wa, y_ft, xa, mod_l, ng, ones_bd, w_out, g2, wg, wu, wd)


def _rope_tables(n_lat, n_rows):
    t = jnp.arange(n_lat, dtype=jnp.int32)
    row = (t // GRID_W).astype(F32)
    col = (t % GRID_W).astype(F32)
    n_freq = HEAD_DIM // 4
    inv_freq = ROPE_THETA ** (-jnp.arange(n_freq, dtype=F32) / n_freq)
    ang = jnp.concatenate([row[:, None] * inv_freq, col[:, None] * inv_freq], -1)
    cos = jnp.tile(jnp.cos(ang), (1, 4))
    sin = jnp.sin(ang)
    sin = jnp.concatenate([-sin, sin, -sin, sin], -1)
    cos = jnp.concatenate([cos, jnp.ones((n_rows - n_lat, LANES), F32)], 0)
    sin = jnp.concatenate([sin, jnp.zeros((n_rows - n_lat, LANES), F32)], 0)
    return cos, sin


def _dft_matrices(n, n_chan):
    scale = 1.0 / math.sqrt(n * n_chan)
    p = 64 if n % 64 == 0 and n > 64 else 1
    q = n // p
    col = jnp.arange(n, dtype=jnp.int32)[None, :]

    def angles(rows, period):
        return ((jnp.arange(rows, dtype=jnp.int32)[:, None] * col) % period).astype(F32) * (2.0 * math.pi / period)

    ang_a = angles(p, p)
    ang_b = angles(q, n)
    ca, sa = jnp.cos(ang_a)[:, None, :], jnp.sin(ang_a)[:, None, :]
    cb, sb = jnp.cos(ang_b)[None, :, :] * scale, jnp.sin(ang_b)[None, :, :] * scale
    cos = (ca * cb - sa * sb).reshape(n, n)
    msin = (-(sa * cb + ca * sb)).reshape(n, n)
    return cos.astype(BF16), msin.astype(BF16)


def _channel_dft_weights():
    n = HEAD_DIM
    idx = np.arange(n)
    ang = 2.0 * np.pi * ((idx[:, None] * idx[None, :]) % n) / n
    eye = np.eye(4)
    w = np.concatenate([np.kron(eye, np.cos(ang)), np.kron(eye, np.sin(ang))], 1)
    return jnp.asarray(w, F32).astype(BF16)


def kernel(x, c, ctx, c_ctx, norm1_g, norm2_g, w_ada, b_ada, w_in, dn_conv_w, dn_A_log, dn_dt_bias, dn_norm_g,
           ga_q_norm, ga_k_norm, wa_q_norm, wa_k_norm, wa_sink, w_out, w_ffn_gate, w_ffn_up, w_ffn_down):
    B, S, _ = x.shape
    L = ctx.shape[1]
    depth = w_in.shape[0]
    R = S + L
    assert S % TM == 0 and S % TM_TAIL == 0 and L <= TM_TAIL and B <= MOD_ROWS // 2
    assert all(S % t == 0 and L <= t for t in (TQ_WINDOW, TQ_DFT, TQ_GLOBAL))

    cc = jnp.zeros((MOD_ROWS, D_MODEL), F32).at[:B].set(c).at[MOD_ROWS // 2].set(c_ctx)
    mod = _modulation(cc, w_ada, b_ada).reshape(depth, MOD_ROWS, N_MOD, D_MODEL)

    cos_t, sin_t = _rope_tables(S, pl.cdiv(R, TM) * TM)
    cos_lat, msin_lat = _dft_matrices(S, HEAD_DIM)
    cos_ctx, msin_ctx = _dft_matrices(L, HEAD_DIM)
    ftw = _channel_dft_weights()
    ones_bd = jnp.asarray(np.kron(np.eye(2), np.ones((HEAD_DIM, HEAD_DIM))), F32).astype(BF16)
    tile2 = lambda g: jnp.tile(g, 2)

    w_in_r = jnp.concatenate([w_in[:, :, :1024], w_in[:, :, 1040:2320], w_in[:, :, 1024:1040],
                              jnp.zeros((depth, D_MODEL, C_END - 2320), F32)], 2).astype(BF16)
    w_out_b, w_gate_b, w_up_b, w_down_b = [t.astype(BF16) for t in (w_out, w_ffn_gate, w_ffn_up, w_ffn_down)]

    xa = jnp.concatenate([x, ctx], 1)
    for l in range(depth):
        last = l == depth - 1
        gains = jnp.zeros((8, LANES), F32)
        for n, g in enumerate((ga_q_norm[l], ga_k_norm[l], wa_q_norm[l], wa_k_norm[l])):
            gains = gains.at[n].set(tile2(g))
        (qkv, z, ab, gaq, gak, gav, waq, wak, wav, ftz) = _in_projection(
            xa, mod[l], norm1_g[l][None, :], w_in_r, cos_t, sin_t, gains, ones_bd, ftw, l, S)

        conv_w8 = jnp.zeros((8, 768), F32).at[:3].set(dn_conv_w[l])
        neg_a = jnp.zeros((1, LANES), F32).at[0, :8].set(-jnp.exp(dn_A_log[l].astype(F32)).reshape(-1))
        dt_b = jnp.zeros((1, LANES), F32).at[0, :8].set(dn_dt_bias[l].astype(F32).reshape(-1))
        dq, dk, dv, gb = _dn_prep(qkv, ab, conv_w8, neg_a, dt_b, ones_bd, S)
        gt = gb[:, :, :8].reshape(B, R // CHUNK, CHUNK, 8).transpose(0, 1, 3, 2)
        o_f, o_b = _dn_scan(dq, dk, dv, gb, gt, S)

        y_ga = _global_attention(gaq, gak, gav, S)
        y_wa = _window_attention(waq, wak, wav, wa_sink[l].astype(F32), S)
        y_ft = _position_dft(ftz, cos_lat, msin_lat, cos_ctx, msin_ctx, S)

        rows_out = S if last else R
        xa = _block_tail(o_f, o_b, z, y_ga, y_wa, y_ft, xa, mod[l], tile2(dn_norm_g[l])[None, :], ones_bd,
                         w_out_b, norm2_g[l][None, :], w_gate_b, w_up_b, w_down_b, l, S, rows_out)
    return xa
```

```python
import functools
import math

import jax
import jax.numpy as jnp
import numpy as np
from jax import lax
from jax.experimental import pallas as pl
from jax.experimental.pallas import tpu as pltpu

F32 = jnp.float32
BF16 = jnp.bfloat16
HIGHEST = lax.Precision.HIGHEST

D_MODEL = 1024
HEAD_DIM = 64
N_HEADS = 4
N_KV = 2
GRID_W = 64
CHUNK = 64
WINDOW = 128
ROPE_THETA = 10000.0
FFN_HIDDEN = 2816
N_MOD = 6
EPS = 1e-6
NEG_INF = -1e30

LANES = 128
TM = 512
TM_TAIL = 512
TQ_GLOBAL = 256
TQ_WINDOW = 512
TQ_DFT = 512
DFT_BATCH = 2
MOD_ROWS = 16
FFN_CHUNK = 256
DN_BATCH = 4
LOG2E = math.log2(math.e)

C_QKV, C_Z, C_GA, C_WA, C_FT, C_AB, C_END = 0, 768, 1024, 1536, 2048, 2304, 2432


def _cparams(sem, vmem_mb):
    return pltpu.CompilerParams(dimension_semantics=sem, vmem_limit_bytes=vmem_mb * 1024 * 1024)


def _silu(x):
    return x * (1.0 / (1.0 + jnp.exp(-x)))


def _head_sumsq(t, ones_bd):
    return jnp.dot((t * t).astype(BF16), ones_bd, preferred_element_type=F32)


def _mod_spec(n_lat_blocks):
    ctx_row = MOD_ROWS // 2
    return pl.BlockSpec((1, N_MOD, D_MODEL), lambda b, j: (jnp.where(j == n_lat_blocks, ctx_row, b), 0, 0))


def _mod_kernel(c_ref, w_ref, b_ref, o_ref):
    a = _silu(c_ref[...])
    o_ref[0] = jnp.dot(a, w_ref[0], preferred_element_type=F32, precision=HIGHEST) + b_ref[0]


def _modulation(cc, w_ada, b_ada):
    depth = w_ada.shape[0]
    return pl.pallas_call(
        _mod_kernel,
        grid=(depth, N_MOD),
        in_specs=[pl.BlockSpec((MOD_ROWS, D_MODEL), lambda l, n: (0, 0)),
                  pl.BlockSpec((1, D_MODEL, D_MODEL), lambda l, n: (l, 0, n)),
                  pl.BlockSpec((1, 1, D_MODEL), lambda l, n: (l, 0, n))],
        out_specs=pl.BlockSpec((1, MOD_ROWS, D_MODEL), lambda l, n: (l, 0, n)),
        out_shape=jax.ShapeDtypeStruct((depth, MOD_ROWS, N_MOD * D_MODEL), F32),
        compiler_params=_cparams(("parallel", "parallel"), 32),
        name="modulation",
    )(cc, w_ada, b_ada.reshape(depth, 1, N_MOD * D_MODEL))


def _inproj_kernel(x_ref, mod_ref, g1_ref, w_ref, cos_ref, sin_ref, gains_ref, ones_ref, ftw_ref,
                   qkv_ref, z_ref, ab_ref, gaq_ref, gak_ref, gav_ref, waq_ref, wak_ref, wav_ref, ft_ref):
    x = x_ref[0]
    ms = jnp.mean(x * x, -1, keepdims=True)
    y = x * lax.rsqrt(ms + EPS) * g1_ref[...]
    h = (y * (1.0 + mod_ref[0, 1:2, :]) + mod_ref[0, 0:1, :]).astype(BF16)

    def proj(c0, c1):
        return jnp.dot(h, w_ref[:, c0:c1], preferred_element_type=F32)

    qkv_ref[0] = proj(C_QKV, C_Z)
    z_ref[0] = proj(C_Z, C_GA)
    ab_ref[0] = proj(C_AB, C_END)

    ones_bd = ones_ref[...]
    cos_t = cos_ref[...]
    sin_t = sin_ref[...]
    lane = lax.broadcasted_iota(jnp.int32, cos_t.shape, 1)
    first_half = (lane & (HEAD_DIM // 2)) == 0

    def norm_rope(t, gain_row, scale):
        yv = t * lax.rsqrt(_head_sumsq(t, ones_bd) * (1.0 / HEAD_DIM) + EPS) * gain_row
        partner = jnp.where(first_half, pltpu.roll(yv, LANES - HEAD_DIM // 2, 1), pltpu.roll(yv, HEAD_DIM // 2, 1))
        return (yv * cos_t + partner * sin_t) * scale

    def heads_out(ref, t, n):
        for i in range(n):
            ref[0, i] = t[:, i * HEAD_DIM:(i + 1) * HEAD_DIM].astype(BF16)

    for c0, q_ref, k_ref, v_ref, gq, gk in ((C_GA, gaq_ref, gak_ref, gav_ref, 0, 1),
                                            (C_WA, waq_ref, wak_ref, wav_ref, 2, 3)):
        p = proj(c0, c0 + 512)
        scale = HEAD_DIM ** -0.5 * LOG2E
        q = jnp.concatenate([norm_rope(p[:, 0:128], gains_ref[gq:gq + 1, :], scale),
                             norm_rope(p[:, 128:256], gains_ref[gq:gq + 1, :], scale)], -1)
        k = norm_rope(p[:, 256:384], gains_ref[gk:gk + 1, :], 1.0)
        heads_out(q_ref, q, N_HEADS)
        heads_out(k_ref, k, N_KV)
        heads_out(v_ref, p[:, 384:512], N_KV)

    pf = proj(C_FT, C_AB).astype(BF16)
    ft_ref[0] = jnp.dot(pf, ftw_ref[...], preferred_element_type=F32).astype(BF16)


def _in_projection(xa, mod_l, g1, w_in_r, cos_t, sin_t, gains, ones_bd, ftw, layer, n_lat):
    B, R, _ = xa.shape
    nb = pl.cdiv(R, TM)
    const2 = lambda b, j: (0, 0)
    tok = lambda w: pl.BlockSpec((1, TM, w), lambda b, j: (b, j, 0))
    hd = lambda n: pl.BlockSpec((1, n, TM, HEAD_DIM), lambda b, j: (b, 0, j, 0))
    sds = jax.ShapeDtypeStruct
    return pl.pallas_call(
        _inproj_kernel,
        grid=(B, nb),
        in_specs=[tok(D_MODEL),
                  _mod_spec(n_lat // TM),
                  pl.BlockSpec((1, D_MODEL), const2),
                  _layer_weight_spec(layer, (D_MODEL, C_END)),
                  pl.BlockSpec((TM, LANES), lambda b, j: (j, 0)),
                  pl.BlockSpec((TM, LANES), lambda b, j: (j, 0)),
                  pl.BlockSpec((8, LANES), const2),
                  pl.BlockSpec((LANES, LANES), const2),
                  pl.BlockSpec((256, 512), const2)],
        out_specs=[tok(768), tok(256), tok(LANES),
                   hd(N_HEADS), hd(N_KV), hd(N_KV), hd(N_HEADS), hd(N_KV), hd(N_KV),
                   tok(512)],
        out_shape=[sds((B, R, 768), F32), sds((B, R, 256), F32), sds((B, R, LANES), F32),
                   sds((B, N_HEADS, R, HEAD_DIM), BF16), sds((B, N_KV, R, HEAD_DIM), BF16),
                   sds((B, N_KV, R, HEAD_DIM), BF16),
                   sds((B, N_HEADS, R, HEAD_DIM), BF16), sds((B, N_KV, R, HEAD_DIM), BF16),
                   sds((B, N_KV, R, HEAD_DIM), BF16),
                   sds((B, R, 512), BF16)],
        compiler_params=_cparams(("parallel", "parallel"), 48),
        name="in_projection",
    )(xa, mod_l, g1, w_in_r, cos_t, sin_t, gains, ones_bd, ftw)


def _dnprep_kernel(cur_ref, prev_ref, next_ref, ab_ref, cw_ref, na_ref, dtb_ref, ones_ref,
                   q_ref, k_ref, v_ref, gb_ref, *, lat_blocks, n_ctx):
    j = pl.program_id(1)
    cur = cur_ref[0]
    tp = cur.shape[0]
    row = lax.broadcasted_iota(jnp.int32, cur.shape, 0)
    is_ctx = j == lat_blocks
    has_prev = jnp.where((j == 0) | is_ctx, 0.0, 1.0)
    has_next = jnp.where(j >= lat_blocks - 1, 0.0, 1.0)
    last_row = jnp.where(is_ctx, n_ctx - 1, tp - 1)
    prev_row = prev_ref[0, 7:8, :] * has_prev
    next_row = next_ref[0, 0:1, :] * has_next
    up = jnp.where(row == 0, prev_row, pltpu.roll(cur, 1, 0))
    dn = jnp.where(row == last_row, next_row, pltpu.roll(cur, tp - 1, 0))
    a = _silu(up * cw_ref[0:1, :] + cur * cw_ref[1:2, :] + dn * cw_ref[2:3, :])

    ones_bd = ones_ref[...]

    for s in range(2):
        lanes = slice(s * LANES, (s + 1) * LANES)
        tq = a[:, lanes]
        q_ref[0, :, lanes] = tq * lax.rsqrt(_head_sumsq(tq, ones_bd) + EPS) * HEAD_DIM ** -0.5
        tk = a[:, 256 + s * LANES:256 + (s + 1) * LANES]
        k_ref[0, :, lanes] = tk * lax.rsqrt(_head_sumsq(tk, ones_bd) + EPS)
    v_ref[0] = a[:, 512:768]

    ab = ab_ref[0]
    t = ab + dtb_ref[...]
    softplus = jnp.maximum(t, 0.0) + jnp.log(1.0 + jnp.exp(-jnp.abs(t)))
    g = na_ref[...] * softplus
    beta = 1.0 / (1.0 + jnp.exp(-ab))
    lane = lax.broadcasted_iota(jnp.int32, (CHUNK, LANES), 1)
    ii = lax.broadcasted_iota(jnp.int32, (CHUNK, CHUNK), 0)
    jj = lax.broadcasted_iota(jnp.int32, (CHUNK, CHUNK), 1)
    prefix = jnp.where(ii >= jj, 1.0, 0.0)
    suffix = jnp.where(ii <= jj, 1.0, 0.0)
    for c in range(tp // CHUNK):
        rows = slice(c * CHUNK, (c + 1) * CHUNK)
        gc = g[rows]
        fwd = jnp.dot(prefix, gc, preferred_element_type=F32, precision=HIGHEST)
        bwd = jnp.dot(suffix, gc, preferred_element_type=F32, precision=HIGHEST)
        gb_ref[0, rows, :] = jnp.where(lane < N_HEADS, fwd, jnp.where(lane < 2 * N_HEADS, bwd, beta[rows]))


def _dn_prep(qkv, ab, conv_w8, neg_a, dt_b, ones_bd, n_lat):
    B, R, _ = qkv.shape
    nb = pl.cdiv(R, TM)
    hb = TM // 8
    const2 = lambda b, j: (0, 0)
    tok = lambda w: pl.BlockSpec((1, TM, w), lambda b, j: (b, j, 0))
    sds = jax.ShapeDtypeStruct
    return pl.pallas_call(
        functools.partial(_dnprep_kernel, lat_blocks=n_lat // TM, n_ctx=R - n_lat),
        grid=(B, nb),
        in_specs=[tok(768),
                  pl.BlockSpec((1, 8, 768), lambda b, j: (b, jnp.maximum(j * hb - 1, 0), 0)),
                  pl.BlockSpec((1, 8, 768), lambda b, j: (b, jnp.minimum((j + 1) * hb, R // 8 - 1), 0)),
                  tok(LANES),
                  pl.BlockSpec((8, 768), const2),
                  pl.BlockSpec((1, LANES), const2),
                  pl.BlockSpec((1, LANES), const2),
                  pl.BlockSpec((LANES, LANES), const2)],
        out_specs=[tok(256), tok(256), tok(256), tok(LANES)],
        out_shape=[sds((B, R, 256), F32)] * 3 + [sds((B, R, LANES), F32)],
        compiler_params=_cparams(("parallel", "parallel"), 32),
        name="deltanet_prep",
    )(qkv, qkv, qkv, ab, conv_w8, neg_a, dt_b, ones_bd)


def _bdot(a, b, dims):
    return lax.dot_general(a.astype(BF16), b.astype(BF16), dims, preferred_element_type=F32)


_NN = (((2,), (1,)), ((0,), (0,)))
_NT = (((2,), (2,)), ((0,), (0,)))
_TN = (((1,), (1,)), ((0,), (0,)))


def _pair_blockdiag(x):
    first = (lax.broadcasted_iota(jnp.int32, x.shape, 2) & HEAD_DIM) == 0
    return jnp.concatenate([jnp.where(first, x, 0.0), jnp.where(first, 0.0, x)], 1)


def _dnscan_kernel(qf_ref, kf_ref, vf_ref, gbf_ref, gtf_ref, qb_ref, kb_ref, vb_ref, gbb_ref, gtb_ref,
                   of_ref, ob_ref, state_ref, *, bb):
    i = pl.program_id(1)
    np_ = 2 * bb * (N_HEADS // 2)
    shape = (np_, CHUNK, LANES)

    @pl.when(i == 0)
    def _():
        state_ref[...] = jnp.zeros_like(state_ref)

    lane2 = lax.broadcasted_iota(jnp.int32, (CHUNK, LANES), 1)
    first2 = (lane2 & HEAD_DIM) == 0
    gc_cols, b_cols, gc_rows, g_lasts, qs, ks, vs = [], [], [], [], [], [], []
    for d, (q_ref, k_ref, v_ref, gb_ref, gt_ref) in enumerate(
            ((qf_ref, kf_ref, vf_ref, gbf_ref, gtf_ref), (qb_ref, kb_ref, vb_ref, gbb_ref, gtb_ref))):
        last = CHUNK - 1 if d == 0 else 0
        for b in range(bb):
            gb = gb_ref[b]
            gt = gt_ref[b, 0]
            for m in range(N_HEADS // 2):
                c = d * N_HEADS + 2 * m
                lanes = slice(m * LANES, (m + 1) * LANES)
                qs.append(q_ref[b, :, lanes])
                ks.append(k_ref[b, :, lanes])
                vs.append(v_ref[b, :, lanes])
                col = jnp.where(first2, gb[:, c:c + 1], gb[:, c + 1:c + 2])
                gc_cols.append(col)
                g_lasts.append(col[last:last + 1, :])
                b_cols.append(jnp.where(first2, gb[:, 2 * N_HEADS + c:2 * N_HEADS + c + 1],
                                        gb[:, 2 * N_HEADS + c + 1:2 * N_HEADS + c + 2]))
                gc_rows.append(jnp.concatenate([gt[c:c + 1, :], gt[c + 1:c + 2, :]], 1))
    q = jnp.stack(qs)
    k = jnp.stack(ks)
    v = jnp.stack(vs)
    gc_col = jnp.stack(gc_cols)
    b_col = jnp.stack(b_cols)
    gc_row = jnp.stack(gc_rows)
    g_last = jnp.stack(g_lasts)

    backward = lax.broadcasted_iota(jnp.int32, shape, 0) >= np_ // 2
    i3 = lax.broadcasted_iota(jnp.int32, shape, 1)
    j3 = lax.broadcasted_iota(jnp.int32, shape, 2) & (HEAD_DIM - 1)
    late = jnp.where(backward, j3, i3)
    early = jnp.where(backward, i3, j3)
    incl = late >= early
    strict = late > early
    ii = lax.broadcasted_iota(jnp.int32, (CHUNK, LANES), 0)
    jj = lane2 & (HEAD_DIM - 1)
    eye = jnp.where(ii == jj, 1.0, 0.0)
    xor = ii ^ jj
    blocks = [(xor >> sh) == 1 for sh in range(6)]
    bd = _pair_blockdiag

    decay = jnp.exp(jnp.where(incl, gc_col - gc_row, NEG_INF))
    e_gc = jnp.exp(gc_col)
    kb = k * b_col
    kq = _bdot(jnp.concatenate([kb, q], 1), bd(k), _NT)
    a_mat = jnp.where(strict, kq[:, :CHUNK] * decay, 0.0)
    intra = jnp.where(incl, kq[:, CHUNK:] * decay, 0.0)
    t = eye - jnp.where(blocks[0], a_mat, 0.0)
    for mask in blocks[1:-1]:
        t = t - _bdot(t, bd(_bdot(jnp.where(mask, a_mat, 0.0), bd(t), _NN)), _NN)
    rhs = jnp.concatenate([v * b_col, kb * e_gc], 2)
    half = _bdot(t, bd(rhs), _NN)
    sol = half - _bdot(t, bd(_bdot(jnp.where(blocks[-1], a_mat, 0.0), bd(half), _NN)), _NN)
    u, w = sol[:, :, :LANES], sol[:, :, LANES:]
    state = state_ref[...]
    wq = _bdot(jnp.concatenate([w, q * e_gc], 1), bd(state), _NN)
    v_new = u - wq[:, :CHUNK]
    o = wq[:, CHUNK:] + _bdot(intra, bd(v_new), _NN)
    k_dec = k * jnp.exp(g_last - gc_col)
    kv = _bdot(bd(k_dec), bd(v_new), _TN)
    first3 = (lax.broadcasted_iota(jnp.int32, shape, 2) & HEAD_DIM) == 0
    state_ref[...] = state * jnp.exp(g_last) + jnp.where(first3, kv[:, :CHUNK], kv[:, CHUNK:])

    p = 0
    for o_ref in (of_ref, ob_ref):
        for b in range(bb):
            for m in range(N_HEADS // 2):
                o_ref[b, :, m * LANES:(m + 1) * LANES] = o[p]
                p += 1


def _dn_scan(q, k, v, gb, gt, n_lat):
    B, R, _ = q.shape
    nc = R // CHUNK
    nlc = n_lat // CHUNK
    ncc = nc - nlc
    bb = DN_BATCH if B % DN_BATCH == 0 else 1

    def fwd_chunk(i):
        return jnp.where(i < ncc, nlc + i, i - ncc)

    def bwd_chunk(i):
        return jnp.where(i < ncc, nc - 1 - i, nlc - 1 - (i - ncc))

    tokf = lambda w: pl.BlockSpec((bb, CHUNK, w), lambda b, i: (b, fwd_chunk(i), 0))
    tokb = lambda w: pl.BlockSpec((bb, CHUNK, w), lambda b, i: (b, bwd_chunk(i), 0))
    gtf = pl.BlockSpec((bb, 1, 8, CHUNK), lambda b, i: (b, fwd_chunk(i), 0, 0))
    gtb = pl.BlockSpec((bb, 1, 8, CHUNK), lambda b, i: (b, bwd_chunk(i), 0, 0))
    sds = jax.ShapeDtypeStruct
    return pl.pallas_call(
        functools.partial(_dnscan_kernel, bb=bb),
        grid=(B // bb, nc),
        in_specs=[tokf(256), tokf(256), tokf(256), tokf(LANES), gtf,
                  tokb(256), tokb(256), tokb(256), tokb(LANES), gtb],
        out_specs=[tokf(256), tokb(256)],
        out_shape=[sds((B, R, 256), F32), sds((B, R, 256), F32)],
        scratch_shapes=[pltpu.VMEM((N_HEADS * bb, HEAD_DIM, LANES), F32)],
        compiler_params=_cparams(("parallel", "arbitrary"), 32),
        name="deltanet_scan",
    )(q, k, v, gb, gt, q, k, v, gb, gt)


def _softmax_pv(s_list, v_list, sink):
    m = functools.reduce(jnp.maximum, [s.max(-1, keepdims=True) for s in s_list])
    if sink is not None:
        m = jnp.maximum(m, sink)
    ps = [jnp.exp2(s - m) for s in s_list]
    den = functools.reduce(lambda a, b: a + b, [p.sum(-1, keepdims=True) for p in ps])
    if sink is not None:
        den = den + jnp.exp2(sink - m)
    o = functools.reduce(lambda a, b: a + b,
                         [jnp.dot(p.astype(BF16), v, preferred_element_type=F32) for p, v in zip(ps, v_list)])
    return o / den


def _scores(q, k):
    return lax.dot_general(q, k, (((1,), (1,)), ((), ())), preferred_element_type=F32)


def _store_heads(o_ref, heads, rows):
    o_ref[0, :rows, :] = jnp.concatenate(heads, -1).astype(BF16)


def _gattn_kernel(q_ref, k_ref, v_ref, o_ref, *, n_lat):
    j = pl.program_id(2)
    tq = q_ref.shape[2]
    n_ctx = k_ref.shape[2] - n_lat

    @pl.when(j == n_lat // tq)
    def _():
        k = k_ref[0, 0, n_lat:, :]
        v = v_ref[0, 0, n_lat:, :]
        _store_heads(o_ref, [_softmax_pv([_scores(q_ref[0, g, :n_ctx, :], k)], [v], None) for g in range(2)],
                     n_ctx)

    @pl.when(j < n_lat // tq)
    def _():
        k = k_ref[0, 0]
        v = v_ref[0, 0]
        _store_heads(o_ref, [_softmax_pv([_scores(q_ref[0, g], k)], [v], None) for g in range(2)], tq)


def _global_attention(q, k, v, n_lat):
    B, _, R, _ = q.shape
    tq = TQ_GLOBAL
    return pl.pallas_call(
        functools.partial(_gattn_kernel, n_lat=n_lat),
        grid=(B, N_KV, pl.cdiv(R, tq)),
        in_specs=[pl.BlockSpec((1, 2, tq, HEAD_DIM), lambda b, h, j: (b, h, j, 0)),
                  pl.BlockSpec((1, 1, R, HEAD_DIM), lambda b, h, j: (b, h, 0, 0)),
                  pl.BlockSpec((1, 1, R, HEAD_DIM), lambda b, h, j: (b, h, 0, 0))],
        out_specs=pl.BlockSpec((1, tq, LANES), lambda b, h, j: (b, j, h)),
        out_shape=jax.ShapeDtypeStruct((B, R, N_HEADS * HEAD_DIM), BF16),
        compiler_params=_cparams(("parallel", "parallel", "parallel"), 48),
        name="global_attention",
    )(q, k, v)


def _wattn_kernel(sink_ref, bias_ref, q_ref, k_ref, v_ref, o_ref, *, n_lat):
    h = pl.program_id(1)
    j = pl.program_id(2)
    tq = q_ref.shape[2]
    n_ctx = k_ref.shape[2] - n_lat
    band = tq + 2 * WINDOW
    kc = k_ref[0, 0, n_lat:, :]
    vc = v_ref[0, 0, n_lat:, :]
    sinks = [sink_ref[2 * h + g] * LOG2E for g in range(2)]

    @pl.when(j == n_lat // tq)
    def _():
        _store_heads(o_ref, [_softmax_pv([_scores(q_ref[0, g, :n_ctx, :], kc)], [vc], sinks[g])
                             for g in range(2)], n_ctx)

    @pl.when(j < n_lat // tq)
    def _():
        q0 = j * tq
        start = pl.multiple_of(jnp.clip(q0 - WINDOW, 0, n_lat - band), WINDOW)
        kb = k_ref[0, 0, pl.ds(start, band), :]
        vb = v_ref[0, 0, pl.ds(start, band), :]
        bias = bias_ref[(q0 - start) // WINDOW]
        heads = []
        for g in range(2):
            q = q_ref[0, g]
            heads.append(_softmax_pv([_scores(q, kb) + bias, _scores(q, kc)], [vb, vc], sinks[g]))
        _store_heads(o_ref, heads, tq)


def _window_bias(tq):
    band = tq + 2 * WINDOW
    off = jnp.arange(3, dtype=jnp.int32)[:, None, None] * WINDOW
    r = jnp.arange(tq, dtype=jnp.int32)[None, :, None]
    c = jnp.arange(band, dtype=jnp.int32)[None, None, :]
    return jnp.where(jnp.abs(r + off - c) <= WINDOW, 0.0, NEG_INF).astype(F32)


def _window_attention(q, k, v, sink, n_lat):
    B, _, R, _ = q.shape
    tq = TQ_WINDOW
    bias = _window_bias(tq)
    return pl.pallas_call(
        functools.partial(_wattn_kernel, n_lat=n_lat),
        grid=(B, N_KV, pl.cdiv(R, tq)),
        in_specs=[pl.BlockSpec(memory_space=pltpu.SMEM),
                  pl.BlockSpec(bias.shape, lambda b, h, j: (0, 0, 0), pipeline_mode=pl.Buffered(1)),
                  pl.BlockSpec((1, 2, tq, HEAD_DIM), lambda b, h, j: (b, h, j, 0)),
                  pl.BlockSpec((1, 1, R, HEAD_DIM), lambda b, h, j: (b, h, 0, 0)),
                  pl.BlockSpec((1, 1, R, HEAD_DIM), lambda b, h, j: (b, h, 0, 0))],
        out_specs=pl.BlockSpec((1, tq, LANES), lambda b, h, j: (b, j, h)),
        out_shape=jax.ShapeDtypeStruct((B, R, N_HEADS * HEAD_DIM), BF16),
        compiler_params=_cparams(("parallel", "parallel", "parallel"), 48),
        name="window_attention",
    )(sink, bias, q, k, v)


def _dft_kernel(z_ref, cl_ref, sl_ref, cc_ref, sc_ref, o_ref, *, n_lat):
    j = pl.program_id(1)
    half = z_ref.shape[2] // 2

    def mix(c, s, z):
        return (jnp.dot(c, z[:, :half], preferred_element_type=F32)
                + jnp.dot(s, z[:, half:], preferred_element_type=F32)).astype(BF16)

    n_ctx = z_ref.shape[1] - n_lat

    @pl.when(j == n_lat // TQ_DFT)
    def _():
        for b in range(z_ref.shape[0]):
            o_ref[b, :n_ctx, :] = mix(cc_ref[...], sc_ref[...], z_ref[b, n_lat:, :])

    @pl.when(j < n_lat // TQ_DFT)
    def _():
        for b in range(z_ref.shape[0]):
            o_ref[b] = mix(cl_ref[...], sl_ref[...], z_ref[b, :n_lat, :])


def _position_dft(z, cos_lat, msin_lat, cos_ctx, msin_ctx, n_lat):
    B, R, _ = z.shape
    n_ctx = R - n_lat
    bb = DFT_BATCH if B % DFT_BATCH == 0 else 1
    lat_blocks = n_lat // TQ_DFT
    lat = pl.BlockSpec((TQ_DFT, n_lat), lambda b, j: (jnp.minimum(j, lat_blocks - 1), 0))
    ctx = pl.BlockSpec((n_ctx, n_ctx), lambda b, j: (0, 0))
    return pl.pallas_call(
        functools.partial(_dft_kernel, n_lat=n_lat),
        grid=(B // bb, pl.cdiv(R, TQ_DFT)),
        in_specs=[pl.BlockSpec((bb, R, 512), lambda b, j: (b, 0, 0), pipeline_mode=pl.Buffered(1)),
                  lat, lat, ctx, ctx],
        out_specs=pl.BlockSpec((bb, TQ_DFT, 256), lambda b, j: (b, j, 0)),
        out_shape=jax.ShapeDtypeStruct((B, R, 256), BF16),
        compiler_params=_cparams(("parallel", "parallel"), 48),
        name="position_dft",
    )(z, cos_lat, msin_lat, cos_ctx, msin_ctx)


def _block_tail_kernel(of_ref, ob_ref, z_ref, ga_ref, wa_ref, ft_ref, x_ref, mod_ref, ng_ref, ones_ref, w_ref,
                       g2_ref, wg_ref, wu_ref, wd_ref, o_ref):
    o = of_ref[0] + ob_ref[0]
    z = z_ref[0]
    ones_bd = ones_ref[...]
    acc = None
    for s in range(2):
        t = o[:, s * LANES:(s + 1) * LANES]
        y = t * lax.rsqrt(_head_sumsq(t, ones_bd) * (1.0 / HEAD_DIM) + EPS) * ng_ref[...]
        y = (y * _silu(z[:, s * LANES:(s + 1) * LANES])).astype(BF16)
        part = jnp.dot(y, w_ref[s * LANES:(s + 1) * LANES, :], preferred_element_type=F32)
        acc = part if acc is None else acc + part
    for n, ref in enumerate((ga_ref, wa_ref, ft_ref)):
        acc = acc + jnp.dot(ref[0], w_ref[256 * (n + 1):256 * (n + 2), :], preferred_element_type=F32)
    x = x_ref[0] + mod_ref[0, 2:3, :] * acc

    ms = jnp.mean(x * x, -1, keepdims=True)
    y = x * lax.rsqrt(ms + EPS) * g2_ref[...]
    h = (y * (1.0 + mod_ref[0, 4:5, :]) + mod_ref[0, 3:4, :]).astype(BF16)
    acc = None
    for c in range(FFN_HIDDEN // FFN_CHUNK):
        cs = slice(c * FFN_CHUNK, (c + 1) * FFN_CHUNK)
        g = jnp.dot(h, wg_ref[:, cs], preferred_element_type=F32)
        u = jnp.dot(h, wu_ref[:, cs], preferred_element_type=F32)
        a = (_silu(g) * u).astype(BF16)
        part = jnp.dot(a, wd_ref[cs, :], preferred_element_type=F32)
        acc = part if acc is None else acc + part
    o_ref[0] = x + mod_ref[0, 5:6, :] * acc


def _layer_weight_spec(layer, shape):
    return pl.BlockSpec((None,) + shape, lambda b, j: (layer, 0, 0), pipeline_mode=pl.Buffered(1))


def _block_tail(of, ob, z, y_ga, y_wa, y_ft, xa, mod_l, ng, ones_bd, w_out, g2, wg, wu, wd, layer, n_lat,
                rows_out):
    B = xa.shape[0]
    const2 = lambda b, j: (0, 0)
    tok = lambda w: pl.BlockSpec((1, TM_TAIL, w), lambda b, j: (b, j, 0))
    resident = functools.partial(_layer_weight_spec, layer)
    return pl.pallas_call(
        _block_tail_kernel,
        grid=(B, pl.cdiv(rows_out, TM_TAIL)),
        in_specs=[tok(256)] * 6 + [
            tok(D_MODEL),
            _mod_spec(n_lat // TM_TAIL),
            pl.BlockSpec((1, LANES), const2),
            pl.BlockSpec((LANES, LANES), const2),
            resident((D_MODEL, D_MODEL)),
            pl.BlockSpec((1, D_MODEL), const2),
            resident((D_MODEL, FFN_HIDDEN)),
            resident((D_MODEL, FFN_HIDDEN)),
            resident((FFN_HIDDEN, D_MODEL))],
        out_specs=tok(D_MODEL),
        out_shape=jax.ShapeDtypeStruct((B, rows_out, D_MODEL), F32),
        compiler_params=_cparams(("parallel", "parallel"), 52),
        name="out_projection_ffn",
    )(of, ob, z, y_ga, y_wa, y_ft, xa, mod_l, ng, ones_bd, w_out, g2, wg, wu, wd)


def _rope_tables(n_lat, n_rows):
    t = jnp.arange(n_lat, dtype=jnp.int32)
    row = (t // GRID_W).astype(F32)
    col = (t % GRID_W).astype(F32)
    n_freq = HEAD_DIM // 4
    inv_freq = ROPE_THETA ** (-jnp.arange(n_freq, dtype=F32) / n_freq)
    ang = jnp.concatenate([row[:, None] * inv_freq, col[:, None] * inv_freq], -1)
    cos = jnp.tile(jnp.cos(ang), (1, 4))
    sin = jnp.sin(ang)
    sin = jnp.concatenate([-sin, sin, -sin, sin], -1)
    cos = jnp.concatenate([cos, jnp.ones((n_rows - n_lat, LANES), F32)], 0)
    sin = jnp.concatenate([sin, jnp.zeros((n_rows - n_lat, LANES), F32)], 0)
    return cos, sin


def _dft_matrices(n, n_chan):
    scale = 1.0 / math.sqrt(n * n_chan)
    p = 64 if n % 64 == 0 and n > 64 else 1
    q = n // p
    col = jnp.arange(n, dtype=jnp.int32)[None, :]

    def angles(rows, period):
        return ((jnp.arange(rows, dtype=jnp.int32)[:, None] * col) % period).astype(F32) * (2.0 * math.pi / period)

    ang_a = angles(p, p)
    ang_b = angles(q, n)
    ca, sa = jnp.cos(ang_a)[:, None, :], jnp.sin(ang_a)[:, None, :]
    cb, sb = jnp.cos(ang_b)[None, :, :] * scale, jnp.sin(ang_b)[None, :, :] * scale
    cos = (ca * cb - sa * sb).reshape(n, n)
    msin = (-(sa * cb + ca * sb)).reshape(n, n)
    return cos.astype(BF16), msin.astype(BF16)


def _channel_dft_weights():
    n = HEAD_DIM
    idx = np.arange(n)
    ang = 2.0 * np.pi * ((idx[:, None] * idx[None, :]) % n) / n
    eye = np.eye(4)
    w = np.concatenate([np.kron(eye, np.cos(ang)), np.kron(eye, np.sin(ang))], 1)
    return jnp.asarray(w, F32).astype(BF16)


def kernel(x, c, ctx, c_ctx, norm1_g, norm2_g, w_ada, b_ada, w_in, dn_conv_w, dn_A_log, dn_dt_bias, dn_norm_g,
           ga_q_norm, ga_k_norm, wa_q_norm, wa_k_norm, wa_sink, w_out, w_ffn_gate, w_ffn_up, w_ffn_down):
    B, S, _ = x.shape
    L = ctx.shape[1]
    depth = w_in.shape[0]
    R = S + L
    assert S % TM == 0 and S % TM_TAIL == 0 and L <= TM_TAIL and B <= MOD_ROWS // 2
    assert all(S % t == 0 and L <= t for t in (TQ_WINDOW, TQ_DFT, TQ_GLOBAL))

    cc = jnp.zeros((MOD_ROWS, D_MODEL), F32).at[:B].set(c).at[MOD_ROWS // 2].set(c_ctx)
    mod = _modulation(cc, w_ada, b_ada).reshape(depth, MOD_ROWS, N_MOD, D_MODEL)

    cos_t, sin_t = _rope_tables(S, pl.cdiv(R, TM) * TM)
    cos_lat, msin_lat = _dft_matrices(S, HEAD_DIM)
    cos_ctx, msin_ctx = _dft_matrices(L, HEAD_DIM)
    ftw = _channel_dft_weights()
    ones_bd = jnp.asarray(np.kron(np.eye(2), np.ones((HEAD_DIM, HEAD_DIM))), F32).astype(BF16)
    tile2 = lambda g: jnp.tile(g, 2)

    w_in_r = jnp.concatenate([w_in[:, :, :1024], w_in[:, :, 1040:2320], w_in[:, :, 1024:1040],
                              jnp.zeros((depth, D_MODEL, C_END - 2320), F32)], 2).astype(BF16)
    w_out_b, w_gate_b, w_up_b, w_down_b = [t.astype(BF16) for t in (w_out, w_ffn_gate, w_ffn_up, w_ffn_down)]

    xa = jnp.concatenate([x, ctx], 1)
    for l in range(depth):
        last = l == depth - 1
        gains = jnp.zeros((8, LANES), F32)
        for n, g in enumerate((ga_q_norm[l], ga_k_norm[l], wa_q_norm[l], wa_k_norm[l])):
            gains = gains.at[n].set(tile2(g))
        (qkv, z, ab, gaq, gak, gav, waq, wak, wav, ftz) = _in_projection(
            xa, mod[l], norm1_g[l][None, :], w_in_r, cos_t, sin_t, gains, ones_bd, ftw, l, S)

        conv_w8 = jnp.zeros((8, 768), F32).at[:3].set(dn_conv_w[l])
        neg_a = jnp.zeros((1, LANES), F32).at[0, :8].set(-jnp.exp(dn_A_log[l].astype(F32)).reshape(-1))
        dt_b = jnp.zeros((1, LANES), F32).at[0, :8].set(dn_dt_bias[l].astype(F32).reshape(-1))
        dq, dk, dv, gb = _dn_prep(qkv, ab, conv_w8, neg_a, dt_b, ones_bd, S)
        gt = gb[:, :, :8].reshape(B, R // CHUNK, CHUNK, 8).transpose(0, 1, 3, 2)
        o_f, o_b = _dn_scan(dq, dk, dv, gb, gt, S)

        y_ga = _global_attention(gaq, gak, gav, S)
        y_wa = _window_attention(waq, wak, wav, wa_sink[l].astype(F32), S)
        y_ft = _position_dft(ftz, cos_lat, msin_lat, cos_ctx, msin_ctx, S)

        rows_out = S if last else R
        xa = _block_tail(o_f, o_b, z, y_ga, y_wa, y_ft, xa, mod[l], tile2(dn_norm_g[l])[None, :], ones_bd,
                         w_out_b, norm2_g[l][None, :], w_gate_b, w_up_b, w_down_b, l, S, rows_out)
    return xa
```

```python
import functools
import math

import jax
import jax.numpy as jnp
import numpy as np
from jax import lax
from jax.experimental import pallas as pl
from jax.experimental.pallas import tpu as pltpu

F32 = jnp.float32
BF16 = jnp.bfloat16
HIGHEST = lax.Precision.HIGHEST

D_MODEL = 1024
HEAD_DIM = 64
N_HEADS = 4
N_KV = 2
GRID_W = 64
CHUNK = 64
WINDOW = 128
ROPE_THETA = 10000.0
FFN_HIDDEN = 2816
N_MOD = 6
EPS = 1e-6
NEG_INF = -1e30

LANES = 128
TM = 512
TM_TAIL = 512
TQ_GLOBAL = 256
TQ_WINDOW = 512
TQ_DFT = 512
DFT_BATCH = 2
MOD_ROWS = 16
FFN_CHUNK = 256
DN_BATCH = 4
LOG2E = math.log2(math.e)

C_QKV, C_Z, C_GA, C_WA, C_FT, C_AB, C_END = 0, 768, 1024, 1536, 2048, 2304, 2432


def _cparams(sem, vmem_mb):
    return pltpu.CompilerParams(dimension_semantics=sem, vmem_limit_bytes=vmem_mb * 1024 * 1024)


def _silu(x):
    return x * (1.0 / (1.0 + jnp.exp(-x)))


def _head_sumsq(t, ones_bd):
    return jnp.dot((t * t).astype(BF16), ones_bd, preferred_element_type=F32)


def _mod_spec(n_lat_blocks):
    ctx_row = MOD_ROWS // 2
    return pl.BlockSpec((1, N_MOD, D_MODEL), lambda b, j: (jnp.where(j == n_lat_blocks, ctx_row, b), 0, 0))


def _mod_kernel(c_ref, w_ref, b_ref, o_ref):
    a = _silu(c_ref[...])
    o_ref[0] = jnp.dot(a, w_ref[0], preferred_element_type=F32, precision=HIGHEST) + b_ref[0]


def _modulation(cc, w_ada, b_ada):
    depth = w_ada.shape[0]
    return pl.pallas_call(
        _mod_kernel,
        grid=(depth, N_MOD),
        in_specs=[pl.BlockSpec((MOD_ROWS, D_MODEL), lambda l, n: (0, 0)),
                  pl.BlockSpec((1, D_MODEL, D_MODEL), lambda l, n: (l, 0, n)),
                  pl.BlockSpec((1, 1, D_MODEL), lambda l, n: (l, 0, n))],
        out_specs=pl.BlockSpec((1, MOD_ROWS, D_MODEL), lambda l, n: (l, 0, n)),
        out_shape=jax.ShapeDtypeStruct((depth, MOD_ROWS, N_MOD * D_MODEL), F32),
        compiler_params=_cparams(("parallel", "parallel"), 32),
        name="modulation",
    )(cc, w_ada, b_ada.reshape(depth, 1, N_MOD * D_MODEL))


def _inproj_kernel(x_ref, mod_ref, g1_ref, w_ref, cos_ref, sin_ref, gains_ref, ones_ref, ftw_ref,
                   qkv_ref, z_ref, ab_ref, gaq_ref, gak_ref, gav_ref, waq_ref, wak_ref, wav_ref, ft_ref):
    x = x_ref[0]
    ms = jnp.mean(x * x, -1, keepdims=True)
    y = x * lax.rsqrt(ms + EPS) * g1_ref[...]
    h = (y * (1.0 + mod_ref[0, 1:2, :]) + mod_ref[0, 0:1, :]).astype(BF16)

    def proj(c0, c1):
        return jnp.dot(h, w_ref[:, c0:c1], preferred_element_type=F32)

    qkv_ref[0] = proj(C_QKV, C_Z)
    z_ref[0] = proj(C_Z, C_GA)
    ab_ref[0] = proj(C_AB, C_END)

    ones_bd = ones_ref[...]
    cos_t = cos_ref[...]
    sin_t = sin_ref[...]
    lane = lax.broadcasted_iota(jnp.int32, cos_t.shape, 1)
    first_half = (lane & (HEAD_DIM // 2)) == 0

    def norm_rope(t, gain_row, scale):
        yv = t * lax.rsqrt(_head_sumsq(t, ones_bd) * (1.0 / HEAD_DIM) + EPS) * gain_row
        partner = jnp.where(first_half, pltpu.roll(yv, LANES - HEAD_DIM // 2, 1), pltpu.roll(yv, HEAD_DIM // 2, 1))
        return (yv * cos_t + partner * sin_t) * scale

    def heads_out(ref, t, n):
        for i in range(n):
            ref[0, i] = t[:, i * HEAD_DIM:(i + 1) * HEAD_DIM].astype(BF16)

    for c0, q_ref, k_ref, v_ref, gq, gk in ((C_GA, gaq_ref, gak_ref, gav_ref, 0, 1),
                                            (C_WA, waq_ref, wak_ref, wav_ref, 2, 3)):
        p = proj(c0, c0 + 512)
        scale = HEAD_DIM ** -0.5 * LOG2E
        q = jnp.concatenate([norm_rope(p[:, 0:128], gains_ref[gq:gq + 1, :], scale),
                             norm_rope(p[:, 128:256], gains_ref[gq:gq + 1, :], scale)], -1)
        k = norm_rope(p[:, 256:384], gains_ref[gk:gk + 1, :], 1.0)
        heads_out(q_ref, q, N_HEADS)
        heads_out(k_ref, k, N_KV)
        for i in range(N_KV):
            vh = p[:, 384 + i * HEAD_DIM:384 + (i + 1) * HEAD_DIM]
            v_ref[0, i] = jnp.where(lane < HEAD_DIM, jnp.concatenate([vh, vh], -1),
                                    jnp.where(lane == HEAD_DIM, 1.0, 0.0)).astype(BF16)

    pf = proj(C_FT, C_AB).astype(BF16)
    ft_ref[0] = jnp.dot(pf, ftw_ref[...], preferred_element_type=F32).astype(BF16)


def _in_projection(xa, mod_l, g1, w_in_r, cos_t, sin_t, gains, ones_bd, ftw, layer, n_lat):
    B, R, _ = xa.shape
    nb = pl.cdiv(R, TM)
    const2 = lambda b, j: (0, 0)
    tok = lambda w: pl.BlockSpec((1, TM, w), lambda b, j: (b, j, 0))
    hd = lambda n, w=HEAD_DIM: pl.BlockSpec((1, n, TM, w), lambda b, j: (b, 0, j, 0))
    sds = jax.ShapeDtypeStruct
    heads = lambda n, w=HEAD_DIM: sds((B, n, R, w), BF16)
    return pl.pallas_call(
        _inproj_kernel,
        grid=(B, nb),
        in_specs=[tok(D_MODEL),
                  _mod_spec(n_lat // TM),
                  pl.BlockSpec((1, D_MODEL), const2),
                  _layer_weight_spec(layer, (D_MODEL, C_END)),
                  pl.BlockSpec((TM, LANES), lambda b, j: (j, 0)),
                  pl.BlockSpec((TM, LANES), lambda b, j: (j, 0)),
                  pl.BlockSpec((8, LANES), const2),
                  pl.BlockSpec((LANES, LANES), const2),
                  pl.BlockSpec((256, 512), const2)],
        out_specs=[tok(768), tok(256), tok(LANES),
                   hd(N_HEADS), hd(N_KV), hd(N_KV, LANES), hd(N_HEADS), hd(N_KV), hd(N_KV, LANES),
                   tok(512)],
        out_shape=[sds((B, R, 768), F32), sds((B, R, 256), F32), sds((B, R, LANES), F32),
                   heads(N_HEADS), heads(N_KV), heads(N_KV, LANES),
                   heads(N_HEADS), heads(N_KV), heads(N_KV, LANES),
                   sds((B, R, 512), BF16)],
        compiler_params=_cparams(("parallel", "parallel"), 48),
        name="in_projection",
    )(xa, mod_l, g1, w_in_r, cos_t, sin_t, gains, ones_bd, ftw)


def _dnprep_kernel(cur_ref, prev_ref, next_ref, ab_ref, cw_ref, na_ref, dtb_ref, ones_ref,
                   q_ref, k_ref, v_ref, gb_ref, *, lat_blocks, n_ctx):
    j = pl.program_id(1)
    cur = cur_ref[0]
    tp = cur.shape[0]
    row = lax.broadcasted_iota(jnp.int32, cur.shape, 0)
    is_ctx = j == lat_blocks
    has_prev = jnp.where((j == 0) | is_ctx, 0.0, 1.0)
    has_next = jnp.where(j >= lat_blocks - 1, 0.0, 1.0)
    last_row = jnp.where(is_ctx, n_ctx - 1, tp - 1)
    prev_row = prev_ref[0, 7:8, :] * has_prev
    next_row = next_ref[0, 0:1, :] * has_next
    up = jnp.where(row == 0, prev_row, pltpu.roll(cur, 1, 0))
    dn = jnp.where(row == last_row, next_row, pltpu.roll(cur, tp - 1, 0))
    a = _silu(up * cw_ref[0:1, :] + cur * cw_ref[1:2, :] + dn * cw_ref[2:3, :])

    ones_bd = ones_ref[...]

    for s in range(2):
        lanes = slice(s * LANES, (s + 1) * LANES)
        tq = a[:, lanes]
        q_ref[0, :, lanes] = tq * lax.rsqrt(_head_sumsq(tq, ones_bd) + EPS) * HEAD_DIM ** -0.5
        tk = a[:, 256 + s * LANES:256 + (s + 1) * LANES]
        k_ref[0, :, lanes] = tk * lax.rsqrt(_head_sumsq(tk, ones_bd) + EPS)
    v_ref[0] = a[:, 512:768]

    ab = ab_ref[0]
    t = ab + dtb_ref[...]
    softplus = jnp.maximum(t, 0.0) + jnp.log(1.0 + jnp.exp(-jnp.abs(t)))
    g = na_ref[...] * softplus
    beta = 1.0 / (1.0 + jnp.exp(-ab))
    lane = lax.broadcasted_iota(jnp.int32, (CHUNK, LANES), 1)
    ii = lax.broadcasted_iota(jnp.int32, (CHUNK, CHUNK), 0)
    jj = lax.broadcasted_iota(jnp.int32, (CHUNK, CHUNK), 1)
    prefix = jnp.where(ii >= jj, 1.0, 0.0)
    suffix = jnp.where(ii <= jj, 1.0, 0.0)
    for c in range(tp // CHUNK):
        rows = slice(c * CHUNK, (c + 1) * CHUNK)
        gc = g[rows]
        fwd = jnp.dot(prefix, gc, preferred_element_type=F32, precision=HIGHEST)
        bwd = jnp.dot(suffix, gc, preferred_element_type=F32, precision=HIGHEST)
        gb_ref[0, rows, :] = jnp.where(lane < N_HEADS, fwd, jnp.where(lane < 2 * N_HEADS, bwd, beta[rows]))


def _dn_prep(qkv, ab, conv_w8, neg_a, dt_b, ones_bd, n_lat):
    B, R, _ = qkv.shape
    nb = pl.cdiv(R, TM)
    hb = TM // 8
    const2 = lambda b, j: (0, 0)
    tok = lambda w: pl.BlockSpec((1, TM, w), lambda b, j: (b, j, 0))
    sds = jax.ShapeDtypeStruct
    return pl.pallas_call(
        functools.partial(_dnprep_kernel, lat_blocks=n_lat // TM, n_ctx=R - n_lat),
        grid=(B, nb),
        in_specs=[tok(768),
                  pl.BlockSpec((1, 8, 768), lambda b, j: (b, jnp.maximum(j * hb - 1, 0), 0)),
                  pl.BlockSpec((1, 8, 768), lambda b, j: (b, jnp.minimum((j + 1) * hb, R // 8 - 1), 0)),
                  tok(LANES),
                  pl.BlockSpec((8, 768), const2),
                  pl.BlockSpec((1, LANES), const2),
                  pl.BlockSpec((1, LANES), const2),
                  pl.BlockSpec((LANES, LANES), const2)],
        out_specs=[tok(256), tok(256), tok(256), tok(LANES)],
        out_shape=[sds((B, R, 256), F32)] * 3 + [sds((B, R, LANES), F32)],
        compiler_params=_cparams(("parallel", "parallel"), 32),
        name="deltanet_prep",
    )(qkv, qkv, qkv, ab, conv_w8, neg_a, dt_b, ones_bd)


def _bdot(a, b, dims):
    return lax.dot_general(a.astype(BF16), b.astype(BF16), dims, preferred_element_type=F32)


_NN = (((2,), (1,)), ((0,), (0,)))
_NT = (((2,), (2,)), ((0,), (0,)))
_TN = (((1,), (1,)), ((0,), (0,)))


def _pair_blockdiag(x):
    first = (lax.broadcasted_iota(jnp.int32, x.shape, 2) & HEAD_DIM) == 0
    return jnp.concatenate([jnp.where(first, x, 0.0), jnp.where(first, 0.0, x)], 1)


def _dnscan_kernel(qf_ref, kf_ref, vf_ref, gbf_ref, gtf_ref, qb_ref, kb_ref, vb_ref, gbb_ref, gtb_ref,
                   of_ref, ob_ref, state_ref, *, bb):
    i = pl.program_id(1)
    np_ = 2 * bb * (N_HEADS // 2)
    shape = (np_, CHUNK, LANES)

    @pl.when(i == 0)
    def _():
        state_ref[...] = jnp.zeros_like(state_ref)

    lane2 = lax.broadcasted_iota(jnp.int32, (CHUNK, LANES), 1)
    first2 = (lane2 & HEAD_DIM) == 0
    gc_cols, b_cols, gc_rows, g_lasts, qs, ks, vs = [], [], [], [], [], [], []
    for d, (q_ref, k_ref, v_ref, gb_ref, gt_ref) in enumerate(
            ((qf_ref, kf_ref, vf_ref, gbf_ref, gtf_ref), (qb_ref, kb_ref, vb_ref, gbb_ref, gtb_ref))):
        last = CHUNK - 1 if d == 0 else 0
        for b in range(bb):
            gb = gb_ref[b]
            gt = gt_ref[b, 0]
            for m in range(N_HEADS // 2):
                c = d * N_HEADS + 2 * m
                lanes = slice(m * LANES, (m + 1) * LANES)
                qs.append(q_ref[b, :, lanes])
                ks.append(k_ref[b, :, lanes])
                vs.append(v_ref[b, :, lanes])
                col = jnp.where(first2, gb[:, c:c + 1], gb[:, c + 1:c + 2])
                gc_cols.append(col)
                g_lasts.append(col[last:last + 1, :])
                b_cols.append(jnp.where(first2, gb[:, 2 * N_HEADS + c:2 * N_HEADS + c + 1],
                                        gb[:, 2 * N_HEADS + c + 1:2 * N_HEADS + c + 2]))
                gc_rows.append(jnp.concatenate([gt[c:c + 1, :], gt[c + 1:c + 2, :]], 1))
    q = jnp.stack(qs)
    k = jnp.stack(ks)
    v = jnp.stack(vs)
    gc_col = jnp.stack(gc_cols)
    b_col = jnp.stack(b_cols)
    gc_row = jnp.stack(gc_rows)
    g_last = jnp.stack(g_lasts)

    backward = lax.broadcasted_iota(jnp.int32, shape, 0) >= np_ // 2
    i3 = lax.broadcasted_iota(jnp.int32, shape, 1)
    j3 = lax.broadcasted_iota(jnp.int32, shape, 2) & (HEAD_DIM - 1)
    late = jnp.where(backward, j3, i3)
    early = jnp.where(backward, i3, j3)
    incl = late >= early
    strict = late > early
    ii = lax.broadcasted_iota(jnp.int32, (CHUNK, LANES), 0)
    jj = lane2 & (HEAD_DIM - 1)
    eye = jnp.where(ii == jj, 1.0, 0.0)
    xor = ii ^ jj
    blocks = [(xor >> sh) == 1 for sh in range(6)]
    bd = _pair_blockdiag

    decay = jnp.exp(jnp.where(incl, gc_col - gc_row, NEG_INF))
    e_gc = jnp.exp(gc_col)
    kb = k * b_col
    kq = _bdot(jnp.concatenate([kb, q], 1), bd(k), _NT)
    a_mat = jnp.where(strict, kq[:, :CHUNK] * decay, 0.0)
    intra = jnp.where(incl, kq[:, CHUNK:] * decay, 0.0)
    t = eye - jnp.where(blocks[0], a_mat, 0.0)
    for mask in blocks[1:-1]:
        t = t - _bdot(t, bd(_bdot(jnp.where(mask, a_mat, 0.0), bd(t), _NN)), _NN)
    rhs = jnp.concatenate([v * b_col, kb * e_gc], 2)
    half = _bdot(t, bd(rhs), _NN)
    sol = half - _bdot(t, bd(_bdot(jnp.where(blocks[-1], a_mat, 0.0), bd(half), _NN)), _NN)
    u, w = sol[:, :, :LANES], sol[:, :, LANES:]
    state = state_ref[...]
    wq = _bdot(jnp.concatenate([w, q * e_gc], 1), bd(state), _NN)
    v_new = u - wq[:, :CHUNK]
    o = wq[:, CHUNK:] + _bdot(intra, bd(v_new), _NN)
    k_dec = k * jnp.exp(g_last - gc_col)
    kv = _bdot(bd(k_dec), bd(v_new), _TN)
    first3 = (lax.broadcasted_iota(jnp.int32, shape, 2) & HEAD_DIM) == 0
    state_ref[...] = state * jnp.exp(g_last) + jnp.where(first3, kv[:, :CHUNK], kv[:, CHUNK:])

    p = 0
    for o_ref in (of_ref, ob_ref):
        for b in range(bb):
            for m in range(N_HEADS // 2):
                o_ref[b, :, m * LANES:(m + 1) * LANES] = o[p]
                p += 1


def _dn_scan(q, k, v, gb, gt, n_lat):
    B, R, _ = q.shape
    nc = R // CHUNK
    nlc = n_lat // CHUNK
    ncc = nc - nlc
    bb = DN_BATCH if B % DN_BATCH == 0 else 1

    def fwd_chunk(i):
        return jnp.where(i < ncc, nlc + i, i - ncc)

    def bwd_chunk(i):
        return jnp.where(i < ncc, nc - 1 - i, nlc - 1 - (i - ncc))

    tokf = lambda w: pl.BlockSpec((bb, CHUNK, w), lambda b, i: (b, fwd_chunk(i), 0))
    tokb = lambda w: pl.BlockSpec((bb, CHUNK, w), lambda b, i: (b, bwd_chunk(i), 0))
    gtf = pl.BlockSpec((bb, 1, 8, CHUNK), lambda b, i: (b, fwd_chunk(i), 0, 0))
    gtb = pl.BlockSpec((bb, 1, 8, CHUNK), lambda b, i: (b, bwd_chunk(i), 0, 0))
    sds = jax.ShapeDtypeStruct
    return pl.pallas_call(
        functools.partial(_dnscan_kernel, bb=bb),
        grid=(B // bb, nc),
        in_specs=[tokf(256), tokf(256), tokf(256), tokf(LANES), gtf,
                  tokb(256), tokb(256), tokb(256), tokb(LANES), gtb],
        out_specs=[tokf(256), tokb(256)],
        out_shape=[sds((B, R, 256), F32), sds((B, R, 256), F32)],
        scratch_shapes=[pltpu.VMEM((N_HEADS * bb, HEAD_DIM, LANES), F32)],
        compiler_params=_cparams(("parallel", "arbitrary"), 32),
        name="deltanet_scan",
    )(q, k, v, gb, gt, q, k, v, gb, gt)


def _softmax_pv(s_list, v_list, sink):
    m = functools.reduce(jnp.maximum, [s.max(-1, keepdims=True) for s in s_list])
    if sink is not None:
        m = jnp.maximum(m, sink)
    ov = functools.reduce(lambda a, b: a + b, [jnp.dot(jnp.exp2(s - m).astype(BF16), v, preferred_element_type=F32)
                                               for s, v in zip(s_list, v_list)])
    den = ov[:, HEAD_DIM:HEAD_DIM + 1]
    if sink is not None:
        den = den + jnp.exp2(sink - m)
    return ov[:, :HEAD_DIM] / den


def _scores(q, k):
    return lax.dot_general(q, k, (((1,), (1,)), ((), ())), preferred_element_type=F32)


def _store_heads(o_ref, heads, rows):
    o_ref[0, :rows, :] = jnp.concatenate(heads, -1).astype(BF16)


def _gattn_kernel(q_ref, k_ref, v_ref, o_ref, *, n_lat):
    j = pl.program_id(2)
    tq = q_ref.shape[2]
    n_ctx = k_ref.shape[2] - n_lat

    @pl.when(j == n_lat // tq)
    def _():
        k = k_ref[0, 0, n_lat:, :]
        v = v_ref[0, 0, n_lat:, :]
        _store_heads(o_ref, [_softmax_pv([_scores(q_ref[0, g, :n_ctx, :], k)], [v], None) for g in range(2)],
                     n_ctx)

    @pl.when(j < n_lat // tq)
    def _():
        k = k_ref[0, 0]
        v = v_ref[0, 0]
        _store_heads(o_ref, [_softmax_pv([_scores(q_ref[0, g], k)], [v], None) for g in range(2)], tq)


def _global_attention(q, k, v, n_lat):
    B, _, R, _ = q.shape
    tq = TQ_GLOBAL
    return pl.pallas_call(
        functools.partial(_gattn_kernel, n_lat=n_lat),
        grid=(B, N_KV, pl.cdiv(R, tq)),
        in_specs=[pl.BlockSpec((1, 2, tq, HEAD_DIM), lambda b, h, j: (b, h, j, 0)),
                  pl.BlockSpec((1, 1, R, HEAD_DIM), lambda b, h, j: (b, h, 0, 0)),
                  pl.BlockSpec((1, 1, R, LANES), lambda b, h, j: (b, h, 0, 0))],
        out_specs=pl.BlockSpec((1, tq, LANES), lambda b, h, j: (b, j, h)),
        out_shape=jax.ShapeDtypeStruct((B, R, N_HEADS * HEAD_DIM), BF16),
        compiler_params=_cparams(("parallel", "parallel", "parallel"), 48),
        name="global_attention",
    )(q, k, v)


def _wattn_kernel(sink_ref, bias_ref, q_ref, k_ref, v_ref, o_ref, *, n_lat):
    j = pl.program_id(1)
    tq = q_ref.shape[2]
    n_ctx = k_ref.shape[2] - n_lat
    band = tq + 2 * WINDOW
    group = N_HEADS // N_KV
    sinks = [sink_ref[h] * LOG2E for h in range(N_HEADS)]

    @pl.when(j == n_lat // tq)
    def _():
        _store_heads(o_ref, [_softmax_pv([_scores(q_ref[0, h, :n_ctx, :], k_ref[0, h // group, n_lat:, :])],
                                         [v_ref[0, h // group, n_lat:, :]], sinks[h])
                             for h in range(N_HEADS)], n_ctx)

    @pl.when(j < n_lat // tq)
    def _():
        q0 = j * tq
        start = pl.multiple_of(jnp.clip(q0 - WINDOW, 0, n_lat - band), WINDOW)
        bias = bias_ref[(q0 - start) // WINDOW]
        heads = []
        for h in range(N_HEADS):
            kv = h // group
            q = q_ref[0, h]
            s_loc = _scores(q, k_ref[0, kv, pl.ds(start, band), :]) + bias
            s_ctx = _scores(q, k_ref[0, kv, n_lat:, :])
            heads.append(_softmax_pv([s_loc, s_ctx], [v_ref[0, kv, pl.ds(start, band), :],
                                                      v_ref[0, kv, n_lat:, :]], sinks[h]))
        _store_heads(o_ref, heads, tq)


def _window_bias(tq):
    band = tq + 2 * WINDOW
    off = jnp.arange(3, dtype=jnp.int32)[:, None, None] * WINDOW
    r = jnp.arange(tq, dtype=jnp.int32)[None, :, None]
    c = jnp.arange(band, dtype=jnp.int32)[None, None, :]
    return jnp.where(jnp.abs(r + off - c) <= WINDOW, 0.0, NEG_INF).astype(F32)


def _window_attention(q, k, v, sink, n_lat):
    B, _, R, _ = q.shape
    tq = TQ_WINDOW
    bias = _window_bias(tq)
    return pl.pallas_call(
        functools.partial(_wattn_kernel, n_lat=n_lat),
        grid=(B, pl.cdiv(R, tq)),
        in_specs=[pl.BlockSpec(memory_space=pltpu.SMEM),
                  pl.BlockSpec(bias.shape, lambda b, j: (0, 0, 0), pipeline_mode=pl.Buffered(1)),
                  pl.BlockSpec((1, N_HEADS, tq, HEAD_DIM), lambda b, j: (b, 0, j, 0)),
                  pl.BlockSpec((1, N_KV, R, HEAD_DIM), lambda b, j: (b, 0, 0, 0)),
                  pl.BlockSpec((1, N_KV, R, LANES), lambda b, j: (b, 0, 0, 0))],
        out_specs=pl.BlockSpec((1, tq, N_HEADS * HEAD_DIM), lambda b, j: (b, j, 0)),
        out_shape=jax.ShapeDtypeStruct((B, R, N_HEADS * HEAD_DIM), BF16),
        compiler_params=_cparams(("parallel", "parallel"), 48),
        name="window_attention",
    )(sink, bias, q, k, v)


def _dft_kernel(z_ref, cl_ref, sl_ref, cc_ref, sc_ref, o_ref, *, n_lat):
    j = pl.program_id(1)
    half = z_ref.shape[2] // 2

    def mix(c, s, z):
        return (jnp.dot(c, z[:, :half], preferred_element_type=F32)
                + jnp.dot(s, z[:, half:], preferred_element_type=F32)).astype(BF16)

    n_ctx = z_ref.shape[1] - n_lat

    @pl.when(j == n_lat // TQ_DFT)
    def _():
        for b in range(z_ref.shape[0]):
            o_ref[b, :n_ctx, :] = mix(cc_ref[...], sc_ref[...], z_ref[b, n_lat:, :])

    @pl.when(j < n_lat // TQ_DFT)
    def _():
        for b in range(z_ref.shape[0]):
            o_ref[b] = mix(cl_ref[...], sl_ref[...], z_ref[b, :n_lat, :])


def _position_dft(z, cos_lat, msin_lat, cos_ctx, msin_ctx, n_lat):
    B, R, _ = z.shape
    n_ctx = R - n_lat
    bb = DFT_BATCH if B % DFT_BATCH == 0 else 1
    lat_blocks = n_lat // TQ_DFT
    lat = pl.BlockSpec((TQ_DFT, n_lat), lambda b, j: (jnp.minimum(j, lat_blocks - 1), 0))
    ctx = pl.BlockSpec((n_ctx, n_ctx), lambda b, j: (0, 0))
    return pl.pallas_call(
        functools.partial(_dft_kernel, n_lat=n_lat),
        grid=(B // bb, pl.cdiv(R, TQ_DFT)),
        in_specs=[pl.BlockSpec((bb, R, 512), lambda b, j: (b, 0, 0), pipeline_mode=pl.Buffered(1)),
                  lat, lat, ctx, ctx],
        out_specs=pl.BlockSpec((bb, TQ_DFT, 256), lambda b, j: (b, j, 0)),
        out_shape=jax.ShapeDtypeStruct((B, R, 256), BF16),
        compiler_params=_cparams(("parallel", "parallel"), 48),
        name="position_dft",
    )(z, cos_lat, msin_lat, cos_ctx, msin_ctx)


def _block_tail_kernel(of_ref, ob_ref, z_ref, ga_ref, wa_ref, ft_ref, x_ref, mod_ref, ng_ref, ones_ref, w_ref,
                       g2_ref, wg_ref, wu_ref, wd_ref, o_ref):
    o = of_ref[0] + ob_ref[0]
    z = z_ref[0]
    ones_bd = ones_ref[...]
    acc = None
    for s in range(2):
        t = o[:, s * LANES:(s + 1) * LANES]
        y = t * lax.rsqrt(_head_sumsq(t, ones_bd) * (1.0 / HEAD_DIM) + EPS) * ng_ref[...]
        y = (y * _silu(z[:, s * LANES:(s + 1) * LANES])).astype(BF16)
        part = jnp.dot(y, w_ref[s * LANES:(s + 1) * LANES, :], preferred_element_type=F32)
        acc = part if acc is None else acc + part
    for n, ref in enumerate((ga_ref, wa_ref, ft_ref)):
        acc = acc + jnp.dot(ref[0], w_ref[256 * (n + 1):256 * (n + 2), :], preferred_element_type=F32)
    x = x_ref[0] + mod_ref[0, 2:3, :] * acc

    ms = jnp.mean(x * x, -1, keepdims=True)
    y = x * lax.rsqrt(ms + EPS) * g2_ref[...]
    h = (y * (1.0 + mod_ref[0, 4:5, :]) + mod_ref[0, 3:4, :]).astype(BF16)
    acc = None
    for c in range(FFN_HIDDEN // FFN_CHUNK):
        cs = slice(c * FFN_CHUNK, (c + 1) * FFN_CHUNK)
        g = jnp.dot(h, wg_ref[:, cs], preferred_element_type=F32)
        u = jnp.dot(h, wu_ref[:, cs], preferred_element_type=F32)
        a = (_silu(g) * u).astype(BF16)
        part = jnp.dot(a, wd_ref[cs, :], preferred_element_type=F32)
        acc = part if acc is None else acc + part
    o_ref[0] = x + mod_ref[0, 5:6, :] * acc


def _layer_weight_spec(layer, shape):
    return pl.BlockSpec((None,) + shape, lambda b, j: (layer, 0, 0), pipeline_mode=pl.Buffered(1))


def _block_tail(of, ob, z, y_ga, y_wa, y_ft, xa, mod_l, ng, ones_bd, w_out, g2, wg, wu, wd, layer, n_lat,
                rows_out):
    B = xa.shape[0]
    const2 = lambda b, j: (0, 0)
    tok = lambda w: pl.BlockSpec((1, TM_TAIL, w), lambda b, j: (b, j, 0))
    resident = functools.partial(_layer_weight_spec, layer)
    return pl.pallas_call(
        _block_tail_kernel,
        grid=(B, pl.cdiv(rows_out, TM_TAIL)),
        in_specs=[tok(256)] * 6 + [
            tok(D_MODEL),
            _mod_spec(n_lat // TM_TAIL),
            pl.BlockSpec((1, LANES), const2),
            pl.BlockSpec((LANES, LANES), const2),
            resident((D_MODEL, D_MODEL)),
            pl.BlockSpec((1, D_MODEL), const2),
            resident((D_MODEL, FFN_HIDDEN)),
            resident((D_MODEL, FFN_HIDDEN)),
            resident((FFN_HIDDEN, D_MODEL))],
        out_specs=tok(D_MODEL),
        out_shape=jax.ShapeDtypeStruct((B, rows_out, D_MODEL), F32),
        compiler_params=_cparams(("parallel", "parallel"), 52),
        name="out_projection_ffn",
    )(of, ob, z, y_ga, y_wa, y_ft, xa, mod_l, ng, ones_bd, w_out, g2, wg, wu, wd)


def _rope_tables(n_lat, n_rows):
    t = jnp.arange(n_lat, dtype=jnp.int32)
    row = (t // GRID_W).astype(F32)
    col = (t % GRID_W).astype(F32)
    n_freq = HEAD_DIM // 4
    inv_freq = ROPE_THETA ** (-jnp.arange(n_freq, dtype=F32) / n_freq)
    ang = jnp.concatenate([row[:, None] * inv_freq, col[:, None] * inv_freq], -1)
    cos = jnp.tile(jnp.cos(ang), (1, 4))
    sin = jnp.sin(ang)
    sin = jnp.concatenate([-sin, sin, -sin, sin], -1)
    cos = jnp.concatenate([cos, jnp.ones((n_rows - n_lat, LANES), F32)], 0)
    sin = jnp.concatenate([sin, jnp.zeros((n_rows - n_lat, LANES), F32)], 0)
    return cos, sin


def _dft_matrices(n, n_chan):
    scale = 1.0 / math.sqrt(n * n_chan)
    p = 64 if n % 64 == 0 and n > 64 else 1
    q = n // p
    col = jnp.arange(n, dtype=jnp.int32)[None, :]

    def angles(rows, period):
        return ((jnp.arange(rows, dtype=jnp.int32)[:, None] * col) % period).astype(F32) * (2.0 * math.pi / period)

    ang_a = angles(p, p)
    ang_b = angles(q, n)
    ca, sa = jnp.cos(ang_a)[:, None, :], jnp.sin(ang_a)[:, None, :]
    cb, sb = jnp.cos(ang_b)[None, :, :] * scale, jnp.sin(ang_b)[None, :, :] * scale
    cos = (ca * cb - sa * sb).reshape(n, n)
    msin = (-(sa * cb + ca * sb)).reshape(n, n)
    return cos.astype(BF16), msin.astype(BF16)


def _channel_dft_weights():
    n = HEAD_DIM
    idx = np.arange(n)
    ang = 2.0 * np.pi * ((idx[:, None] * idx[None, :]) % n) / n
    eye = np.eye(4)
    w = np.concatenate([np.kron(eye, np.cos(ang)), np.kron(eye, np.sin(ang))], 1)
    return jnp.asarray(w, F32).astype(BF16)


def kernel(x, c, ctx, c_ctx, norm1_g, norm2_g, w_ada, b_ada, w_in, dn_conv_w, dn_A_log, dn_dt_bias, dn_norm_g,
           ga_q_norm, ga_k_norm, wa_q_norm, wa_k_norm, wa_sink, w_out, w_ffn_gate, w_ffn_up, w_ffn_down):
    B, S, _ = x.shape
    L = ctx.shape[1]
    depth = w_in.shape[0]
    R = S + L
    assert S % TM == 0 and S % TM_TAIL == 0 and L <= TM_TAIL and B <= MOD_ROWS // 2
    assert all(S % t == 0 and L <= t for t in (TQ_WINDOW, TQ_DFT, TQ_GLOBAL))

    cc = jnp.zeros((MOD_ROWS, D_MODEL), F32).at[:B].set(c).at[MOD_ROWS // 2].set(c_ctx)
    mod = _modulation(cc, w_ada, b_ada).reshape(depth, MOD_ROWS, N_MOD, D_MODEL)

    cos_t, sin_t = _rope_tables(S, pl.cdiv(R, TM) * TM)
    cos_lat, msin_lat = _dft_matrices(S, HEAD_DIM)
    cos_ctx, msin_ctx = _dft_matrices(L, HEAD_DIM)
    ftw = _channel_dft_weights()
    ones_bd = jnp.asarray(np.kron(np.eye(2), np.ones((HEAD_DIM, HEAD_DIM))), F32).astype(BF16)
    tile2 = lambda g: jnp.tile(g, 2)

    w_in_r = jnp.concatenate([w_in[:, :, :1024], w_in[:, :, 1040:2320], w_in[:, :, 1024:1040],
                              jnp.zeros((depth, D_MODEL, C_END - 2320), F32)], 2).astype(BF16)
    w_out_b, w_gate_b, w_up_b, w_down_b = [t.astype(BF16) for t in (w_out, w_ffn_gate, w_ffn_up, w_ffn_down)]

    xa = jnp.concatenate([x, ctx], 1)
    for l in range(depth):
        last = l == depth - 1
        gains = jnp.zeros((8, LANES), F32)
        for n, g in enumerate((ga_q_norm[l], ga_k_norm[l], wa_q_norm[l], wa_k_norm[l])):
            gains = gains.at[n].set(tile2(g))
        (qkv, z, ab, gaq, gak, gav, waq, wak, wav, ftz) = _in_projection(
            xa, mod[l], norm1_g[l][None, :], w_in_r, cos_t, sin_t, gains, ones_bd, ftw, l, S)

        conv_w8 = jnp.zeros((8, 768), F32).at[:3].set(dn_conv_w[l])
        neg_a = jnp.zeros((1, LANES), F32).at[0, :8].set(-jnp.exp(dn_A_log[l].astype(F32)).reshape(-1))
        dt_b = jnp.zeros((1, LANES), F32).at[0, :8].set(dn_dt_bias[l].astype(F32).reshape(-1))
        dq, dk, dv, gb = _dn_prep(qkv, ab, conv_w8, neg_a, dt_b, ones_bd, S)
        gt = gb[:, :, :8].reshape(B, R // CHUNK, CHUNK, 8).transpose(0, 1, 3, 2)
        o_f, o_b = _dn_scan(dq, dk, dv, gb, gt, S)

        y_ga = _global_attention(gaq, gak, gav, S)
        y_wa = _window_attention(waq, wak, wav, wa_sink[l].astype(F32), S)
        y_ft = _position_dft(ftz, cos_lat, msin_lat, cos_ctx, msin_ctx, S)

        rows_out = S if last else R
        xa = _block_tail(o_f, o_b, z, y_ga, y_wa, y_ft, xa, mod[l], tile2(dn_norm_g[l])[None, :], ones_bd,
                         w_out_b, norm2_g[l][None, :], w_gate_b, w_up_b, w_down_b, l, S, rows_out)
    return xa
```

```python
import functools
import math

import jax
import jax.numpy as jnp
import numpy as np
from jax import lax
from jax.experimental import pallas as pl
from jax.experimental.pallas import tpu as pltpu

F32 = jnp.float32
BF16 = jnp.bfloat16
HIGHEST = lax.Precision.HIGHEST

D_MODEL = 1024
HEAD_DIM = 64
N_HEADS = 4
N_KV = 2
GRID_W = 64
CHUNK = 64
WINDOW = 128
ROPE_THETA = 10000.0
FFN_HIDDEN = 2816
N_MOD = 6
EPS = 1e-6
NEG_INF = -1e30

LANES = 128
TM = 512
TM_TAIL = 512
TQ_GLOBAL = 256
TQ_WINDOW = 512
TQ_DFT = 512
DFT_BATCH = 2
MOD_ROWS = 16
FFN_CHUNK = 256
DN_BATCH = 8
LOG2E = math.log2(math.e)

C_QKV, C_Z, C_GA, C_WA, C_FT, C_AB, C_END = 0, 768, 1024, 1536, 2048, 2304, 2432


def _cparams(sem, vmem_mb):
    return pltpu.CompilerParams(dimension_semantics=sem, vmem_limit_bytes=vmem_mb * 1024 * 1024)


def _silu(x):
    return x * (1.0 / (1.0 + jnp.exp(-x)))


def _head_sumsq(t, ones_bd):
    return jnp.dot((t * t).astype(BF16), ones_bd, preferred_element_type=F32)


def _mod_spec(n_lat_blocks):
    ctx_row = MOD_ROWS // 2
    return pl.BlockSpec((1, N_MOD, D_MODEL), lambda b, j: (jnp.where(j == n_lat_blocks, ctx_row, b), 0, 0))


def _mod_kernel(c_ref, w_ref, b_ref, o_ref):
    a = _silu(c_ref[...])
    o_ref[0] = jnp.dot(a, w_ref[0], preferred_element_type=F32, precision=HIGHEST) + b_ref[0]


def _modulation(cc, w_ada, b_ada):
    depth = w_ada.shape[0]
    return pl.pallas_call(
        _mod_kernel,
        grid=(depth, N_MOD),
        in_specs=[pl.BlockSpec((MOD_ROWS, D_MODEL), lambda l, n: (0, 0)),
                  pl.BlockSpec((1, D_MODEL, D_MODEL), lambda l, n: (l, 0, n)),
                  pl.BlockSpec((1, 1, D_MODEL), lambda l, n: (l, 0, n))],
        out_specs=pl.BlockSpec((1, MOD_ROWS, D_MODEL), lambda l, n: (l, 0, n)),
        out_shape=jax.ShapeDtypeStruct((depth, MOD_ROWS, N_MOD * D_MODEL), F32),
        compiler_params=_cparams(("parallel", "parallel"), 32),
        name="modulation",
    )(cc, w_ada, b_ada.reshape(depth, 1, N_MOD * D_MODEL))


def _inproj_kernel(x_ref, mod_ref, g1_ref, w_ref, cos_ref, sin_ref, gains_ref, ones_ref, ftw_ref,
                   qkv_ref, z_ref, ab_ref, gaq_ref, gak_ref, gav_ref, waq_ref, wak_ref, wav_ref, ft_ref):
    x = x_ref[0]
    ms = jnp.mean(x * x, -1, keepdims=True)
    y = x * lax.rsqrt(ms + EPS) * g1_ref[...]
    h = (y * (1.0 + mod_ref[0, 1:2, :]) + mod_ref[0, 0:1, :]).astype(BF16)

    def proj(c0, c1):
        return jnp.dot(h, w_ref[:, c0:c1], preferred_element_type=F32)

    qkv_ref[0] = proj(C_QKV, C_Z)
    z_ref[0] = proj(C_Z, C_GA)
    ab_ref[0] = proj(C_AB, C_END)

    ones_bd = ones_ref[...]
    cos_t = cos_ref[...]
    sin_t = sin_ref[...]
    lane = lax.broadcasted_iota(jnp.int32, cos_t.shape, 1)
    first_half = (lane & (HEAD_DIM // 2)) == 0

    def norm_rope(t, gain_row, scale):
        yv = t * lax.rsqrt(_head_sumsq(t, ones_bd) * (1.0 / HEAD_DIM) + EPS) * gain_row
        partner = jnp.where(first_half, pltpu.roll(yv, LANES - HEAD_DIM // 2, 1), pltpu.roll(yv, HEAD_DIM // 2, 1))
        return (yv * cos_t + partner * sin_t) * scale

    def heads_out(ref, t, n):
        for i in range(n):
            ref[0, i] = t[:, i * HEAD_DIM:(i + 1) * HEAD_DIM].astype(BF16)

    for c0, q_ref, k_ref, v_ref, gq, gk in ((C_GA, gaq_ref, gak_ref, gav_ref, 0, 1),
                                            (C_WA, waq_ref, wak_ref, wav_ref, 2, 3)):
        p = proj(c0, c0 + 512)
        scale = HEAD_DIM ** -0.5 * LOG2E
        q = jnp.concatenate([norm_rope(p[:, 0:128], gains_ref[gq:gq + 1, :], scale),
                             norm_rope(p[:, 128:256], gains_ref[gq:gq + 1, :], scale)], -1)
        k = norm_rope(p[:, 256:384], gains_ref[gk:gk + 1, :], 1.0)
        heads_out(q_ref, q, N_HEADS)
        heads_out(k_ref, k, N_KV)
        for i in range(N_KV):
            vh = p[:, 384 + i * HEAD_DIM:384 + (i + 1) * HEAD_DIM]
            v_ref[0, i] = jnp.where(lane < HEAD_DIM, jnp.concatenate([vh, vh], -1),
                                    jnp.where(lane == HEAD_DIM, 1.0, 0.0)).astype(BF16)

    pf = proj(C_FT, C_AB).astype(BF16)
    ft_ref[0] = jnp.dot(pf, ftw_ref[...], preferred_element_type=F32).astype(BF16)


def _in_projection(xa, mod_l, g1, w_in_r, cos_t, sin_t, gains, ones_bd, ftw, layer, n_lat):
    B, R, _ = xa.shape
    nb = pl.cdiv(R, TM)
    const2 = lambda b, j: (0, 0)
    tok = lambda w: pl.BlockSpec((1, TM, w), lambda b, j: (b, j, 0))
    hd = lambda n, w=HEAD_DIM: pl.BlockSpec((1, n, TM, w), lambda b, j: (b, 0, j, 0))
    sds = jax.ShapeDtypeStruct
    heads = lambda n, w=HEAD_DIM: sds((B, n, R, w), BF16)
    return pl.pallas_call(
        _inproj_kernel,
        grid=(B, nb),
        in_specs=[tok(D_MODEL),
                  _mod_spec(n_lat // TM),
                  pl.BlockSpec((1, D_MODEL), const2),
                  _layer_weight_spec(layer, (D_MODEL, C_END)),
                  pl.BlockSpec((TM, LANES), lambda b, j: (j, 0)),
                  pl.BlockSpec((TM, LANES), lambda b, j: (j, 0)),
                  pl.BlockSpec((8, LANES), const2),
                  pl.BlockSpec((LANES, LANES), const2),
                  pl.BlockSpec((256, 512), const2)],
        out_specs=[tok(768), tok(256), tok(LANES),
                   hd(N_HEADS), hd(N_KV), hd(N_KV, LANES), hd(N_HEADS), hd(N_KV), hd(N_KV, LANES),
                   tok(512)],
        out_shape=[sds((B, R, 768), F32), sds((B, R, 256), F32), sds((B, R, LANES), F32),
                   heads(N_HEADS), heads(N_KV), heads(N_KV, LANES),
                   heads(N_HEADS), heads(N_KV), heads(N_KV, LANES),
                   sds((B, R, 512), BF16)],
        compiler_params=_cparams(("parallel", "parallel"), 48),
        name="in_projection",
    )(xa, mod_l, g1, w_in_r, cos_t, sin_t, gains, ones_bd, ftw)


def _dnprep_kernel(cur_ref, prev_ref, next_ref, ab_ref, cw_ref, na_ref, dtb_ref, ones_ref,
                   q_ref, k_ref, v_ref, gb_ref, *, lat_blocks, n_ctx):
    j = pl.program_id(1)
    cur = cur_ref[0]
    tp = cur.shape[0]
    row = lax.broadcasted_iota(jnp.int32, cur.shape, 0)
    is_ctx = j == lat_blocks
    has_prev = jnp.where((j == 0) | is_ctx, 0.0, 1.0)
    has_next = jnp.where(j >= lat_blocks - 1, 0.0, 1.0)
    last_row = jnp.where(is_ctx, n_ctx - 1, tp - 1)
    prev_row = prev_ref[0, 7:8, :] * has_prev
    next_row = next_ref[0, 0:1, :] * has_next
    up = jnp.where(row == 0, prev_row, pltpu.roll(cur, 1, 0))
    dn = jnp.where(row == last_row, next_row, pltpu.roll(cur, tp - 1, 0))
    a = _silu(up * cw_ref[0:1, :] + cur * cw_ref[1:2, :] + dn * cw_ref[2:3, :])

    ones_bd = ones_ref[...]

    for s in range(2):
        lanes = slice(s * LANES, (s + 1) * LANES)
        tq = a[:, lanes]
        q_ref[0, :, lanes] = tq * lax.rsqrt(_head_sumsq(tq, ones_bd) + EPS) * HEAD_DIM ** -0.5
        tk = a[:, 256 + s * LANES:256 + (s + 1) * LANES]
        k_ref[0, :, lanes] = tk * lax.rsqrt(_head_sumsq(tk, ones_bd) + EPS)
    v_ref[0] = a[:, 512:768]

    ab = ab_ref[0]
    t = ab + dtb_ref[...]
    softplus = jnp.maximum(t, 0.0) + jnp.log(1.0 + jnp.exp(-jnp.abs(t)))
    g = na_ref[...] * softplus
    beta = 1.0 / (1.0 + jnp.exp(-ab))
    lane = lax.broadcasted_iota(jnp.int32, (CHUNK, LANES), 1)
    ii = lax.broadcasted_iota(jnp.int32, (CHUNK, CHUNK), 0)
    jj = lax.broadcasted_iota(jnp.int32, (CHUNK, CHUNK), 1)
    prefix = jnp.where(ii >= jj, 1.0, 0.0)
    suffix = jnp.where(ii <= jj, 1.0, 0.0)
    for c in range(tp // CHUNK):
        rows = slice(c * CHUNK, (c + 1) * CHUNK)
        gc = g[rows]
        fwd = jnp.dot(prefix, gc, preferred_element_type=F32, precision=HIGHEST)
        bwd = jnp.dot(suffix, gc, preferred_element_type=F32, precision=HIGHEST)
        gb_ref[0, rows, :] = jnp.where(lane < N_HEADS, fwd, jnp.where(lane < 2 * N_HEADS, bwd, beta[rows]))


def _dn_prep(qkv, ab, conv_w8, neg_a, dt_b, ones_bd, n_lat):
    B, R, _ = qkv.shape
    nb = pl.cdiv(R, TM)
    hb = TM // 8
    const2 = lambda b, j: (0, 0)
    tok = lambda w: pl.BlockSpec((1, TM, w), lambda b, j: (b, j, 0))
    sds = jax.ShapeDtypeStruct
    return pl.pallas_call(
        functools.partial(_dnprep_kernel, lat_blocks=n_lat // TM, n_ctx=R - n_lat),
        grid=(B, nb),
        in_specs=[tok(768),
                  pl.BlockSpec((1, 8, 768), lambda b, j: (b, jnp.maximum(j * hb - 1, 0), 0)),
                  pl.BlockSpec((1, 8, 768), lambda b, j: (b, jnp.minimum((j + 1) * hb, R // 8 - 1), 0)),
                  tok(LANES),
                  pl.BlockSpec((8, 768), const2),
                  pl.BlockSpec((1, LANES), const2),
                  pl.BlockSpec((1, LANES), const2),
                  pl.BlockSpec((LANES, LANES), const2)],
        out_specs=[tok(256), tok(256), tok(256), tok(LANES)],
        out_shape=[sds((B, R, 256), F32)] * 3 + [sds((B, R, LANES), F32)],
        compiler_params=_cparams(("parallel", "parallel"), 32),
        name="deltanet_prep",
    )(qkv, qkv, qkv, ab, conv_w8, neg_a, dt_b, ones_bd)


def _bdot(a, b, dims):
    return lax.dot_general(a.astype(BF16), b.astype(BF16), dims, preferred_element_type=F32)


_NN = (((2,), (1,)), ((0,), (0,)))
_NT = (((2,), (2,)), ((0,), (0,)))
_TN = (((1,), (1,)), ((0,), (0,)))


def _pair_blockdiag(x):
    first = (lax.broadcasted_iota(jnp.int32, x.shape, 2) & HEAD_DIM) == 0
    return jnp.concatenate([jnp.where(first, x, 0.0), jnp.where(first, 0.0, x)], 1)


def _dnscan_kernel(qf_ref, kf_ref, vf_ref, gbf_ref, gtf_ref, qb_ref, kb_ref, vb_ref, gbb_ref, gtb_ref,
                   of_ref, ob_ref, sf_ref, sb_ref, *, bb):
    i = pl.program_id(1)

    @pl.when(i == 0)
    def _():
        sf_ref[...] = jnp.zeros_like(sf_ref)
        sb_ref[...] = jnp.zeros_like(sb_ref)

    _dn_direction(qf_ref, kf_ref, vf_ref, gbf_ref, gtf_ref, of_ref, sf_ref, bb, 0)
    _dn_direction(qb_ref, kb_ref, vb_ref, gbb_ref, gtb_ref, ob_ref, sb_ref, bb, 1)


def _dn_direction(q_ref, k_ref, v_ref, gb_ref, gt_ref, o_ref, state_ref, bb, d):
    lane2 = lax.broadcasted_iota(jnp.int32, (CHUNK, LANES), 1)
    first2 = (lane2 & HEAD_DIM) == 0
    last = CHUNK - 1 if d == 0 else 0
    gc_cols, b_cols, gc_rows, g_lasts, qs, ks, vs = [], [], [], [], [], [], []
    for b in range(bb):
        gb = gb_ref[b]
        gt = gt_ref[b, 0]
        for m in range(N_HEADS // 2):
            c = d * N_HEADS + 2 * m
            lanes = slice(m * LANES, (m + 1) * LANES)
            qs.append(q_ref[b, :, lanes])
            ks.append(k_ref[b, :, lanes])
            vs.append(v_ref[b, :, lanes])
            col = jnp.where(first2, gb[:, c:c + 1], gb[:, c + 1:c + 2])
            gc_cols.append(col)
            g_lasts.append(col[last:last + 1, :])
            b_cols.append(jnp.where(first2, gb[:, 2 * N_HEADS + c:2 * N_HEADS + c + 1],
                                    gb[:, 2 * N_HEADS + c + 1:2 * N_HEADS + c + 2]))
            gc_rows.append(jnp.concatenate([gt[c:c + 1, :], gt[c + 1:c + 2, :]], 1))
    q = jnp.stack(qs)
    k = jnp.stack(ks)
    v = jnp.stack(vs)
    gc_col = jnp.stack(gc_cols)
    b_col = jnp.stack(b_cols)
    gc_row = jnp.stack(gc_rows)
    g_last = jnp.stack(g_lasts)
    shape = q.shape

    ii = lax.broadcasted_iota(jnp.int32, (CHUNK, LANES), 0)
    jj = lane2 & (HEAD_DIM - 1)
    incl = (ii >= jj) if d == 0 else (ii <= jj)
    strict = (ii > jj) if d == 0 else (ii < jj)
    eye = jnp.where(ii == jj, 1.0, 0.0)
    xor = ii ^ jj
    blocks = [(xor >> sh) == 1 for sh in range(6)]
    bd = _pair_blockdiag

    decay = jnp.exp(jnp.where(incl, gc_col - gc_row, NEG_INF))
    e_gc = jnp.exp(gc_col)
    kb = k * b_col
    kq = _bdot(jnp.concatenate([kb, q], 1), bd(k), _NT)
    a_mat = jnp.where(strict, kq[:, :CHUNK] * decay, 0.0)
    intra = jnp.where(incl, kq[:, CHUNK:] * decay, 0.0)
    t = eye - jnp.where(blocks[0], a_mat, 0.0)
    for mask in blocks[1:-1]:
        t = t - _bdot(t, bd(_bdot(jnp.where(mask, a_mat, 0.0), bd(t), _NN)), _NN)
    rhs = jnp.concatenate([v * b_col, kb * e_gc], 2)
    half = _bdot(t, bd(rhs), _NN)
    sol = half - _bdot(t, bd(_bdot(jnp.where(blocks[-1], a_mat, 0.0), bd(half), _NN)), _NN)
    u, w = sol[:, :, :LANES], sol[:, :, LANES:]
    state = state_ref[...]
    wq = _bdot(jnp.concatenate([w, q * e_gc], 1), bd(state), _NN)
    v_new = u - wq[:, :CHUNK]
    o = wq[:, CHUNK:] + _bdot(intra, bd(v_new), _NN)
    k_dec = k * jnp.exp(g_last - gc_col)
    kv = _bdot(bd(k_dec), bd(v_new), _TN)
    first3 = (lax.broadcasted_iota(jnp.int32, shape, 2) & HEAD_DIM) == 0
    state_ref[...] = state * jnp.exp(g_last) + jnp.where(first3, kv[:, :CHUNK], kv[:, CHUNK:])

    for b in range(bb):
        for m in range(N_HEADS // 2):
            o_ref[b, :, m * LANES:(m + 1) * LANES] = o[b * (N_HEADS // 2) + m]


def _dn_scan(q, k, v, gb, gt, n_lat):
    B, R, _ = q.shape
    nc = R // CHUNK
    nlc = n_lat // CHUNK
    ncc = nc - nlc
    bb = DN_BATCH if B % DN_BATCH == 0 else 1

    def fwd_chunk(i):
        return jnp.where(i < ncc, nlc + i, i - ncc)

    def bwd_chunk(i):
        return jnp.where(i < ncc, nc - 1 - i, nlc - 1 - (i - ncc))

    tokf = lambda w: pl.BlockSpec((bb, CHUNK, w), lambda b, i: (b, fwd_chunk(i), 0))
    tokb = lambda w: pl.BlockSpec((bb, CHUNK, w), lambda b, i: (b, bwd_chunk(i), 0))
    gtf = pl.BlockSpec((bb, 1, 8, CHUNK), lambda b, i: (b, fwd_chunk(i), 0, 0))
    gtb = pl.BlockSpec((bb, 1, 8, CHUNK), lambda b, i: (b, bwd_chunk(i), 0, 0))
    sds = jax.ShapeDtypeStruct
    return pl.pallas_call(
        functools.partial(_dnscan_kernel, bb=bb),
        grid=(B // bb, nc),
        in_specs=[tokf(256), tokf(256), tokf(256), tokf(LANES), gtf,
                  tokb(256), tokb(256), tokb(256), tokb(LANES), gtb],
        out_specs=[tokf(256), tokb(256)],
        out_shape=[sds((B, R, 256), F32), sds((B, R, 256), F32)],
        scratch_shapes=[pltpu.VMEM((N_HEADS // 2 * bb, HEAD_DIM, LANES), F32)] * 2,
        compiler_params=_cparams(("parallel", "arbitrary"), 32),
        name="deltanet_scan",
    )(q, k, v, gb, gt, q, k, v, gb, gt)


def _softmax_pv(s_list, v_list, sink):
    m = functools.reduce(jnp.maximum, [s.max(-1, keepdims=True) for s in s_list])
    if sink is not None:
        m = jnp.maximum(m, sink)
    ov = functools.reduce(lambda a, b: a + b, [jnp.dot(jnp.exp2(s - m).astype(BF16), v, preferred_element_type=F32)
                                               for s, v in zip(s_list, v_list)])
    den = ov[:, HEAD_DIM:HEAD_DIM + 1]
    if sink is not None:
        den = den + jnp.exp2(sink - m)
    return ov[:, :HEAD_DIM] / den


def _scores(q, k):
    return lax.dot_general(q, k, (((1,), (1,)), ((), ())), preferred_element_type=F32)


def _store_heads(o_ref, heads, rows):
    o_ref[0, :rows, :] = jnp.concatenate(heads, -1).astype(BF16)


def _gattn_kernel(q_ref, k_ref, v_ref, o_ref, *, n_lat):
    j = pl.program_id(1)
    tq = q_ref.shape[2]
    n_ctx = k_ref.shape[2] - n_lat
    group = N_HEADS // N_KV

    @pl.when(j == n_lat // tq)
    def _():
        _store_heads(o_ref, [_softmax_pv([_scores(q_ref[0, h, :n_ctx, :], k_ref[0, h // group, n_lat:, :])],
                                         [v_ref[0, h // group, n_lat:, :]], None)
                             for h in range(N_HEADS)], n_ctx)

    @pl.when(j < n_lat // tq)
    def _():
        _store_heads(o_ref, [_softmax_pv([_scores(q_ref[0, h], k_ref[0, h // group])], [v_ref[0, h // group]],
                                         None) for h in range(N_HEADS)], tq)


def _global_attention(q, k, v, n_lat):
    B, _, R, _ = q.shape
    tq = TQ_GLOBAL
    return pl.pallas_call(
        functools.partial(_gattn_kernel, n_lat=n_lat),
        grid=(B, pl.cdiv(R, tq)),
        in_specs=[pl.BlockSpec((1, N_HEADS, tq, HEAD_DIM), lambda b, j: (b, 0, j, 0)),
                  pl.BlockSpec((1, N_KV, R, HEAD_DIM), lambda b, j: (b, 0, 0, 0)),
                  pl.BlockSpec((1, N_KV, R, LANES), lambda b, j: (b, 0, 0, 0))],
        out_specs=pl.BlockSpec((1, tq, N_HEADS * HEAD_DIM), lambda b, j: (b, j, 0)),
        out_shape=jax.ShapeDtypeStruct((B, R, N_HEADS * HEAD_DIM), BF16),
        compiler_params=_cparams(("parallel", "parallel"), 56),
        name="global_attention",
    )(q, k, v)


def _wattn_kernel(sink_ref, bias_ref, q_ref, k_ref, v_ref, o_ref, *, n_lat):
    j = pl.program_id(1)
    tq = q_ref.shape[2]
    n_ctx = k_ref.shape[2] - n_lat
    band = tq + 2 * WINDOW
    group = N_HEADS // N_KV
    sinks = [sink_ref[h] * LOG2E for h in range(N_HEADS)]

    @pl.when(j == n_lat // tq)
    def _():
        _store_heads(o_ref, [_softmax_pv([_scores(q_ref[0, h, :n_ctx, :], k_ref[0, h // group, n_lat:, :])],
                                         [v_ref[0, h // group, n_lat:, :]], sinks[h])
                             for h in range(N_HEADS)], n_ctx)

    @pl.when(j < n_lat // tq)
    def _():
        q0 = j * tq
        start = pl.multiple_of(jnp.clip(q0 - WINDOW, 0, n_lat - band), WINDOW)
        bias = bias_ref[(q0 - start) // WINDOW]
        heads = []
        for h in range(N_HEADS):
            kv = h // group
            q = q_ref[0, h]
            s_loc = _scores(q, k_ref[0, kv, pl.ds(start, band), :]) + bias
            s_ctx = _scores(q, k_ref[0, kv, n_lat:, :])
            heads.append(_softmax_pv([s_loc, s_ctx], [v_ref[0, kv, pl.ds(start, band), :],
                                                      v_ref[0, kv, n_lat:, :]], sinks[h]))
        _store_heads(o_ref, heads, tq)


def _window_bias(tq):
    band = tq + 2 * WINDOW
    off = jnp.arange(3, dtype=jnp.int32)[:, None, None] * WINDOW
    r = jnp.arange(tq, dtype=jnp.int32)[None, :, None]
    c = jnp.arange(band, dtype=jnp.int32)[None, None, :]
    return jnp.where(jnp.abs(r + off - c) <= WINDOW, 0.0, NEG_INF).astype(F32)


def _window_attention(q, k, v, sink, n_lat):
    B, _, R, _ = q.shape
    tq = TQ_WINDOW
    bias = _window_bias(tq)
    return pl.pallas_call(
        functools.partial(_wattn_kernel, n_lat=n_lat),
        grid=(B, pl.cdiv(R, tq)),
        in_specs=[pl.BlockSpec(memory_space=pltpu.SMEM),
                  pl.BlockSpec(bias.shape, lambda b, j: (0, 0, 0), pipeline_mode=pl.Buffered(1)),
                  pl.BlockSpec((1, N_HEADS, tq, HEAD_DIM), lambda b, j: (b, 0, j, 0)),
                  pl.BlockSpec((1, N_KV, R, HEAD_DIM), lambda b, j: (b, 0, 0, 0)),
                  pl.BlockSpec((1, N_KV, R, LANES), lambda b, j: (b, 0, 0, 0))],
        out_specs=pl.BlockSpec((1, tq, N_HEADS * HEAD_DIM), lambda b, j: (b, j, 0)),
        out_shape=jax.ShapeDtypeStruct((B, R, N_HEADS * HEAD_DIM), BF16),
        compiler_params=_cparams(("parallel", "parallel"), 48),
        name="window_attention",
    )(sink, bias, q, k, v)


def _dft_kernel(z_ref, cl_ref, sl_ref, cc_ref, sc_ref, o_ref, *, n_lat):
    j = pl.program_id(1)
    half = z_ref.shape[2] // 2

    def mix(c, s, z):
        return (jnp.dot(c, z[:, :half], preferred_element_type=F32)
                + jnp.dot(s, z[:, half:], preferred_element_type=F32)).astype(BF16)

    n_ctx = z_ref.shape[1] - n_lat

    @pl.when(j == n_lat // TQ_DFT)
    def _():
        for b in range(z_ref.shape[0]):
            o_ref[b, :n_ctx, :] = mix(cc_ref[...], sc_ref[...], z_ref[b, n_lat:, :])

    @pl.when(j < n_lat // TQ_DFT)
    def _():
        for b in range(z_ref.shape[0]):
            o_ref[b] = mix(cl_ref[...], sl_ref[...], z_ref[b, :n_lat, :])


def _position_dft(z, cos_lat, msin_lat, cos_ctx, msin_ctx, n_lat):
    B, R, _ = z.shape
    n_ctx = R - n_lat
    bb = DFT_BATCH if B % DFT_BATCH == 0 else 1
    lat_blocks = n_lat // TQ_DFT
    lat = pl.BlockSpec((TQ_DFT, n_lat), lambda b, j: (jnp.minimum(j, lat_blocks - 1), 0))
    ctx = pl.BlockSpec((n_ctx, n_ctx), lambda b, j: (0, 0))
    return pl.pallas_call(
        functools.partial(_dft_kernel, n_lat=n_lat),
        grid=(B // bb, pl.cdiv(R, TQ_DFT)),
        in_specs=[pl.BlockSpec((bb, R, 512), lambda b, j: (b, 0, 0), pipeline_mode=pl.Buffered(1)),
                  lat, lat, ctx, ctx],
        out_specs=pl.BlockSpec((bb, TQ_DFT, 256), lambda b, j: (b, j, 0)),
        out_shape=jax.ShapeDtypeStruct((B, R, 256), BF16),
        compiler_params=_cparams(("parallel", "parallel"), 48),
        name="position_dft",
    )(z, cos_lat, msin_lat, cos_ctx, msin_ctx)


def _block_tail_kernel(of_ref, ob_ref, z_ref, ga_ref, wa_ref, ft_ref, x_ref, mod_ref, ng_ref, ones_ref, w_ref,
                       g2_ref, wg_ref, wu_ref, wd_ref, o_ref):
    o = of_ref[0] + ob_ref[0]
    z = z_ref[0]
    ones_bd = ones_ref[...]
    acc = None
    for s in range(2):
        t = o[:, s * LANES:(s + 1) * LANES]
        y = t * lax.rsqrt(_head_sumsq(t, ones_bd) * (1.0 / HEAD_DIM) + EPS) * ng_ref[...]
        y = (y * _silu(z[:, s * LANES:(s + 1) * LANES])).astype(BF16)
        part = jnp.dot(y, w_ref[s * LANES:(s + 1) * LANES, :], preferred_element_type=F32)
        acc = part if acc is None else acc + part
    for n, ref in enumerate((ga_ref, wa_ref, ft_ref)):
        acc = acc + jnp.dot(ref[0], w_ref[256 * (n + 1):256 * (n + 2), :], preferred_element_type=F32)
    x = x_ref[0] + mod_ref[0, 2:3, :] * acc

    ms = jnp.mean(x * x, -1, keepdims=True)
    y = x * lax.rsqrt(ms + EPS) * g2_ref[...]
    h = (y * (1.0 + mod_ref[0, 4:5, :]) + mod_ref[0, 3:4, :]).astype(BF16)
    acc = None
    for c in range(FFN_HIDDEN // FFN_CHUNK):
        cs = slice(c * FFN_CHUNK, (c + 1) * FFN_CHUNK)
        g = jnp.dot(h, wg_ref[:, cs], preferred_element_type=F32)
        u = jnp.dot(h, wu_ref[:, cs], preferred_element_type=F32)
        a = (_silu(g) * u).astype(BF16)
        part = jnp.dot(a, wd_ref[cs, :], preferred_element_type=F32)
        acc = part if acc is None else acc + part
    o_ref[0] = x + mod_ref[0, 5:6, :] * acc


def _layer_weight_spec(layer, shape):
    return pl.BlockSpec((None,) + shape, lambda b, j: (layer, 0, 0), pipeline_mode=pl.Buffered(1))


def _block_tail(of, ob, z, y_ga, y_wa, y_ft, xa, mod_l, ng, ones_bd, w_out, g2, wg, wu, wd, layer, n_lat,
                rows_out):
    B = xa.shape[0]
    const2 = lambda b, j: (0, 0)
    tok = lambda w: pl.BlockSpec((1, TM_TAIL, w), lambda b, j: (b, j, 0))
    resident = functools.partial(_layer_weight_spec, layer)
    return pl.pallas_call(
        _block_tail_kernel,
        grid=(B, pl.cdiv(rows_out, TM_TAIL)),
        in_specs=[tok(256)] * 6 + [
            tok(D_MODEL),
            _mod_spec(n_lat // TM_TAIL),
            pl.BlockSpec((1, LANES), const2),
            pl.BlockSpec((LANES, LANES), const2),
            resident((D_MODEL, D_MODEL)),
            pl.BlockSpec((1, D_MODEL), const2),
            resident((D_MODEL, FFN_HIDDEN)),
            resident((D_MODEL, FFN_HIDDEN)),
            resident((FFN_HIDDEN, D_MODEL))],
        out_specs=tok(D_MODEL),
        out_shape=jax.ShapeDtypeStruct((B, rows_out, D_MODEL), F32),
        compiler_params=_cparams(("parallel", "parallel"), 52),
        name="out_projection_ffn",
    )(of, ob, z, y_ga, y_wa, y_ft, xa, mod_l, ng, ones_bd, w_out, g2, wg, wu, wd)


def _rope_tables(n_lat, n_rows):
    t = jnp.arange(n_lat, dtype=jnp.int32)
    row = (t // GRID_W).astype(F32)
    col = (t % GRID_W).astype(F32)
    n_freq = HEAD_DIM // 4
    inv_freq = ROPE_THETA ** (-jnp.arange(n_freq, dtype=F32) / n_freq)
    ang = jnp.concatenate([row[:, None] * inv_freq, col[:, None] * inv_freq], -1)
    cos = jnp.tile(jnp.cos(ang), (1, 4))
    sin = jnp.sin(ang)
    sin = jnp.concatenate([-sin, sin, -sin, sin], -1)
    cos = jnp.concatenate([cos, jnp.ones((n_rows - n_lat, LANES), F32)], 0)
    sin = jnp.concatenate([sin, jnp.zeros((n_rows - n_lat, LANES), F32)], 0)
    return cos, sin


def _dft_matrices(n, n_chan):
    scale = 1.0 / math.sqrt(n * n_chan)
    p = 64 if n % 64 == 0 and n > 64 else 1
    q = n // p
    col = jnp.arange(n, dtype=jnp.int32)[None, :]

    def angles(rows, period):
        return ((jnp.arange(rows, dtype=jnp.int32)[:, None] * col) % period).astype(F32) * (2.0 * math.pi / period)

    ang_a = angles(p, p)
    ang_b = angles(q, n)
    ca, sa = jnp.cos(ang_a)[:, None, :], jnp.sin(ang_a)[:, None, :]
    cb, sb = jnp.cos(ang_b)[None, :, :] * scale, jnp.sin(ang_b)[None, :, :] * scale
    cos = (ca * cb - sa * sb).reshape(n, n)
    msin = (-(sa * cb + ca * sb)).reshape(n, n)
    return cos.astype(BF16), msin.astype(BF16)


def _channel_dft_weights():
    n = HEAD_DIM
    idx = np.arange(n)
    ang = 2.0 * np.pi * ((idx[:, None] * idx[None, :]) % n) / n
    eye = np.eye(4)
    w = np.concatenate([np.kron(eye, np.cos(ang)), np.kron(eye, np.sin(ang))], 1)
    return jnp.asarray(w, F32).astype(BF16)


def kernel(x, c, ctx, c_ctx, norm1_g, norm2_g, w_ada, b_ada, w_in, dn_conv_w, dn_A_log, dn_dt_bias, dn_norm_g,
           ga_q_norm, ga_k_norm, wa_q_norm, wa_k_norm, wa_sink, w_out, w_ffn_gate, w_ffn_up, w_ffn_down):
    B, S, _ = x.shape
    L = ctx.shape[1]
    depth = w_in.shape[0]
    R = S + L
    assert S % TM == 0 and S % TM_TAIL == 0 and L <= TM_TAIL and B <= MOD_ROWS // 2
    assert all(S % t == 0 and L <= t for t in (TQ_WINDOW, TQ_DFT, TQ_GLOBAL))

    cc = jnp.zeros((MOD_ROWS, D_MODEL), F32).at[:B].set(c).at[MOD_ROWS // 2].set(c_ctx)
    mod = _modulation(cc, w_ada, b_ada).reshape(depth, MOD_ROWS, N_MOD, D_MODEL)

    cos_t, sin_t = _rope_tables(S, pl.cdiv(R, TM) * TM)
    cos_lat, msin_lat = _dft_matrices(S, HEAD_DIM)
    cos_ctx, msin_ctx = _dft_matrices(L, HEAD_DIM)
    ftw = _channel_dft_weights()
    ones_bd = jnp.asarray(np.kron(np.eye(2), np.ones((HEAD_DIM, HEAD_DIM))), F32).astype(BF16)
    tile2 = lambda g: jnp.tile(g, 2)

    w_in_r = jnp.concatenate([w_in[:, :, :1024], w_in[:, :, 1040:2320], w_in[:, :, 1024:1040],
                              jnp.zeros((depth, D_MODEL, C_END - 2320), F32)], 2).astype(BF16)
    w_out_b, w_gate_b, w_up_b, w_down_b = [t.astype(BF16) for t in (w_out, w_ffn_gate, w_ffn_up, w_ffn_down)]

    xa = jnp.concatenate([x, ctx], 1)
    for l in range(depth):
        last = l == depth - 1
        gains = jnp.zeros((8, LANES), F32)
        for n, g in enumerate((ga_q_norm[l], ga_k_norm[l], wa_q_norm[l], wa_k_norm[l])):
            gains = gains.at[n].set(tile2(g))
        (qkv, z, ab, gaq, gak, gav, waq, wak, wav, ftz) = _in_projection(
            xa, mod[l], norm1_g[l][None, :], w_in_r, cos_t, sin_t, gains, ones_bd, ftw, l, S)

        conv_w8 = jnp.zeros((8, 768), F32).at[:3].set(dn_conv_w[l])
        neg_a = jnp.zeros((1, LANES), F32).at[0, :8].set(-jnp.exp(dn_A_log[l].astype(F32)).reshape(-1))
        dt_b = jnp.zeros((1, LANES), F32).at[0, :8].set(dn_dt_bias[l].astype(F32).reshape(-1))
        dq, dk, dv, gb = _dn_prep(qkv, ab, conv_w8, neg_a, dt_b, ones_bd, S)
        gt = gb[:, :, :8].reshape(B, R // CHUNK, CHUNK, 8).transpose(0, 1, 3, 2)
        o_f, o_b = _dn_scan(dq, dk, dv, gb, gt, S)

        y_ga = _global_attention(gaq, gak, gav, S)
        y_wa = _window_attention(waq, wak, wav, wa_sink[l].astype(F32), S)
        y_ft = _position_dft(ftz, cos_lat, msin_lat, cos_ctx, msin_ctx, S)

        rows_out = S if last else R
        xa = _block_tail(o_f, o_b, z, y_ga, y_wa, y_ft, xa, mod[l], tile2(dn_norm_g[l])[None, :], ones_bd,
                         w_out_b, norm2_g[l][None, :], w_gate_b, w_up_b, w_down_b, l, S, rows_out)
    return xa
```

```python
import functools
import math

import jax
import jax.numpy as jnp
import numpy as np
from jax import lax
from jax.experimental import pallas as pl
from jax.experimental.pallas import tpu as pltpu

F32 = jnp.float32
BF16 = jnp.bfloat16
HIGHEST = lax.Precision.HIGHEST

D_MODEL = 1024
HEAD_DIM = 64
N_HEADS = 4
N_KV = 2
GRID_W = 64
CHUNK = 64
WINDOW = 128
ROPE_THETA = 10000.0
FFN_HIDDEN = 2816
N_MOD = 6
EPS = 1e-6
NEG_INF = -1e30

LANES = 128
TM = 512
TM_TAIL = 512
TQ_GLOBAL = 256
TQ_WINDOW = 512
TQ_DFT = 512
DFT_BATCH = 2
MOD_ROWS = 16
FFN_CHUNK = 256
DN_BATCH = 8
LOG2E = math.log2(math.e)

C_QKV, C_Z, C_GA, C_WA, C_FT, C_AB, C_END = 0, 768, 1024, 1536, 2048, 2304, 2432


def _cparams(sem, vmem_mb):
    return pltpu.CompilerParams(dimension_semantics=sem, vmem_limit_bytes=vmem_mb * 1024 * 1024)


def _silu(x):
    return x * (1.0 / (1.0 + jnp.exp(-x)))


def _head_sumsq(t, ones_bd):
    return jnp.dot((t * t).astype(BF16), ones_bd, preferred_element_type=F32)


def _mod_spec(n_lat_blocks):
    ctx_row = MOD_ROWS // 2
    return pl.BlockSpec((1, N_MOD, D_MODEL), lambda b, j: (jnp.where(j == n_lat_blocks, ctx_row, b), 0, 0))


def _mod_kernel(c_ref, w_ref, b_ref, o_ref):
    a = _silu(c_ref[...])
    o_ref[0] = jnp.dot(a, w_ref[0], preferred_element_type=F32, precision=HIGHEST) + b_ref[0]


def _modulation(cc, w_ada, b_ada):
    depth = w_ada.shape[0]
    return pl.pallas_call(
        _mod_kernel,
        grid=(depth, N_MOD),
        in_specs=[pl.BlockSpec((MOD_ROWS, D_MODEL), lambda l, n: (0, 0)),
                  pl.BlockSpec((1, D_MODEL, D_MODEL), lambda l, n: (l, 0, n)),
                  pl.BlockSpec((1, 1, D_MODEL), lambda l, n: (l, 0, n))],
        out_specs=pl.BlockSpec((1, MOD_ROWS, D_MODEL), lambda l, n: (l, 0, n)),
        out_shape=jax.ShapeDtypeStruct((depth, MOD_ROWS, N_MOD * D_MODEL), F32),
        compiler_params=_cparams(("parallel", "parallel"), 32),
        name="modulation",
    )(cc, w_ada, b_ada.reshape(depth, 1, N_MOD * D_MODEL))


def _inproj_kernel(x_ref, xp_ref, xn_ref, mod_ref, g1_ref, w_ref, cos_ref, sin_ref, gains_ref, ones_ref, ftw_ref,
                   cw_ref, na_ref, dtb_ref,
                   dq_ref, dk_ref, dv_ref, gb_ref, z_ref, gaq_ref, gak_ref, gav_ref, waq_ref, wak_ref, wav_ref,
                   ft_ref, *, lat_blocks, n_ctx):
    j = pl.program_id(1)
    tm = x_ref.shape[1]
    x = jnp.concatenate([x_ref[0], xp_ref[0], xn_ref[0]], 0)
    ms = jnp.mean(x * x, -1, keepdims=True)
    y = x * lax.rsqrt(ms + EPS) * g1_ref[...]
    h_all = (y * (1.0 + mod_ref[0, 1:2, :]) + mod_ref[0, 0:1, :]).astype(BF16)
    h = h_all[:tm]

    def proj(c0, c1):
        return jnp.dot(h, w_ref[:, c0:c1], preferred_element_type=F32)

    z_ref[0] = proj(C_Z, C_GA)
    ones_bd = ones_ref[...]

    qkv = jnp.dot(h_all, w_ref[:, C_QKV:C_Z], preferred_element_type=F32)
    cur = qkv[:tm]
    row = lax.broadcasted_iota(jnp.int32, cur.shape, 0)
    is_ctx = j == lat_blocks
    has_prev = jnp.where((j == 0) | is_ctx, 0.0, 1.0)
    has_next = jnp.where(j >= lat_blocks - 1, 0.0, 1.0)
    last_row = jnp.where(is_ctx, n_ctx - 1, tm - 1)
    prev_row = qkv[tm + 7:tm + 8] * has_prev
    next_row = qkv[tm + 8:tm + 9] * has_next
    up = jnp.where(row == 0, prev_row, pltpu.roll(cur, 1, 0))
    dn = jnp.where(row == last_row, next_row, pltpu.roll(cur, tm - 1, 0))
    a = _silu(up * cw_ref[0:1, :] + cur * cw_ref[1:2, :] + dn * cw_ref[2:3, :])
    for s in range(2):
        lanes = slice(s * LANES, (s + 1) * LANES)
        tq = a[:, lanes]
        dq_ref[0, :, lanes] = tq * lax.rsqrt(_head_sumsq(tq, ones_bd) + EPS) * HEAD_DIM ** -0.5
        tk = a[:, 256 + s * LANES:256 + (s + 1) * LANES]
        dk_ref[0, :, lanes] = tk * lax.rsqrt(_head_sumsq(tk, ones_bd) + EPS)
    dv_ref[0] = a[:, 512:768]

    ab = proj(C_AB, C_END)
    t = ab + dtb_ref[...]
    softplus = jnp.maximum(t, 0.0) + jnp.log(1.0 + jnp.exp(-jnp.abs(t)))
    g = na_ref[...] * softplus
    beta = 1.0 / (1.0 + jnp.exp(-ab))
    lane_c = lax.broadcasted_iota(jnp.int32, (CHUNK, LANES), 1)
    ii = lax.broadcasted_iota(jnp.int32, (CHUNK, CHUNK), 0)
    jj = lax.broadcasted_iota(jnp.int32, (CHUNK, CHUNK), 1)
    prefix = jnp.where(ii >= jj, 1.0, 0.0)
    suffix = jnp.where(ii <= jj, 1.0, 0.0)
    for c in range(tm // CHUNK):
        rows = slice(c * CHUNK, (c + 1) * CHUNK)
        gc = g[rows]
        fwd = jnp.dot(prefix, gc, preferred_element_type=F32, precision=HIGHEST)
        bwd = jnp.dot(suffix, gc, preferred_element_type=F32, precision=HIGHEST)
        gb_ref[0, rows, :] = jnp.where(lane_c < N_HEADS, fwd, jnp.where(lane_c < 2 * N_HEADS, bwd, beta[rows]))

    cos_t = cos_ref[...]
    sin_t = sin_ref[...]
    lane = lax.broadcasted_iota(jnp.int32, cos_t.shape, 1)
    first_half = (lane & (HEAD_DIM // 2)) == 0

    def norm_rope(t, gain_row, scale):
        yv = t * lax.rsqrt(_head_sumsq(t, ones_bd) * (1.0 / HEAD_DIM) + EPS) * gain_row
        partner = jnp.where(first_half, pltpu.roll(yv, LANES - HEAD_DIM // 2, 1), pltpu.roll(yv, HEAD_DIM // 2, 1))
        return (yv * cos_t + partner * sin_t) * scale

    def heads_out(ref, t, n):
        for i in range(n):
            ref[0, i] = t[:, i * HEAD_DIM:(i + 1) * HEAD_DIM].astype(BF16)

    for c0, q_ref, k_ref, v_ref, gq, gk in ((C_GA, gaq_ref, gak_ref, gav_ref, 0, 1),
                                            (C_WA, waq_ref, wak_ref, wav_ref, 2, 3)):
        p = proj(c0, c0 + 512)
        scale = HEAD_DIM ** -0.5 * LOG2E
        q = jnp.concatenate([norm_rope(p[:, 0:128], gains_ref[gq:gq + 1, :], scale),
                             norm_rope(p[:, 128:256], gains_ref[gq:gq + 1, :], scale)], -1)
        k = norm_rope(p[:, 256:384], gains_ref[gk:gk + 1, :], 1.0)
        heads_out(q_ref, q, N_HEADS)
        heads_out(k_ref, k, N_KV)
        for i in range(N_KV):
            vh = p[:, 384 + i * HEAD_DIM:384 + (i + 1) * HEAD_DIM]
            v_ref[0, i] = jnp.where(lane < HEAD_DIM, jnp.concatenate([vh, vh], -1),
                                    jnp.where(lane == HEAD_DIM, 1.0, 0.0)).astype(BF16)

    pf = proj(C_FT, C_AB).astype(BF16)
    ft_ref[0] = jnp.dot(pf, ftw_ref[...], preferred_element_type=F32).astype(BF16)


def _in_projection(xa, mod_l, g1, w_in_r, cos_t, sin_t, gains, ones_bd, ftw, conv_w8, neg_a, dt_b, layer, n_lat):
    B, R, _ = xa.shape
    nb = pl.cdiv(R, TM)
    hb = TM // 8
    const2 = lambda b, j: (0, 0)
    tok = lambda w: pl.BlockSpec((1, TM, w), lambda b, j: (b, j, 0))
    hd = lambda n, w=HEAD_DIM: pl.BlockSpec((1, n, TM, w), lambda b, j: (b, 0, j, 0))
    sds = jax.ShapeDtypeStruct
    heads = lambda n, w=HEAD_DIM: sds((B, n, R, w), BF16)
    return pl.pallas_call(
        functools.partial(_inproj_kernel, lat_blocks=n_lat // TM, n_ctx=R - n_lat),
        grid=(B, nb),
        in_specs=[tok(D_MODEL),
                  pl.BlockSpec((1, 8, D_MODEL), lambda b, j: (b, jnp.maximum(j * hb - 1, 0), 0)),
                  pl.BlockSpec((1, 8, D_MODEL), lambda b, j: (b, jnp.minimum((j + 1) * hb, R // 8 - 1), 0)),
                  _mod_spec(n_lat // TM),
                  pl.BlockSpec((1, D_MODEL), const2),
                  _layer_weight_spec(layer, (D_MODEL, C_END)),
                  pl.BlockSpec((TM, LANES), lambda b, j: (j, 0)),
                  pl.BlockSpec((TM, LANES), lambda b, j: (j, 0)),
                  pl.BlockSpec((8, LANES), const2),
                  pl.BlockSpec((LANES, LANES), const2),
                  pl.BlockSpec((256, 512), const2),
                  pl.BlockSpec((8, 768), const2),
                  pl.BlockSpec((1, LANES), const2),
                  pl.BlockSpec((1, LANES), const2)],
        out_specs=[tok(256), tok(256), tok(256), tok(LANES), tok(256),
                   hd(N_HEADS), hd(N_KV), hd(N_KV, LANES), hd(N_HEADS), hd(N_KV), hd(N_KV, LANES),
                   tok(512)],
        out_shape=[sds((B, R, 256), F32)] * 3 + [sds((B, R, LANES), F32), sds((B, R, 256), F32),
                   heads(N_HEADS), heads(N_KV), heads(N_KV, LANES),
                   heads(N_HEADS), heads(N_KV), heads(N_KV, LANES),
                   sds((B, R, 512), BF16)],
        compiler_params=_cparams(("parallel", "parallel"), 48),
        name="in_projection",
    )(xa, xa, xa, mod_l, g1, w_in_r, cos_t, sin_t, gains, ones_bd, ftw, conv_w8, neg_a, dt_b)


def _bdot(a, b, dims):
    return lax.dot_general(a.astype(BF16), b.astype(BF16), dims, preferred_element_type=F32)


_NN = (((2,), (1,)), ((0,), (0,)))
_NT = (((2,), (2,)), ((0,), (0,)))
_TN = (((1,), (1,)), ((0,), (0,)))


def _pair_blockdiag(x):
    first = (lax.broadcasted_iota(jnp.int32, x.shape, 2) & HEAD_DIM) == 0
    return jnp.concatenate([jnp.where(first, x, 0.0), jnp.where(first, 0.0, x)], 1)


def _dnscan_kernel(qf_ref, kf_ref, vf_ref, gbf_ref, gtf_ref, qb_ref, kb_ref, vb_ref, gbb_ref, gtb_ref,
                   of_ref, ob_ref, sf_ref, sb_ref, *, bb):
    i = pl.program_id(1)

    @pl.when(i == 0)
    def _():
        sf_ref[...] = jnp.zeros_like(sf_ref)
        sb_ref[...] = jnp.zeros_like(sb_ref)

    _dn_direction(qf_ref, kf_ref, vf_ref, gbf_ref, gtf_ref, of_ref, sf_ref, bb, 0)
    _dn_direction(qb_ref, kb_ref, vb_ref, gbb_ref, gtb_ref, ob_ref, sb_ref, bb, 1)


def _dn_direction(q_ref, k_ref, v_ref, gb_ref, gt_ref, o_ref, state_ref, bb, d):
    lane2 = lax.broadcasted_iota(jnp.int32, (CHUNK, LANES), 1)
    first2 = (lane2 & HEAD_DIM) == 0
    last = CHUNK - 1 if d == 0 else 0
    gc_cols, b_cols, gc_rows, g_lasts, qs, ks, vs = [], [], [], [], [], [], []
    for b in range(bb):
        gb = gb_ref[b]
        gt = gt_ref[b, 0]
        for m in range(N_HEADS // 2):
            c = d * N_HEADS + 2 * m
            lanes = slice(m * LANES, (m + 1) * LANES)
            qs.append(q_ref[b, :, lanes])
            ks.append(k_ref[b, :, lanes])
            vs.append(v_ref[b, :, lanes])
            col = jnp.where(first2, gb[:, c:c + 1], gb[:, c + 1:c + 2])
            gc_cols.append(col)
            g_lasts.append(col[last:last + 1, :])
            b_cols.append(jnp.where(first2, gb[:, 2 * N_HEADS + c:2 * N_HEADS + c + 1],
                                    gb[:, 2 * N_HEADS + c + 1:2 * N_HEADS + c + 2]))
            gc_rows.append(jnp.concatenate([gt[c:c + 1, :], gt[c + 1:c + 2, :]], 1))
    q = jnp.stack(qs)
    k = jnp.stack(ks)
    v = jnp.stack(vs)
    gc_col = jnp.stack(gc_cols)
    b_col = jnp.stack(b_cols)
    gc_row = jnp.stack(gc_rows)
    g_last = jnp.stack(g_lasts)
    shape = q.shape

    ii = lax.broadcasted_iota(jnp.int32, (CHUNK, LANES), 0)
    jj = lane2 & (HEAD_DIM - 1)
    incl = (ii >= jj) if d == 0 else (ii <= jj)
    strict = (ii > jj) if d == 0 else (ii < jj)
    eye = jnp.where(ii == jj, 1.0, 0.0)
    xor = ii ^ jj
    blocks = [(xor >> sh) == 1 for sh in range(6)]
    bd = _pair_blockdiag

    decay = jnp.exp(jnp.where(incl, gc_col - gc_row, NEG_INF))
    e_gc = jnp.exp(gc_col)
    kb = k * b_col
    kq = _bdot(jnp.concatenate([kb, q], 1), bd(k), _NT)
    a_mat = jnp.where(strict, kq[:, :CHUNK] * decay, 0.0)
    intra = jnp.where(incl, kq[:, CHUNK:] * decay, 0.0)
    t = eye - jnp.where(blocks[0], a_mat, 0.0)
    for mask in blocks[1:-1]:
        t = t - _bdot(t, bd(_bdot(jnp.where(mask, a_mat, 0.0), bd(t), _NN)), _NN)
    rhs = jnp.concatenate([v * b_col, kb * e_gc], 2)
    half = _bdot(t, bd(rhs), _NN)
    sol = half - _bdot(t, bd(_bdot(jnp.where(blocks[-1], a_mat, 0.0), bd(half), _NN)), _NN)
    u, w = sol[:, :, :LANES], sol[:, :, LANES:]
    state = state_ref[...]
    wq = _bdot(jnp.concatenate([w, q * e_gc], 1), bd(state), _NN)
    v_new = u - wq[:, :CHUNK]
    o = wq[:, CHUNK:] + _bdot(intra, bd(v_new), _NN)
    k_dec = k * jnp.exp(g_last - gc_col)
    kv = _bdot(bd(k_dec), bd(v_new), _TN)
    first3 = (lax.broadcasted_iota(jnp.int32, shape, 2) & HEAD_DIM) == 0
    state_ref[...] = state * jnp.exp(g_last) + jnp.where(first3, kv[:, :CHUNK], kv[:, CHUNK:])

    for b in range(bb):
        for m in range(N_HEADS // 2):
            o_ref[b, :, m * LANES:(m + 1) * LANES] = o[b * (N_HEADS // 2) + m]


def _dn_scan(q, k, v, gb, gt, n_lat):
    B, R, _ = q.shape
    nc = R // CHUNK
    nlc = n_lat // CHUNK
    ncc = nc - nlc
    bb = DN_BATCH if B % DN_BATCH == 0 else 1

    def fwd_chunk(i):
        return jnp.where(i < ncc, nlc + i, i - ncc)

    def bwd_chunk(i):
        return jnp.where(i < ncc, nc - 1 - i, nlc - 1 - (i - ncc))

    tokf = lambda w: pl.BlockSpec((bb, CHUNK, w), lambda b, i: (b, fwd_chunk(i), 0))
    tokb = lambda w: pl.BlockSpec((bb, CHUNK, w), lambda b, i: (b, bwd_chunk(i), 0))
    gtf = pl.BlockSpec((bb, 1, 8, CHUNK), lambda b, i: (b, fwd_chunk(i), 0, 0))
    gtb = pl.BlockSpec((bb, 1, 8, CHUNK), lambda b, i: (b, bwd_chunk(i), 0, 0))
    sds = jax.ShapeDtypeStruct
    return pl.pallas_call(
        functools.partial(_dnscan_kernel, bb=bb),
        grid=(B // bb, nc),
        in_specs=[tokf(256), tokf(256), tokf(256), tokf(LANES), gtf,
                  tokb(256), tokb(256), tokb(256), tokb(LANES), gtb],
        out_specs=[tokf(256), tokb(256)],
        out_shape=[sds((B, R, 256), F32), sds((B, R, 256), F32)],
        scratch_shapes=[pltpu.VMEM((N_HEADS // 2 * bb, HEAD_DIM, LANES), F32)] * 2,
        compiler_params=_cparams(("parallel", "arbitrary"), 32),
        name="deltanet_scan",
    )(q, k, v, gb, gt, q, k, v, gb, gt)


def _softmax_pv(s_list, v_list, sink):
    m = functools.reduce(jnp.maximum, [s.max(-1, keepdims=True) for s in s_list])
    if sink is not None:
        m = jnp.maximum(m, sink)
    ov = functools.reduce(lambda a, b: a + b, [jnp.dot(jnp.exp2(s - m).astype(BF16), v, preferred_element_type=F32)
                                               for s, v in zip(s_list, v_list)])
    den = ov[:, HEAD_DIM:HEAD_DIM + 1]
    if sink is not None:
        den = den + jnp.exp2(sink - m)
    return ov[:, :HEAD_DIM] / den


def _scores(q, k):
    return lax.dot_general(q, k, (((1,), (1,)), ((), ())), preferred_element_type=F32)


def _store_heads(o_ref, heads, rows):
    o_ref[0, :rows, :] = jnp.concatenate(heads, -1).astype(BF16)


def _gattn_kernel(q_ref, k_ref, v_ref, o_ref, *, n_lat):
    j = pl.program_id(1)
    tq = q_ref.shape[2]
    n_ctx = k_ref.shape[2] - n_lat
    group = N_HEADS // N_KV

    @pl.when(j == n_lat // tq)
    def _():
        _store_heads(o_ref, [_softmax_pv([_scores(q_ref[0, h, :n_ctx, :], k_ref[0, h // group, n_lat:, :])],
                                         [v_ref[0, h // group, n_lat:, :]], None)
                             for h in range(N_HEADS)], n_ctx)

    @pl.when(j < n_lat // tq)
    def _():
        _store_heads(o_ref, [_softmax_pv([_scores(q_ref[0, h], k_ref[0, h // group])], [v_ref[0, h // group]],
                                         None) for h in range(N_HEADS)], tq)


def _global_attention(q, k, v, n_lat):
    B, _, R, _ = q.shape
    tq = TQ_GLOBAL
    return pl.pallas_call(
        functools.partial(_gattn_kernel, n_lat=n_lat),
        grid=(B, pl.cdiv(R, tq)),
        in_specs=[pl.BlockSpec((1, N_HEADS, tq, HEAD_DIM), lambda b, j: (b, 0, j, 0)),
                  pl.BlockSpec((1, N_KV, R, HEAD_DIM), lambda b, j: (b, 0, 0, 0)),
                  pl.BlockSpec((1, N_KV, R, LANES), lambda b, j: (b, 0, 0, 0))],
        out_specs=pl.BlockSpec((1, tq, N_HEADS * HEAD_DIM), lambda b, j: (b, j, 0)),
        out_shape=jax.ShapeDtypeStruct((B, R, N_HEADS * HEAD_DIM), BF16),
        compiler_params=_cparams(("parallel", "parallel"), 56),
        name="global_attention",
    )(q, k, v)


def _wattn_kernel(sink_ref, bias_ref, q_ref, k_ref, v_ref, o_ref, *, n_lat):
    j = pl.program_id(1)
    tq = q_ref.shape[2]
    n_ctx = k_ref.shape[2] - n_lat
    band = tq + 2 * WINDOW
    group = N_HEADS // N_KV
    sinks = [sink_ref[h] * LOG2E for h in range(N_HEADS)]

    @pl.when(j == n_lat // tq)
    def _():
        _store_heads(o_ref, [_softmax_pv([_scores(q_ref[0, h, :n_ctx, :], k_ref[0, h // group, n_lat:, :])],
                                         [v_ref[0, h // group, n_lat:, :]], sinks[h])
                             for h in range(N_HEADS)], n_ctx)

    @pl.when(j < n_lat // tq)
    def _():
        q0 = j * tq
        start = pl.multiple_of(jnp.clip(q0 - WINDOW, 0, n_lat - band), WINDOW)
        bias = bias_ref[(q0 - start) // WINDOW]
        heads = []
        for h in range(N_HEADS):
            kv = h // group
            q = q_ref[0, h]
            s_loc = _scores(q, k_ref[0, kv, pl.ds(start, band), :]) + bias
            s_ctx = _scores(q, k_ref[0, kv, n_lat:, :])
            heads.append(_softmax_pv([s_loc, s_ctx], [v_ref[0, kv, pl.ds(start, band), :],
                                                      v_ref[0, kv, n_lat:, :]], sinks[h]))
        _store_heads(o_ref, heads, tq)


def _window_bias(tq):
    band = tq + 2 * WINDOW
    off = jnp.arange(3, dtype=jnp.int32)[:, None, None] * WINDOW
    r = jnp.arange(tq, dtype=jnp.int32)[None, :, None]
    c = jnp.arange(band, dtype=jnp.int32)[None, None, :]
    return jnp.where(jnp.abs(r + off - c) <= WINDOW, 0.0, NEG_INF).astype(F32)


def _window_attention(q, k, v, sink, n_lat):
    B, _, R, _ = q.shape
    tq = TQ_WINDOW
    bias = _window_bias(tq)
    return pl.pallas_call(
        functools.partial(_wattn_kernel, n_lat=n_lat),
        grid=(B, pl.cdiv(R, tq)),
        in_specs=[pl.BlockSpec(memory_space=pltpu.SMEM),
                  pl.BlockSpec(bias.shape, lambda b, j: (0, 0, 0), pipeline_mode=pl.Buffered(1)),
                  pl.BlockSpec((1, N_HEADS, tq, HEAD_DIM), lambda b, j: (b, 0, j, 0)),
                  pl.BlockSpec((1, N_KV, R, HEAD_DIM), lambda b, j: (b, 0, 0, 0)),
                  pl.BlockSpec((1, N_KV, R, LANES), lambda b, j: (b, 0, 0, 0))],
        out_specs=pl.BlockSpec((1, tq, N_HEADS * HEAD_DIM), lambda b, j: (b, j, 0)),
        out_shape=jax.ShapeDtypeStruct((B, R, N_HEADS * HEAD_DIM), BF16),
        compiler_params=_cparams(("parallel", "parallel"), 48),
        name="window_attention",
    )(sink, bias, q, k, v)


def _dft_kernel(z_ref, cl_ref, sl_ref, cc_ref, sc_ref, o_ref, *, n_lat):
    j = pl.program_id(1)
    half = z_ref.shape[2] // 2

    def mix(c, s, z):
        return (jnp.dot(c, z[:, :half], preferred_element_type=F32)
                + jnp.dot(s, z[:, half:], preferred_element_type=F32)).astype(BF16)

    n_ctx = z_ref.shape[1] - n_lat

    @pl.when(j == n_lat // TQ_DFT)
    def _():
        for b in range(z_ref.shape[0]):
            o_ref[b, :n_ctx, :] = mix(cc_ref[...], sc_ref[...], z_ref[b, n_lat:, :])

    @pl.when(j < n_lat // TQ_DFT)
    def _():
        for b in range(z_ref.shape[0]):
            o_ref[b] = mix(cl_ref[...], sl_ref[...], z_ref[b, :n_lat, :])


def _position_dft(z, cos_lat, msin_lat, cos_ctx, msin_ctx, n_lat):
    B, R, _ = z.shape
    n_ctx = R - n_lat
    bb = DFT_BATCH if B % DFT_BATCH == 0 else 1
    lat_blocks = n_lat // TQ_DFT
    lat = pl.BlockSpec((TQ_DFT, n_lat), lambda b, j: (jnp.minimum(j, lat_blocks - 1), 0))
    ctx = pl.BlockSpec((n_ctx, n_ctx), lambda b, j: (0, 0))
    return pl.pallas_call(
        functools.partial(_dft_kernel, n_lat=n_lat),
        grid=(B // bb, pl.cdiv(R, TQ_DFT)),
        in_specs=[pl.BlockSpec((bb, R, 512), lambda b, j: (b, 0, 0), pipeline_mode=pl.Buffered(1)),
                  lat, lat, ctx, ctx],
        out_specs=pl.BlockSpec((bb, TQ_DFT, 256), lambda b, j: (b, j, 0)),
        out_shape=jax.ShapeDtypeStruct((B, R, 256), BF16),
        compiler_params=_cparams(("parallel", "parallel"), 48),
        name="position_dft",
    )(z, cos_lat, msin_lat, cos_ctx, msin_ctx)


def _block_tail_kernel(of_ref, ob_ref, z_ref, ga_ref, wa_ref, ft_ref, x_ref, mod_ref, ng_ref, ones_ref, w_ref,
                       g2_ref, wg_ref, wu_ref, wd_ref, o_ref, *, lat_blocks, n_ctx):
    is_ctx = pl.program_id(1) == lat_blocks
    args = (of_ref, ob_ref, z_ref, ga_ref, wa_ref, ft_ref, x_ref, mod_ref, ng_ref, ones_ref, w_ref,
            g2_ref, wg_ref, wu_ref, wd_ref, o_ref)

    @pl.when(is_ctx)
    def _():
        _block_tail_rows(*args, rows=n_ctx)

    @pl.when(jnp.logical_not(is_ctx))
    def _():
        _block_tail_rows(*args, rows=o_ref.shape[1])


def _block_tail_rows(of_ref, ob_ref, z_ref, ga_ref, wa_ref, ft_ref, x_ref, mod_ref, ng_ref, ones_ref, w_ref,
                     g2_ref, wg_ref, wu_ref, wd_ref, o_ref, *, rows):
    o = of_ref[0, :rows] + ob_ref[0, :rows]
    z = z_ref[0, :rows]
    ones_bd = ones_ref[...]
    acc = None
    for s in range(2):
        t = o[:, s * LANES:(s + 1) * LANES]
        y = t * lax.rsqrt(_head_sumsq(t, ones_bd) * (1.0 / HEAD_DIM) + EPS) * ng_ref[...]
        y = (y * _silu(z[:, s * LANES:(s + 1) * LANES])).astype(BF16)
        part = jnp.dot(y, w_ref[s * LANES:(s + 1) * LANES, :], preferred_element_type=F32)
        acc = part if acc is None else acc + part
    for n, ref in enumerate((ga_ref, wa_ref, ft_ref)):
        acc = acc + jnp.dot(ref[0, :rows], w_ref[256 * (n + 1):256 * (n + 2), :], preferred_element_type=F32)
    x = x_ref[0, :rows] + mod_ref[0, 2:3, :] * acc

    ms = jnp.mean(x * x, -1, keepdims=True)
    y = x * lax.rsqrt(ms + EPS) * g2_ref[...]
    h = (y * (1.0 + mod_ref[0, 4:5, :]) + mod_ref[0, 3:4, :]).astype(BF16)
    acc = None
    for c in range(FFN_HIDDEN // FFN_CHUNK):
        cs = slice(c * FFN_CHUNK, (c + 1) * FFN_CHUNK)
        g = jnp.dot(h, wg_ref[:, cs], preferred_element_type=F32)
        u = jnp.dot(h, wu_ref[:, cs], preferred_element_type=F32)
        a = (_silu(g) * u).astype(BF16)
        part = jnp.dot(a, wd_ref[cs, :], preferred_element_type=F32)
        acc = part if acc is None else acc + part
    o_ref[0, :rows] = x + mod_ref[0, 5:6, :] * acc


def _layer_weight_spec(layer, shape):
    return pl.BlockSpec((None,) + shape, lambda b, j: (layer, 0, 0), pipeline_mode=pl.Buffered(1))


def _block_tail(of, ob, z, y_ga, y_wa, y_ft, xa, mod_l, ng, ones_bd, w_out, g2, wg, wu, wd, layer, n_lat,
                rows_out):
    B = xa.shape[0]
    const2 = lambda b, j: (0, 0)
    tok = lambda w: pl.BlockSpec((1, TM_TAIL, w), lambda b, j: (b, j, 0))
    resident = functools.partial(_layer_weight_spec, layer)
    return pl.pallas_call(
        functools.partial(_block_tail_kernel, lat_blocks=n_lat // TM_TAIL, n_ctx=of.shape[1] - n_lat),
        grid=(B, pl.cdiv(rows_out, TM_TAIL)),
        in_specs=[tok(256)] * 6 + [
            tok(D_MODEL),
            _mod_spec(n_lat // TM_TAIL),
            pl.BlockSpec((1, LANES), const2),
            pl.BlockSpec((LANES, LANES), const2),
            resident((D_MODEL, D_MODEL)),
            pl.BlockSpec((1, D_MODEL), const2),
            resident((D_MODEL, FFN_HIDDEN)),
            resident((D_MODEL, FFN_HIDDEN)),
            resident((FFN_HIDDEN, D_MODEL))],
        out_specs=tok(D_MODEL),
        out_shape=jax.ShapeDtypeStruct((B, rows_out, D_MODEL), F32),
        compiler_params=_cparams(("parallel", "parallel"), 52),
        name="out_projection_ffn",
    )(of, ob, z, y_ga, y_wa, y_ft, xa, mod_l, ng, ones_bd, w_out, g2, wg, wu, wd)


def _rope_tables(n_lat, n_rows):
    t = jnp.arange(n_lat, dtype=jnp.int32)
    row = (t // GRID_W).astype(F32)
    col = (t % GRID_W).astype(F32)
    n_freq = HEAD_DIM // 4
    inv_freq = ROPE_THETA ** (-jnp.arange(n_freq, dtype=F32) / n_freq)
    ang = jnp.concatenate([row[:, None] * inv_freq, col[:, None] * inv_freq], -1)
    cos = jnp.tile(jnp.cos(ang), (1, 4))
    sin = jnp.sin(ang)
    sin = jnp.concatenate([-sin, sin, -sin, sin], -1)
    cos = jnp.concatenate([cos, jnp.ones((n_rows - n_lat, LANES), F32)], 0)
    sin = jnp.concatenate([sin, jnp.zeros((n_rows - n_lat, LANES), F32)], 0)
    return cos, sin


def _dft_matrices(n, n_chan):
    scale = 1.0 / math.sqrt(n * n_chan)
    p = 64 if n % 64 == 0 and n > 64 else 1
    q = n // p
    col = jnp.arange(n, dtype=jnp.int32)[None, :]

    def angles(rows, period):
        return ((jnp.arange(rows, dtype=jnp.int32)[:, None] * col) % period).astype(F32) * (2.0 * math.pi / period)

    ang_a = angles(p, p)
    ang_b = angles(q, n)
    ca, sa = jnp.cos(ang_a)[:, None, :], jnp.sin(ang_a)[:, None, :]
    cb, sb = jnp.cos(ang_b)[None, :, :] * scale, jnp.sin(ang_b)[None, :, :] * scale
    cos = (ca * cb - sa * sb).reshape(n, n)
    msin = (-(sa * cb + ca * sb)).reshape(n, n)
    return cos.astype(BF16), msin.astype(BF16)


def _channel_dft_weights():
    n = HEAD_DIM
    idx = np.arange(n)
    ang = 2.0 * np.pi * ((idx[:, None] * idx[None, :]) % n) / n
    eye = np.eye(4)
    w = np.concatenate([np.kron(eye, np.cos(ang)), np.kron(eye, np.sin(ang))], 1)
    return jnp.asarray(w, F32).astype(BF16)


def kernel(x, c, ctx, c_ctx, norm1_g, norm2_g, w_ada, b_ada, w_in, dn_conv_w, dn_A_log, dn_dt_bias, dn_norm_g,
           ga_q_norm, ga_k_norm, wa_q_norm, wa_k_norm, wa_sink, w_out, w_ffn_gate, w_ffn_up, w_ffn_down):
    B, S, _ = x.shape
    L = ctx.shape[1]
    depth = w_in.shape[0]
    R = S + L
    assert S % TM == 0 and S % TM_TAIL == 0 and L <= TM_TAIL and B <= MOD_ROWS // 2
    assert all(S % t == 0 and L <= t for t in (TQ_WINDOW, TQ_DFT, TQ_GLOBAL))

    cc = jnp.zeros((MOD_ROWS, D_MODEL), F32).at[:B].set(c).at[MOD_ROWS // 2].set(c_ctx)
    mod = _modulation(cc, w_ada, b_ada).reshape(depth, MOD_ROWS, N_MOD, D_MODEL)

    cos_t, sin_t = _rope_tables(S, pl.cdiv(R, TM) * TM)
    cos_lat, msin_lat = _dft_matrices(S, HEAD_DIM)
    cos_ctx, msin_ctx = _dft_matrices(L, HEAD_DIM)
    ftw = _channel_dft_weights()
    ones_bd = jnp.asarray(np.kron(np.eye(2), np.ones((HEAD_DIM, HEAD_DIM))), F32).astype(BF16)
    tile2 = lambda g: jnp.tile(g, 2)

    w_in_r = jnp.concatenate([w_in[:, :, :1024], w_in[:, :, 1040:2320], w_in[:, :, 1024:1040],
                              jnp.zeros((depth, D_MODEL, C_END - 2320), F32)], 2).astype(BF16)
    w_out_b, w_gate_b, w_up_b, w_down_b = [t.astype(BF16) for t in (w_out, w_ffn_gate, w_ffn_up, w_ffn_down)]

    xa = jnp.concatenate([x, ctx], 1)
    for l in range(depth):
        last = l == depth - 1
        gains = jnp.zeros((8, LANES), F32)
        for n, g in enumerate((ga_q_norm[l], ga_k_norm[l], wa_q_norm[l], wa_k_norm[l])):
            gains = gains.at[n].set(tile2(g))
        conv_w8 = jnp.zeros((8, 768), F32).at[:3].set(dn_conv_w[l])
        neg_a = jnp.zeros((1, LANES), F32).at[0, :8].set(-jnp.exp(dn_A_log[l].astype(F32)).reshape(-1))
        dt_b = jnp.zeros((1, LANES), F32).at[0, :8].set(dn_dt_bias[l].astype(F32).reshape(-1))
        (dq, dk, dv, gb, z, gaq, gak, gav, waq, wak, wav, ftz) = _in_projection(
            xa, mod[l], norm1_g[l][None, :], w_in_r, cos_t, sin_t, gains, ones_bd, ftw, conv_w8, neg_a, dt_b, l, S)
        gt = gb[:, :, :8].reshape(B, R // CHUNK, CHUNK, 8).transpose(0, 1, 3, 2)
        o_f, o_b = _dn_scan(dq, dk, dv, gb, gt, S)

        y_ga = _global_attention(gaq, gak, gav, S)
        y_wa = _window_attention(waq, wak, wav, wa_sink[l].astype(F32), S)
        y_ft = _position_dft(ftz, cos_lat, msin_lat, cos_ctx, msin_ctx, S)

        rows_out = S if last else R
        xa = _block_tail(o_f, o_b, z, y_ga, y_wa, y_ft, xa, mod[l], tile2(dn_norm_g[l])[None, :], ones_bd,
                         w_out_b, norm2_g[l][None, :], w_gate_b, w_up_b, w_down_b, l, S, rows_out)
    return xa
```

```python
import functools
import math

import jax
import jax.numpy as jnp
import numpy as np
from jax import lax
from jax.experimental import pallas as pl
from jax.experimental.pallas import tpu as pltpu

F32 = jnp.float32
BF16 = jnp.bfloat16
HIGHEST = lax.Precision.HIGHEST

D_MODEL = 1024
HEAD_DIM = 64
N_HEADS = 4
N_KV = 2
GRID_W = 64
CHUNK = 64
WINDOW = 128
ROPE_THETA = 10000.0
FFN_HIDDEN = 2816
N_MOD = 6
EPS = 1e-6
NEG_INF = -1e30

LANES = 128
TM = 512
TM_TAIL = 512
TQ_GLOBAL = 256
TQ_WINDOW = 512
TQ_DFT = 512
DFT_BATCH = 2
MOD_ROWS = 16
FFN_CHUNK = 256
DN_BATCH = 8
LOG2E = math.log2(math.e)

C_QKV, C_Z, C_GA, C_WA, C_FT, C_AB, C_END = 0, 768, 1024, 1536, 2048, 2304, 2432


def _cparams(sem, vmem_mb):
    return pltpu.CompilerParams(dimension_semantics=sem, vmem_limit_bytes=vmem_mb * 1024 * 1024)


def _silu(x):
    return x * (1.0 / (1.0 + jnp.exp(-x)))


def _head_sumsq(t, ones_bd):
    return jnp.dot((t * t).astype(BF16), ones_bd, preferred_element_type=F32)


def _mod_spec(n_lat_blocks):
    ctx_row = MOD_ROWS // 2
    return pl.BlockSpec((1, N_MOD, D_MODEL), lambda b, j: (jnp.where(j == n_lat_blocks, ctx_row, b), 0, 0))


def _mod_kernel(c_ref, w_ref, b_ref, o_ref):
    a = _silu(c_ref[...])
    o_ref[0] = jnp.dot(a, w_ref[0], preferred_element_type=F32, precision=HIGHEST) + b_ref[0]


def _modulation(cc, w_ada, b_ada):
    depth = w_ada.shape[0]
    return pl.pallas_call(
        _mod_kernel,
        grid=(depth, N_MOD),
        in_specs=[pl.BlockSpec((MOD_ROWS, D_MODEL), lambda l, n: (0, 0)),
                  pl.BlockSpec((1, D_MODEL, D_MODEL), lambda l, n: (l, 0, n)),
                  pl.BlockSpec((1, 1, D_MODEL), lambda l, n: (l, 0, n))],
        out_specs=pl.BlockSpec((1, MOD_ROWS, D_MODEL), lambda l, n: (l, 0, n)),
        out_shape=jax.ShapeDtypeStruct((depth, MOD_ROWS, N_MOD * D_MODEL), F32),
        compiler_params=_cparams(("parallel", "parallel"), 32),
        name="modulation",
    )(cc, w_ada, b_ada.reshape(depth, 1, N_MOD * D_MODEL))


def _inproj_kernel(x_ref, xp_ref, xn_ref, mod_ref, g1_ref, w_ref, cos_ref, sin_ref, gains_ref, ones_ref, ftw_ref,
                   cw_ref, na_ref, dtb_ref,
                   dq_ref, dk_ref, dv_ref, gb_ref, z_ref, gaq_ref, gak_ref, gav_ref, waq_ref, wak_ref, wav_ref,
                   ft_ref, *, lat_blocks, n_ctx):
    j = pl.program_id(1)
    tm = x_ref.shape[1]
    x = jnp.concatenate([x_ref[0], xp_ref[0], xn_ref[0]], 0)
    ms = jnp.mean(x * x, -1, keepdims=True)
    y = x * lax.rsqrt(ms + EPS) * g1_ref[...]
    h_all = (y * (1.0 + mod_ref[0, 1:2, :]) + mod_ref[0, 0:1, :]).astype(BF16)
    h = h_all[:tm]

    def proj(c0, c1):
        return jnp.dot(h, w_ref[:, c0:c1], preferred_element_type=F32)

    z_ref[0] = proj(C_Z, C_GA)
    ones_bd = ones_ref[...]

    qkv = jnp.dot(h_all, w_ref[:, C_QKV:C_Z], preferred_element_type=F32)
    cur = qkv[:tm]
    row = lax.broadcasted_iota(jnp.int32, cur.shape, 0)
    is_ctx = j == lat_blocks
    has_prev = jnp.where((j == 0) | is_ctx, 0.0, 1.0)
    has_next = jnp.where(j >= lat_blocks - 1, 0.0, 1.0)
    last_row = jnp.where(is_ctx, n_ctx - 1, tm - 1)
    prev_row = qkv[tm + 7:tm + 8] * has_prev
    next_row = qkv[tm + 8:tm + 9] * has_next
    up = jnp.where(row == 0, prev_row, pltpu.roll(cur, 1, 0))
    dn = jnp.where(row == last_row, next_row, pltpu.roll(cur, tm - 1, 0))
    a = _silu(up * cw_ref[0:1, :] + cur * cw_ref[1:2, :] + dn * cw_ref[2:3, :])
    for s in range(2):
        lanes = slice(s * LANES, (s + 1) * LANES)
        tq = a[:, lanes]
        dq_ref[0, :, lanes] = tq * lax.rsqrt(_head_sumsq(tq, ones_bd) + EPS) * HEAD_DIM ** -0.5
        tk = a[:, 256 + s * LANES:256 + (s + 1) * LANES]
        dk_ref[0, :, lanes] = tk * lax.rsqrt(_head_sumsq(tk, ones_bd) + EPS)
    dv_ref[0] = a[:, 512:768]

    ab = proj(C_AB, C_END)
    t = ab + dtb_ref[...]
    softplus = jnp.maximum(t, 0.0) + jnp.log(1.0 + jnp.exp(-jnp.abs(t)))
    g = na_ref[...] * softplus
    beta = 1.0 / (1.0 + jnp.exp(-ab))
    lane_c = lax.broadcasted_iota(jnp.int32, (CHUNK, LANES), 1)
    ii = lax.broadcasted_iota(jnp.int32, (CHUNK, CHUNK), 0)
    jj = lax.broadcasted_iota(jnp.int32, (CHUNK, CHUNK), 1)
    prefix = jnp.where(ii >= jj, 1.0, 0.0)
    suffix = jnp.where(ii <= jj, 1.0, 0.0)
    for c in range(tm // CHUNK):
        rows = slice(c * CHUNK, (c + 1) * CHUNK)
        gc = g[rows]
        fwd = jnp.dot(prefix, gc, preferred_element_type=F32, precision=HIGHEST)
        bwd = jnp.dot(suffix, gc, preferred_element_type=F32, precision=HIGHEST)
        gb_ref[0, rows, :] = jnp.where(lane_c < N_HEADS, fwd, jnp.where(lane_c < 2 * N_HEADS, bwd, beta[rows]))

    cos_t = cos_ref[...]
    sin_t = sin_ref[...]
    lane = lax.broadcasted_iota(jnp.int32, cos_t.shape, 1)
    first_half = (lane & (HEAD_DIM // 2)) == 0

    def norm_rope(t, gain_row, scale):
        yv = t * lax.rsqrt(_head_sumsq(t, ones_bd) * (1.0 / HEAD_DIM) + EPS) * gain_row
        partner = jnp.where(first_half, pltpu.roll(yv, LANES - HEAD_DIM // 2, 1), pltpu.roll(yv, HEAD_DIM // 2, 1))
        return (yv * cos_t + partner * sin_t) * scale

    def heads_out(ref, t, n):
        for i in range(n):
            ref[0, i] = t[:, i * HEAD_DIM:(i + 1) * HEAD_DIM].astype(BF16)

    for c0, q_ref, k_ref, v_ref, gq, gk in ((C_GA, gaq_ref, gak_ref, gav_ref, 0, 1),
                                            (C_WA, waq_ref, wak_ref, wav_ref, 2, 3)):
        p = proj(c0, c0 + 512)
        scale = HEAD_DIM ** -0.5 * LOG2E
        q = jnp.concatenate([norm_rope(p[:, 0:128], gains_ref[gq:gq + 1, :], scale),
                             norm_rope(p[:, 128:256], gains_ref[gq:gq + 1, :], scale)], -1)
        k = norm_rope(p[:, 256:384], gains_ref[gk:gk + 1, :], 1.0)
        heads_out(q_ref, q, N_HEADS)
        if c0 == C_GA:
            k_ref[0] = k.T.reshape(N_KV, HEAD_DIM, k.shape[0]).astype(BF16)
        else:
            heads_out(k_ref, k, N_KV)
        for i in range(N_KV):
            vh = p[:, 384 + i * HEAD_DIM:384 + (i + 1) * HEAD_DIM]
            v_ref[0, i] = jnp.where(lane < HEAD_DIM, jnp.concatenate([vh, vh], -1),
                                    jnp.where(lane == HEAD_DIM, 1.0, 0.0)).astype(BF16)

    pf = proj(C_FT, C_AB).astype(BF16)
    ft_ref[0] = jnp.dot(pf, ftw_ref[...], preferred_element_type=F32).astype(BF16)


def _in_projection(xa, mod_l, g1, w_in_r, cos_t, sin_t, gains, ones_bd, ftw, conv_w8, neg_a, dt_b, layer, n_lat):
    B, R, _ = xa.shape
    nb = pl.cdiv(R, TM)
    hb = TM // 8
    const2 = lambda b, j: (0, 0)
    tok = lambda w: pl.BlockSpec((1, TM, w), lambda b, j: (b, j, 0))
    hd = lambda n, w=HEAD_DIM: pl.BlockSpec((1, n, TM, w), lambda b, j: (b, 0, j, 0))
    sds = jax.ShapeDtypeStruct
    heads = lambda n, w=HEAD_DIM: sds((B, n, R, w), BF16)
    return pl.pallas_call(
        functools.partial(_inproj_kernel, lat_blocks=n_lat // TM, n_ctx=R - n_lat),
        grid=(B, nb),
        in_specs=[tok(D_MODEL),
                  pl.BlockSpec((1, 8, D_MODEL), lambda b, j: (b, jnp.maximum(j * hb - 1, 0), 0)),
                  pl.BlockSpec((1, 8, D_MODEL), lambda b, j: (b, jnp.minimum((j + 1) * hb, R // 8 - 1), 0)),
                  _mod_spec(n_lat // TM),
                  pl.BlockSpec((1, D_MODEL), const2),
                  _layer_weight_spec(layer, (D_MODEL, C_END)),
                  pl.BlockSpec((TM, LANES), lambda b, j: (j, 0)),
                  pl.BlockSpec((TM, LANES), lambda b, j: (j, 0)),
                  pl.BlockSpec((8, LANES), const2),
                  pl.BlockSpec((LANES, LANES), const2),
                  pl.BlockSpec((256, 512), const2),
                  pl.BlockSpec((8, 768), const2),
                  pl.BlockSpec((1, LANES), const2),
                  pl.BlockSpec((1, LANES), const2)],
        out_specs=[tok(256), tok(256), tok(256), tok(LANES), tok(256),
                   hd(N_HEADS), pl.BlockSpec((1, N_KV, HEAD_DIM, TM), lambda b, j: (b, 0, 0, j)), hd(N_KV, LANES),
                   hd(N_HEADS), hd(N_KV), hd(N_KV, LANES),
                   tok(512)],
        out_shape=[sds((B, R, 256), F32)] * 3 + [sds((B, R, LANES), F32), sds((B, R, 256), F32),
                   heads(N_HEADS), sds((B, N_KV, HEAD_DIM, R), BF16), heads(N_KV, LANES),
                   heads(N_HEADS), heads(N_KV), heads(N_KV, LANES),
                   sds((B, R, 512), BF16)],
        compiler_params=_cparams(("parallel", "parallel"), 48),
        name="in_projection",
    )(xa, xa, xa, mod_l, g1, w_in_r, cos_t, sin_t, gains, ones_bd, ftw, conv_w8, neg_a, dt_b)


def _bdot(a, b, dims):
    return lax.dot_general(a.astype(BF16), b.astype(BF16), dims, preferred_element_type=F32)


_NN = (((2,), (1,)), ((0,), (0,)))
_NT = (((2,), (2,)), ((0,), (0,)))
_TN = (((1,), (1,)), ((0,), (0,)))


def _pair_blockdiag(x):
    first = (lax.broadcasted_iota(jnp.int32, x.shape, 2) & HEAD_DIM) == 0
    return jnp.concatenate([jnp.where(first, x, 0.0), jnp.where(first, 0.0, x)], 1)


def _dnscan_kernel(qf_ref, kf_ref, vf_ref, gbf_ref, gtf_ref, qb_ref, kb_ref, vb_ref, gbb_ref, gtb_ref,
                   of_ref, ob_ref, sf_ref, sb_ref, *, bb):
    i = pl.program_id(1)

    @pl.when(i == 0)
    def _():
        sf_ref[...] = jnp.zeros_like(sf_ref)
        sb_ref[...] = jnp.zeros_like(sb_ref)

    _dn_direction(qf_ref, kf_ref, vf_ref, gbf_ref, gtf_ref, of_ref, sf_ref, bb, 0)
    _dn_direction(qb_ref, kb_ref, vb_ref, gbb_ref, gtb_ref, ob_ref, sb_ref, bb, 1)


def _dn_direction(q_ref, k_ref, v_ref, gb_ref, gt_ref, o_ref, state_ref, bb, d):
    lane2 = lax.broadcasted_iota(jnp.int32, (CHUNK, LANES), 1)
    first2 = (lane2 & HEAD_DIM) == 0
    last = CHUNK - 1 if d == 0 else 0
    gc_cols, b_cols, gc_rows, g_lasts, qs, ks, vs = [], [], [], [], [], [], []
    for b in range(bb):
        gb = gb_ref[b]
        gt = gt_ref[b, 0]
        for m in range(N_HEADS // 2):
            c = d * N_HEADS + 2 * m
            lanes = slice(m * LANES, (m + 1) * LANES)
            qs.append(q_ref[b, :, lanes])
            ks.append(k_ref[b, :, lanes])
            vs.append(v_ref[b, :, lanes])
            col = jnp.where(first2, gb[:, c:c + 1], gb[:, c + 1:c + 2])
            gc_cols.append(col)
            g_lasts.append(col[last:last + 1, :])
            b_cols.append(jnp.where(first2, gb[:, 2 * N_HEADS + c:2 * N_HEADS + c + 1],
                                    gb[:, 2 * N_HEADS + c + 1:2 * N_HEADS + c + 2]))
            gc_rows.append(jnp.concatenate([gt[c:c + 1, :], gt[c + 1:c + 2, :]], 1))
    q = jnp.stack(qs)
    k = jnp.stack(ks)
    v = jnp.stack(vs)
    gc_col = jnp.stack(gc_cols)
    b_col = jnp.stack(b_cols)
    gc_row = jnp.stack(gc_rows)
    g_last = jnp.stack(g_lasts)
    shape = q.shape

    ii = lax.broadcasted_iota(jnp.int32, (CHUNK, LANES), 0)
    jj = lane2 & (HEAD_DIM - 1)
    incl = (ii >= jj) if d == 0 else (ii <= jj)
    strict = (ii > jj) if d == 0 else (ii < jj)
    eye = jnp.where(ii == jj, 1.0, 0.0)
    xor = ii ^ jj
    blocks = [(xor >> sh) == 1 for sh in range(6)]
    bd = _pair_blockdiag

    decay = jnp.exp(jnp.where(incl, gc_col - gc_row, NEG_INF))
    e_gc = jnp.exp(gc_col)
    kb = k * b_col
    kq = _bdot(jnp.concatenate([kb, q], 1), bd(k), _NT)
    a_mat = jnp.where(strict, kq[:, :CHUNK] * decay, 0.0)
    intra = jnp.where(incl, kq[:, CHUNK:] * decay, 0.0)
    t = eye - jnp.where(blocks[0], a_mat, 0.0)
    for mask in blocks[1:-1]:
        t = t - _bdot(t, bd(_bdot(jnp.where(mask, a_mat, 0.0), bd(t), _NN)), _NN)
    rhs = jnp.concatenate([v * b_col, kb * e_gc], 2)
    half = _bdot(t, bd(rhs), _NN)
    sol = half - _bdot(t, bd(_bdot(jnp.where(blocks[-1], a_mat, 0.0), bd(half), _NN)), _NN)
    u, w = sol[:, :, :LANES], sol[:, :, LANES:]
    state = state_ref[...]
    wq = _bdot(jnp.concatenate([w, q * e_gc], 1), bd(state), _NN)
    v_new = u - wq[:, :CHUNK]
    o = wq[:, CHUNK:] + _bdot(intra, bd(v_new), _NN)
    k_dec = k * jnp.exp(g_last - gc_col)
    kv = _bdot(bd(k_dec), bd(v_new), _TN)
    first3 = (lax.broadcasted_iota(jnp.int32, shape, 2) & HEAD_DIM) == 0
    state_ref[...] = state * jnp.exp(g_last) + jnp.where(first3, kv[:, :CHUNK], kv[:, CHUNK:])

    for b in range(bb):
        for m in range(N_HEADS // 2):
            o_ref[b, :, m * LANES:(m + 1) * LANES] = o[b * (N_HEADS // 2) + m]


def _dn_scan(q, k, v, gb, gt, n_lat):
    B, R, _ = q.shape
    nc = R // CHUNK
    nlc = n_lat // CHUNK
    ncc = nc - nlc
    bb = DN_BATCH if B % DN_BATCH == 0 else 1

    def fwd_chunk(i):
        return jnp.where(i < ncc, nlc + i, i - ncc)

    def bwd_chunk(i):
        return jnp.where(i < ncc, nc - 1 - i, nlc - 1 - (i - ncc))

    tokf = lambda w: pl.BlockSpec((bb, CHUNK, w), lambda b, i: (b, fwd_chunk(i), 0))
    tokb = lambda w: pl.BlockSpec((bb, CHUNK, w), lambda b, i: (b, bwd_chunk(i), 0))
    gtf = pl.BlockSpec((bb, 1, 8, CHUNK), lambda b, i: (b, fwd_chunk(i), 0, 0))
    gtb = pl.BlockSpec((bb, 1, 8, CHUNK), lambda b, i: (b, bwd_chunk(i), 0, 0))
    sds = jax.ShapeDtypeStruct
    return pl.pallas_call(
        functools.partial(_dnscan_kernel, bb=bb),
        grid=(B // bb, nc),
        in_specs=[tokf(256), tokf(256), tokf(256), tokf(LANES), gtf,
                  tokb(256), tokb(256), tokb(256), tokb(LANES), gtb],
        out_specs=[tokf(256), tokb(256)],
        out_shape=[sds((B, R, 256), F32), sds((B, R, 256), F32)],
        scratch_shapes=[pltpu.VMEM((N_HEADS // 2 * bb, HEAD_DIM, LANES), F32)] * 2,
        compiler_params=_cparams(("parallel", "arbitrary"), 32),
        name="deltanet_scan",
    )(q, k, v, gb, gt, q, k, v, gb, gt)


def _softmax_pv(s_list, v_list, sink):
    m = functools.reduce(jnp.maximum, [s.max(-1, keepdims=True) for s in s_list])
    if sink is not None:
        m = jnp.maximum(m, sink)
    ov = functools.reduce(lambda a, b: a + b, [jnp.dot(jnp.exp2(s - m).astype(BF16), v, preferred_element_type=F32)
                                               for s, v in zip(s_list, v_list)])
    den = ov[:, HEAD_DIM:HEAD_DIM + 1]
    if sink is not None:
        den = den + jnp.exp2(sink - m)
    return ov[:, :HEAD_DIM] / den


def _scores(q, k):
    return lax.dot_general(q, k, (((1,), (1,)), ((), ())), preferred_element_type=F32)


def _store_heads(o_ref, heads, rows):
    o_ref[0, :rows, :] = jnp.concatenate(heads, -1).astype(BF16)


def _gattn_kernel(q_ref, kt_ref, v_ref, o_ref, *, n_lat):
    j = pl.program_id(1)
    tq = q_ref.shape[2]
    n_ctx = kt_ref.shape[3] - n_lat
    group = N_HEADS // N_KV

    def scores_t(q, kt):
        return jnp.dot(q, kt, preferred_element_type=F32)

    @pl.when(j == n_lat // tq)
    def _():
        _store_heads(o_ref, [_softmax_pv([scores_t(q_ref[0, h, :n_ctx, :], kt_ref[0, h // group, :, n_lat:])],
                                         [v_ref[0, h // group, n_lat:, :]], None)
                             for h in range(N_HEADS)], n_ctx)

    @pl.when(j < n_lat // tq)
    def _():
        _store_heads(o_ref, [_softmax_pv([scores_t(q_ref[0, h], kt_ref[0, h // group])], [v_ref[0, h // group]],
                                         None) for h in range(N_HEADS)], tq)


def _global_attention(q, k, v, n_lat):
    B, _, R, _ = q.shape
    tq = TQ_GLOBAL
    return pl.pallas_call(
        functools.partial(_gattn_kernel, n_lat=n_lat),
        grid=(B, pl.cdiv(R, tq)),
        in_specs=[pl.BlockSpec((1, N_HEADS, tq, HEAD_DIM), lambda b, j: (b, 0, j, 0)),
                  pl.BlockSpec((1, N_KV, HEAD_DIM, R), lambda b, j: (b, 0, 0, 0)),
                  pl.BlockSpec((1, N_KV, R, LANES), lambda b, j: (b, 0, 0, 0))],
        out_specs=pl.BlockSpec((1, tq, N_HEADS * HEAD_DIM), lambda b, j: (b, j, 0)),
        out_shape=jax.ShapeDtypeStruct((B, R, N_HEADS * HEAD_DIM), BF16),
        compiler_params=_cparams(("parallel", "parallel"), 56),
        name="global_attention",
    )(q, k, v)


def _wattn_kernel(sink_ref, bias_ref, q_ref, k_ref, v_ref, o_ref, *, n_lat):
    j = pl.program_id(1)
    tq = q_ref.shape[2]
    n_ctx = k_ref.shape[2] - n_lat
    band = tq + 2 * WINDOW
    group = N_HEADS // N_KV
    sinks = [sink_ref[h] * LOG2E for h in range(N_HEADS)]

    @pl.when(j == n_lat // tq)
    def _():
        _store_heads(o_ref, [_softmax_pv([_scores(q_ref[0, h, :n_ctx, :], k_ref[0, h // group, n_lat:, :])],
                                         [v_ref[0, h // group, n_lat:, :]], sinks[h])
                             for h in range(N_HEADS)], n_ctx)

    @pl.when(j < n_lat // tq)
    def _():
        q0 = j * tq
        start = pl.multiple_of(jnp.clip(q0 - WINDOW, 0, n_lat - band), WINDOW)
        bias = bias_ref[(q0 - start) // WINDOW]
        heads = []
        for h in range(N_HEADS):
            kv = h // group
            q = q_ref[0, h]
            s_loc = _scores(q, k_ref[0, kv, pl.ds(start, band), :]) + bias
            s_ctx = _scores(q, k_ref[0, kv, n_lat:, :])
            heads.append(_softmax_pv([s_loc, s_ctx], [v_ref[0, kv, pl.ds(start, band), :],
                                                      v_ref[0, kv, n_lat:, :]], sinks[h]))
        _store_heads(o_ref, heads, tq)


def _window_bias(tq):
    band = tq + 2 * WINDOW
    off = jnp.arange(3, dtype=jnp.int32)[:, None, None] * WINDOW
    r = jnp.arange(tq, dtype=jnp.int32)[None, :, None]
    c = jnp.arange(band, dtype=jnp.int32)[None, None, :]
    return jnp.where(jnp.abs(r + off - c) <= WINDOW, 0.0, NEG_INF).astype(F32)


def _window_attention(q, k, v, sink, n_lat):
    B, _, R, _ = q.shape
    tq = TQ_WINDOW
    bias = _window_bias(tq)
    return pl.pallas_call(
        functools.partial(_wattn_kernel, n_lat=n_lat),
        grid=(B, pl.cdiv(R, tq)),
        in_specs=[pl.BlockSpec(memory_space=pltpu.SMEM),
                  pl.BlockSpec(bias.shape, lambda b, j: (0, 0, 0), pipeline_mode=pl.Buffered(1)),
                  pl.BlockSpec((1, N_HEADS, tq, HEAD_DIM), lambda b, j: (b, 0, j, 0)),
                  pl.BlockSpec((1, N_KV, R, HEAD_DIM), lambda b, j: (b, 0, 0, 0)),
                  pl.BlockSpec((1, N_KV, R, LANES), lambda b, j: (b, 0, 0, 0))],
        out_specs=pl.BlockSpec((1, tq, N_HEADS * HEAD_DIM), lambda b, j: (b, j, 0)),
        out_shape=jax.ShapeDtypeStruct((B, R, N_HEADS * HEAD_DIM), BF16),
        compiler_params=_cparams(("parallel", "parallel"), 48),
        name="window_attention",
    )(sink, bias, q, k, v)


def _dft_kernel(z_ref, cl_ref, sl_ref, cc_ref, sc_ref, o_ref, *, n_lat):
    j = pl.program_id(1)
    half = z_ref.shape[2] // 2

    def mix(c, s, z):
        return (jnp.dot(c, z[:, :half], preferred_element_type=F32)
                + jnp.dot(s, z[:, half:], preferred_element_type=F32)).astype(BF16)

    n_ctx = z_ref.shape[1] - n_lat

    @pl.when(j == n_lat // TQ_DFT)
    def _():
        for b in range(z_ref.shape[0]):
            o_ref[b, :n_ctx, :] = mix(cc_ref[...], sc_ref[...], z_ref[b, n_lat:, :])

    @pl.when(j < n_lat // TQ_DFT)
    def _():
        for b in range(z_ref.shape[0]):
            o_ref[b] = mix(cl_ref[...], sl_ref[...], z_ref[b, :n_lat, :])


def _position_dft(z, cos_lat, msin_lat, cos_ctx, msin_ctx, n_lat):
    B, R, _ = z.shape
    n_ctx = R - n_lat
    bb = DFT_BATCH if B % DFT_BATCH == 0 else 1
    lat_blocks = n_lat // TQ_DFT
    lat = pl.BlockSpec((TQ_DFT, n_lat), lambda b, j: (jnp.minimum(j, lat_blocks - 1), 0))
    ctx = pl.BlockSpec((n_ctx, n_ctx), lambda b, j: (0, 0))
    return pl.pallas_call(
        functools.partial(_dft_kernel, n_lat=n_lat),
        grid=(B // bb, pl.cdiv(R, TQ_DFT)),
        in_specs=[pl.BlockSpec((bb, R, 512), lambda b, j: (b, 0, 0), pipeline_mode=pl.Buffered(1)),
                  lat, lat, ctx, ctx],
        out_specs=pl.BlockSpec((bb, TQ_DFT, 256), lambda b, j: (b, j, 0)),
        out_shape=jax.ShapeDtypeStruct((B, R, 256), BF16),
        compiler_params=_cparams(("parallel", "parallel"), 48),
        name="position_dft",
    )(z, cos_lat, msin_lat, cos_ctx, msin_ctx)


def _block_tail_kernel(of_ref, ob_ref, z_ref, ga_ref, wa_ref, ft_ref, x_ref, mod_ref, ng_ref, ones_ref, w_ref,
                       g2_ref, wg_ref, wu_ref, wd_ref, o_ref, *, lat_blocks, n_ctx):
    is_ctx = pl.program_id(1) == lat_blocks
    args = (of_ref, ob_ref, z_ref, ga_ref, wa_ref, ft_ref, x_ref, mod_ref, ng_ref, ones_ref, w_ref,
            g2_ref, wg_ref, wu_ref, wd_ref, o_ref)

    @pl.when(is_ctx)
    def _():
        _block_tail_rows(*args, rows=n_ctx)

    @pl.when(jnp.logical_not(is_ctx))
    def _():
        _block_tail_rows(*args, rows=o_ref.shape[1])


def _block_tail_rows(of_ref, ob_ref, z_ref, ga_ref, wa_ref, ft_ref, x_ref, mod_ref, ng_ref, ones_ref, w_ref,
                     g2_ref, wg_ref, wu_ref, wd_ref, o_ref, *, rows):
    o = of_ref[0, :rows] + ob_ref[0, :rows]
    z = z_ref[0, :rows]
    ones_bd = ones_ref[...]
    acc = None
    for s in range(2):
        t = o[:, s * LANES:(s + 1) * LANES]
        y = t * lax.rsqrt(_head_sumsq(t, ones_bd) * (1.0 / HEAD_DIM) + EPS) * ng_ref[...]
        y = (y * _silu(z[:, s * LANES:(s + 1) * LANES])).astype(BF16)
        part = jnp.dot(y, w_ref[s * LANES:(s + 1) * LANES, :], preferred_element_type=F32)
        acc = part if acc is None else acc + part
    for n, ref in enumerate((ga_ref, wa_ref, ft_ref)):
        acc = acc + jnp.dot(ref[0, :rows], w_ref[256 * (n + 1):256 * (n + 2), :], preferred_element_type=F32)
    x = x_ref[0, :rows] + mod_ref[0, 2:3, :] * acc

    ms = jnp.mean(x * x, -1, keepdims=True)
    y = x * lax.rsqrt(ms + EPS) * g2_ref[...]
    h = (y * (1.0 + mod_ref[0, 4:5, :]) + mod_ref[0, 3:4, :]).astype(BF16)
    acc = None
    for c in range(FFN_HIDDEN // FFN_CHUNK):
        cs = slice(c * FFN_CHUNK, (c + 1) * FFN_CHUNK)
        g = jnp.dot(h, wg_ref[:, cs], preferred_element_type=F32)
        u = jnp.dot(h, wu_ref[:, cs], preferred_element_type=F32)
        a = (_silu(g) * u).astype(BF16)
        part = jnp.dot(a, wd_ref[cs, :], preferred_element_type=F32)
        acc = part if acc is None else acc + part
    o_ref[0, :rows] = x + mod_ref[0, 5:6, :] * acc


def _layer_weight_spec(layer, shape):
    return pl.BlockSpec((None,) + shape, lambda b, j: (layer, 0, 0), pipeline_mode=pl.Buffered(1))


def _block_tail(of, ob, z, y_ga, y_wa, y_ft, xa, mod_l, ng, ones_bd, w_out, g2, wg, wu, wd, layer, n_lat,
                rows_out):
    B = xa.shape[0]
    const2 = lambda b, j: (0, 0)
    tok = lambda w: pl.BlockSpec((1, TM_TAIL, w), lambda b, j: (b, j, 0))
    resident = functools.partial(_layer_weight_spec, layer)
    return pl.pallas_call(
        functools.partial(_block_tail_kernel, lat_blocks=n_lat // TM_TAIL, n_ctx=of.shape[1] - n_lat),
        grid=(B, pl.cdiv(rows_out, TM_TAIL)),
        in_specs=[tok(256)] * 6 + [
            tok(D_MODEL),
            _mod_spec(n_lat // TM_TAIL),
            pl.BlockSpec((1, LANES), const2),
            pl.BlockSpec((LANES, LANES), const2),
            resident((D_MODEL, D_MODEL)),
            pl.BlockSpec((1, D_MODEL), const2),
            resident((D_MODEL, FFN_HIDDEN)),
            resident((D_MODEL, FFN_HIDDEN)),
            resident((FFN_HIDDEN, D_MODEL))],
        out_specs=tok(D_MODEL),
        out_shape=jax.ShapeDtypeStruct((B, rows_out, D_MODEL), F32),
        compiler_params=_cparams(("parallel", "parallel"), 52),
        name="out_projection_ffn",
    )(of, ob, z, y_ga, y_wa, y_ft, xa, mod_l, ng, ones_bd, w_out, g2, wg, wu, wd)


def _rope_tables(n_lat, n_rows):
    t = jnp.arange(n_lat, dtype=jnp.int32)
    row = (t // GRID_W).astype(F32)
    col = (t % GRID_W).astype(F32)
    n_freq = HEAD_DIM // 4
    inv_freq = ROPE_THETA ** (-jnp.arange(n_freq, dtype=F32) / n_freq)
    ang = jnp.concatenate([row[:, None] * inv_freq, col[:, None] * inv_freq], -1)
    cos = jnp.tile(jnp.cos(ang), (1, 4))
    sin = jnp.sin(ang)
    sin = jnp.concatenate([-sin, sin, -sin, sin], -1)
    cos = jnp.concatenate([cos, jnp.ones((n_rows - n_lat, LANES), F32)], 0)
    sin = jnp.concatenate([sin, jnp.zeros((n_rows - n_lat, LANES), F32)], 0)
    return cos, sin


def _dft_matrices(n, n_chan):
    scale = 1.0 / math.sqrt(n * n_chan)
    p = 64 if n % 64 == 0 and n > 64 else 1
    q = n // p
    col = jnp.arange(n, dtype=jnp.int32)[None, :]

    def angles(rows, period):
        return ((jnp.arange(rows, dtype=jnp.int32)[:, None] * col) % period).astype(F32) * (2.0 * math.pi / period)

    ang_a = angles(p, p)
    ang_b = angles(q, n)
    ca, sa = jnp.cos(ang_a)[:, None, :], jnp.sin(ang_a)[:, None, :]
    cb, sb = jnp.cos(ang_b)[None, :, :] * scale, jnp.sin(ang_b)[None, :, :] * scale
    cos = (ca * cb - sa * sb).reshape(n, n)
    msin = (-(sa * cb + ca * sb)).reshape(n, n)
    return cos.astype(BF16), msin.astype(BF16)


def _channel_dft_weights():
    n = HEAD_DIM
    idx = np.arange(n)
    ang = 2.0 * np.pi * ((idx[:, None] * idx[None, :]) % n) / n
    eye = np.eye(4)
    w = np.concatenate([np.kron(eye, np.cos(ang)), np.kron(eye, np.sin(ang))], 1)
    return jnp.asarray(w, F32).astype(BF16)


def kernel(x, c, ctx, c_ctx, norm1_g, norm2_g, w_ada, b_ada, w_in, dn_conv_w, dn_A_log, dn_dt_bias, dn_norm_g,
           ga_q_norm, ga_k_norm, wa_q_norm, wa_k_norm, wa_sink, w_out, w_ffn_gate, w_ffn_up, w_ffn_down):
    B, S, _ = x.shape
    L = ctx.shape[1]
    depth = w_in.shape[0]
    R = S + L
    assert S % TM == 0 and S % TM_TAIL == 0 and L <= TM_TAIL and B <= MOD_ROWS // 2
    assert all(S % t == 0 and L <= t for t in (TQ_WINDOW, TQ_DFT, TQ_GLOBAL))

    cc = jnp.zeros((MOD_ROWS, D_MODEL), F32).at[:B].set(c).at[MOD_ROWS // 2].set(c_ctx)
    mod = _modulation(cc, w_ada, b_ada).reshape(depth, MOD_ROWS, N_MOD, D_MODEL)

    cos_t, sin_t = _rope_tables(S, pl.cdiv(R, TM) * TM)
    cos_lat, msin_lat = _dft_matrices(S, HEAD_DIM)
    cos_ctx, msin_ctx = _dft_matrices(L, HEAD_DIM)
    ftw = _channel_dft_weights()
    ones_bd = jnp.asarray(np.kron(np.eye(2), np.ones((HEAD_DIM, HEAD_DIM))), F32).astype(BF16)
    tile2 = lambda g: jnp.tile(g, 2)

    w_in_r = jnp.concatenate([w_in[:, :, :1024], w_in[:, :, 1040:2320], w_in[:, :, 1024:1040],
                              jnp.zeros((depth, D_MODEL, C_END - 2320), F32)], 2).astype(BF16)
    w_out_b, w_gate_b, w_up_b, w_down_b = [t.astype(BF16) for t in (w_out, w_ffn_gate, w_ffn_up, w_ffn_down)]

    xa = jnp.concatenate([x, ctx], 1)
    for l in range(depth):
        last = l == depth - 1
        gains = jnp.zeros((8, LANES), F32)
        for n, g in enumerate((ga_q_norm[l], ga_k_norm[l], wa_q_norm[l], wa_k_norm[l])):
            gains = gains.at[n].set(tile2(g))
        conv_w8 = jnp.zeros((8, 768), F32).at[:3].set(dn_conv_w[l])
        neg_a = jnp.zeros((1, LANES), F32).at[0, :8].set(-jnp.exp(dn_A_log[l].astype(F32)).reshape(-1))
        dt_b = jnp.zeros((1, LANES), F32).at[0, :8].set(dn_dt_bias[l].astype(F32).reshape(-1))
        (dq, dk, dv, gb, z, gaq, gak, gav, waq, wak, wav, ftz) = _in_projection(
            xa, mod[l], norm1_g[l][None, :], w_in_r, cos_t, sin_t, gains, ones_bd, ftw, conv_w8, neg_a, dt_b, l, S)
        gt = gb[:, :, :8].reshape(B, R // CHUNK, CHUNK, 8).transpose(0, 1, 3, 2)
        o_f, o_b = _dn_scan(dq, dk, dv, gb, gt, S)

        y_ga = _global_attention(gaq, gak, gav, S)
        y_wa = _window_attention(waq, wak, wav, wa_sink[l].astype(F32), S)
        y_ft = _position_dft(ftz, cos_lat, msin_lat, cos_ctx, msin_ctx, S)

        rows_out = S if last else R
        xa = _block_tail(o_f, o_b, z, y_ga, y_wa, y_ft, xa, mod[l], tile2(dn_norm_g[l])[None, :], ones_bd,
                         w_out_b, norm2_g[l][None, :], w_gate_b, w_up_b, w_down_b, l, S, rows_out)
    return xa
```

```python
import functools
import math

import jax
import jax.numpy as jnp
import numpy as np
from jax import lax
from jax.experimental import pallas as pl
from jax.experimental.pallas import tpu as pltpu

F32 = jnp.float32
BF16 = jnp.bfloat16
HIGHEST = lax.Precision.HIGHEST

D_MODEL = 1024
HEAD_DIM = 64
N_HEADS = 4
N_KV = 2
MIX = N_HEADS * HEAD_DIM
GRID_W = 64
CHUNK = 64
WINDOW = 128
ROPE_THETA = 10000.0
FFN_HIDDEN = 2816
N_MOD = 6
EPS = 1e-6
NEG_INF = -1e30

LANES = 128
SUBLANES = 8
N_SCANS = 2 * N_HEADS
TM = 512
TM_TAIL = 512
TQ_GLOBAL = 256
TQ_WINDOW = 512
TQ_DFT = 512
DFT_BATCH = 2
MOD_ROWS = 16
FFN_CHUNK = 256
DN_BATCH = 8
LOG2E = math.log2(math.e)

C_QKV, C_Z, C_GA, C_WA, C_FT, C_AB, C_END = 0, 768, 1024, 1536, 2048, 2304, 2432


def _cparams(sem, vmem_mb):
    return pltpu.CompilerParams(dimension_semantics=sem, vmem_limit_bytes=vmem_mb * 1024 * 1024)


def _silu(x):
    return x * (1.0 / (1.0 + jnp.exp(-x)))


def _head_sumsq(t, ones_bd):
    return jnp.dot((t * t).astype(BF16), ones_bd, preferred_element_type=F32)


def _mod_spec(n_lat_blocks):
    ctx_row = MOD_ROWS // 2
    return pl.BlockSpec((1, N_MOD, D_MODEL), lambda b, j: (jnp.where(j == n_lat_blocks, ctx_row, b), 0, 0))


def _layer_weight_spec(layer, shape):
    return pl.BlockSpec((None,) + shape, lambda b, j: (layer, 0, 0), pipeline_mode=pl.Buffered(1))


def _mod_kernel(c_ref, w_ref, b_ref, o_ref):
    a = _silu(c_ref[...])
    o_ref[0] = jnp.dot(a, w_ref[0], preferred_element_type=F32, precision=HIGHEST) + b_ref[0]


def _modulation(cc, w_ada, b_ada):
    depth = w_ada.shape[0]
    return pl.pallas_call(
        _mod_kernel,
        grid=(depth, N_MOD),
        in_specs=[pl.BlockSpec((MOD_ROWS, D_MODEL), lambda l, n: (0, 0)),
                  pl.BlockSpec((1, D_MODEL, D_MODEL), lambda l, n: (l, 0, n)),
                  pl.BlockSpec((1, 1, D_MODEL), lambda l, n: (l, 0, n))],
        out_specs=pl.BlockSpec((1, MOD_ROWS, D_MODEL), lambda l, n: (l, 0, n)),
        out_shape=jax.ShapeDtypeStruct((depth, MOD_ROWS, N_MOD * D_MODEL), F32),
        compiler_params=_cparams(("parallel", "parallel"), 32),
        name="modulation",
    )(cc, w_ada, b_ada.reshape(depth, 1, N_MOD * D_MODEL))


def _inproj_kernel(x_ref, xp_ref, xn_ref, mod_ref, g1_ref, w_ref, cos_ref, sin_ref, gains_ref, ones_ref, ftw_ref,
                   cw_ref, na_ref, dtb_ref,
                   dq_ref, dk_ref, dv_ref, gb_ref, z_ref, gaq_ref, gak_ref, gav_ref, waq_ref, wak_ref, wav_ref,
                   ft_ref, *, lat_blocks, n_ctx):
    j = pl.program_id(1)
    tm = x_ref.shape[1]
    x = jnp.concatenate([x_ref[0], xp_ref[0], xn_ref[0]], 0)
    ms = jnp.mean(x * x, -1, keepdims=True)
    y = x * lax.rsqrt(ms + EPS) * g1_ref[...]
    h_all = (y * (1.0 + mod_ref[0, 1:2, :]) + mod_ref[0, 0:1, :]).astype(BF16)
    h = h_all[:tm]

    def proj(c0, c1):
        return jnp.dot(h, w_ref[:, c0:c1], preferred_element_type=F32)

    z_ref[0] = proj(C_Z, C_GA)
    ones_bd = ones_ref[...]

    qkv = jnp.dot(h_all, w_ref[:, C_QKV:C_Z], preferred_element_type=F32)
    cur = qkv[:tm]
    row = lax.broadcasted_iota(jnp.int32, cur.shape, 0)
    is_ctx = j == lat_blocks
    has_prev = jnp.where((j == 0) | is_ctx, 0.0, 1.0)
    has_next = jnp.where(j >= lat_blocks - 1, 0.0, 1.0)
    last_row = jnp.where(is_ctx, n_ctx - 1, tm - 1)
    prev_row = qkv[tm + SUBLANES - 1:tm + SUBLANES] * has_prev
    next_row = qkv[tm + SUBLANES:tm + SUBLANES + 1] * has_next
    up = jnp.where(row == 0, prev_row, pltpu.roll(cur, 1, 0))
    dn = jnp.where(row == last_row, next_row, pltpu.roll(cur, tm - 1, 0))
    a = _silu(up * cw_ref[0:1, :] + cur * cw_ref[1:2, :] + dn * cw_ref[2:3, :])
    for s in range(2):
        lanes = slice(s * LANES, (s + 1) * LANES)
        tq = a[:, lanes]
        dq_ref[0, :, lanes] = tq * lax.rsqrt(_head_sumsq(tq, ones_bd) + EPS) * HEAD_DIM ** -0.5
        tk = a[:, MIX + s * LANES:MIX + (s + 1) * LANES]
        dk_ref[0, :, lanes] = tk * lax.rsqrt(_head_sumsq(tk, ones_bd) + EPS)
    dv_ref[0] = a[:, 2 * MIX:3 * MIX]

    ab = proj(C_AB, C_END)
    t = ab + dtb_ref[...]
    softplus = jnp.maximum(t, 0.0) + jnp.log(1.0 + jnp.exp(-jnp.abs(t)))
    g = na_ref[...] * softplus
    beta = 1.0 / (1.0 + jnp.exp(-ab))
    lane_c = lax.broadcasted_iota(jnp.int32, (CHUNK, LANES), 1)
    ii = lax.broadcasted_iota(jnp.int32, (CHUNK, CHUNK), 0)
    jj = lax.broadcasted_iota(jnp.int32, (CHUNK, CHUNK), 1)
    prefix = jnp.where(ii >= jj, 1.0, 0.0)
    for c in range(tm // CHUNK):
        rows = slice(c * CHUNK, (c + 1) * CHUNK)
        gc = g[rows]
        fwd = jnp.dot(prefix, gc, preferred_element_type=F32, precision=HIGHEST)
        bwd = fwd[CHUNK - 1:CHUNK, :] - fwd + gc
        gb_ref[0, rows, :] = jnp.where(lane_c < N_HEADS, fwd, jnp.where(lane_c < 2 * N_HEADS, bwd, beta[rows]))

    cos_t = cos_ref[...]
    sin_t = sin_ref[...]
    lane = lax.broadcasted_iota(jnp.int32, cos_t.shape, 1)
    first_half = (lane & (HEAD_DIM // 2)) == 0

    def norm_rope(t, gain_row, scale):
        yv = t * lax.rsqrt(_head_sumsq(t, ones_bd) * (1.0 / HEAD_DIM) + EPS) * gain_row
        partner = jnp.where(first_half, pltpu.roll(yv, LANES - HEAD_DIM // 2, 1), pltpu.roll(yv, HEAD_DIM // 2, 1))
        return (yv * cos_t + partner * sin_t) * scale

    def heads_out(ref, t, n):
        for i in range(n):
            ref[0, i] = t[:, i * HEAD_DIM:(i + 1) * HEAD_DIM].astype(BF16)

    for c0, q_ref, k_ref, v_ref, gq, gk in ((C_GA, gaq_ref, gak_ref, gav_ref, 0, 1),
                                            (C_WA, waq_ref, wak_ref, wav_ref, 2, 3)):
        p = proj(c0, c0 + 2 * MIX)
        scale = HEAD_DIM ** -0.5 * LOG2E
        q = jnp.concatenate([norm_rope(p[:, 0:LANES], gains_ref[gq:gq + 1, :], scale),
                             norm_rope(p[:, LANES:MIX], gains_ref[gq:gq + 1, :], scale)], -1)
        k = norm_rope(p[:, MIX:MIX + LANES], gains_ref[gk:gk + 1, :], 1.0)
        heads_out(q_ref, q, N_HEADS)
        if c0 == C_GA:
            k_ref[0] = k.T.reshape(N_KV, HEAD_DIM, k.shape[0]).astype(BF16)
        else:
            heads_out(k_ref, k, N_KV)
        for i in range(N_KV):
            vh = p[:, MIX + LANES + i * HEAD_DIM:MIX + LANES + (i + 1) * HEAD_DIM]
            v_ref[0, i] = jnp.where(lane < HEAD_DIM, jnp.concatenate([vh, vh], -1),
                                    jnp.where(lane == HEAD_DIM, 1.0, 0.0)).astype(BF16)

    pf = proj(C_FT, C_AB).astype(BF16)
    ft_ref[0] = jnp.dot(pf, ftw_ref[...], preferred_element_type=F32).astype(BF16)


def _in_projection(xa, mod_l, g1, w_in_r, cos_t, sin_t, gains, ones_bd, ftw, conv_w8, neg_a, dt_b, layer, n_lat):
    B, R, _ = xa.shape
    nb = pl.cdiv(R, TM)
    hb = TM // SUBLANES
    const2 = lambda b, j: (0, 0)
    tok = lambda w: pl.BlockSpec((1, TM, w), lambda b, j: (b, j, 0))
    hd = lambda n, w=HEAD_DIM: pl.BlockSpec((1, n, TM, w), lambda b, j: (b, 0, j, 0))
    sds = jax.ShapeDtypeStruct
    heads = lambda n, w=HEAD_DIM: sds((B, n, R, w), BF16)
    return pl.pallas_call(
        functools.partial(_inproj_kernel, lat_blocks=n_lat // TM, n_ctx=R - n_lat),
        grid=(B, nb),
        in_specs=[tok(D_MODEL),
                  pl.BlockSpec((1, SUBLANES, D_MODEL), lambda b, j: (b, jnp.maximum(j * hb - 1, 0), 0)),
                  pl.BlockSpec((1, SUBLANES, D_MODEL),
                               lambda b, j: (b, jnp.minimum((j + 1) * hb, R // SUBLANES - 1), 0)),
                  _mod_spec(n_lat // TM),
                  pl.BlockSpec((1, D_MODEL), const2),
                  _layer_weight_spec(layer, (D_MODEL, C_END)),
                  pl.BlockSpec((TM, LANES), lambda b, j: (j, 0)),
                  pl.BlockSpec((TM, LANES), lambda b, j: (j, 0)),
                  pl.BlockSpec((SUBLANES, LANES), const2),
                  pl.BlockSpec((LANES, LANES), const2),
                  pl.BlockSpec((MIX, 2 * MIX), const2),
                  pl.BlockSpec((SUBLANES, 3 * MIX), const2),
                  pl.BlockSpec((1, LANES), const2),
                  pl.BlockSpec((1, LANES), const2)],
        out_specs=[tok(MIX), tok(MIX), tok(MIX), tok(LANES), tok(MIX),
                   hd(N_HEADS), pl.BlockSpec((1, N_KV, HEAD_DIM, TM), lambda b, j: (b, 0, 0, j)), hd(N_KV, LANES),
                   hd(N_HEADS), hd(N_KV), hd(N_KV, LANES),
                   tok(2 * MIX)],
        out_shape=[sds((B, R, MIX), F32)] * 3 + [sds((B, R, LANES), F32), sds((B, R, MIX), F32),
                   heads(N_HEADS), sds((B, N_KV, HEAD_DIM, R), BF16), heads(N_KV, LANES),
                   heads(N_HEADS), heads(N_KV), heads(N_KV, LANES),
                   sds((B, R, 2 * MIX), BF16)],
        compiler_params=_cparams(("parallel", "parallel"), 48),
        name="in_projection",
    )(xa, xa, xa, mod_l, g1, w_in_r, cos_t, sin_t, gains, ones_bd, ftw, conv_w8, neg_a, dt_b)


def _bdot(a, b, dims):
    return lax.dot_general(a.astype(BF16), b.astype(BF16), dims, preferred_element_type=F32)


_NN = (((2,), (1,)), ((0,), (0,)))
_NT = (((2,), (2,)), ((0,), (0,)))
_TN = (((1,), (1,)), ((0,), (0,)))


def _pair_blockdiag(x):
    first = (lax.broadcasted_iota(jnp.int32, x.shape, 2) & HEAD_DIM) == 0
    return jnp.concatenate([jnp.where(first, x, 0.0), jnp.where(first, 0.0, x)], 1)


def _dnscan_kernel(qf_ref, kf_ref, vf_ref, gbf_ref, gtf_ref, qb_ref, kb_ref, vb_ref, gbb_ref, gtb_ref,
                   of_ref, ob_ref, sf_ref, sb_ref, *, bb):
    i = pl.program_id(1)

    @pl.when(i == 0)
    def _():
        sf_ref[...] = jnp.zeros_like(sf_ref)
        sb_ref[...] = jnp.zeros_like(sb_ref)

    _dn_direction(qf_ref, kf_ref, vf_ref, gbf_ref, gtf_ref, of_ref, sf_ref, bb, 0)
    _dn_direction(qb_ref, kb_ref, vb_ref, gbb_ref, gtb_ref, ob_ref, sb_ref, bb, 1)


def _dn_direction(q_ref, k_ref, v_ref, gb_ref, gt_ref, o_ref, state_ref, bb, d):
    lane2 = lax.broadcasted_iota(jnp.int32, (CHUNK, LANES), 1)
    first2 = (lane2 & HEAD_DIM) == 0
    last = CHUNK - 1 if d == 0 else 0
    gc_cols, b_cols, gc_rows, g_lasts, qs, ks, vs = [], [], [], [], [], [], []
    for b in range(bb):
        gb = gb_ref[b]
        gt = gt_ref[b, 0]
        for m in range(N_HEADS // 2):
            c = d * N_HEADS + 2 * m
            lanes = slice(m * LANES, (m + 1) * LANES)
            qs.append(q_ref[b, :, lanes])
            ks.append(k_ref[b, :, lanes])
            vs.append(v_ref[b, :, lanes])
            col = jnp.where(first2, gb[:, c:c + 1], gb[:, c + 1:c + 2])
            gc_cols.append(col)
            g_lasts.append(col[last:last + 1, :])
            b_cols.append(jnp.where(first2, gb[:, 2 * N_HEADS + c:2 * N_HEADS + c + 1],
                                    gb[:, 2 * N_HEADS + c + 1:2 * N_HEADS + c + 2]))
            gc_rows.append(jnp.concatenate([gt[c:c + 1, :], gt[c + 1:c + 2, :]], 1))
    q = jnp.stack(qs)
    k = jnp.stack(ks)
    v = jnp.stack(vs)
    gc_col = jnp.stack(gc_cols)
    b_col = jnp.stack(b_cols)
    gc_row = jnp.stack(gc_rows)
    g_last = jnp.stack(g_lasts)
    shape = q.shape

    ii = lax.broadcasted_iota(jnp.int32, (CHUNK, LANES), 0)
    jj = lane2 & (HEAD_DIM - 1)
    incl = (ii >= jj) if d == 0 else (ii <= jj)
    strict = (ii > jj) if d == 0 else (ii < jj)
    eye = jnp.where(ii == jj, 1.0, 0.0)
    xor = ii ^ jj
    blocks = [(xor >> sh) == 1 for sh in range(6)]
    bd = _pair_blockdiag

    decay = jnp.exp(jnp.where(incl, gc_col - gc_row, NEG_INF))
    e_gc = jnp.exp(gc_col)
    kb = k * b_col
    kq = _bdot(jnp.concatenate([kb, q], 1), bd(k), _NT)
    a_mat = jnp.where(strict, kq[:, :CHUNK] * decay, 0.0)
    intra = jnp.where(incl, kq[:, CHUNK:] * decay, 0.0)
    t = eye - jnp.where(blocks[0], a_mat, 0.0)
    for mask in blocks[1:-1]:
        t = t - _bdot(t, bd(_bdot(jnp.where(mask, a_mat, 0.0), bd(t), _NN)), _NN)
    rhs = jnp.concatenate([v * b_col, kb * e_gc], 2)
    half = _bdot(t, bd(rhs), _NN)
    sol = half - _bdot(t, bd(_bdot(jnp.where(blocks[-1], a_mat, 0.0), bd(half), _NN)), _NN)
    u, w = sol[:, :, :LANES], sol[:, :, LANES:]
    state = state_ref[...]
    wq = _bdot(jnp.concatenate([w, q * e_gc], 1), bd(state), _NN)
    v_new = u - wq[:, :CHUNK]
    o = wq[:, CHUNK:] + _bdot(intra, bd(v_new), _NN)
    k_dec = k * jnp.exp(g_last - gc_col)
    kv = _bdot(bd(k_dec), bd(v_new), _TN)
    first3 = (lax.broadcasted_iota(jnp.int32, shape, 2) & HEAD_DIM) == 0
    state_ref[...] = state * jnp.exp(g_last) + jnp.where(first3, kv[:, :CHUNK], kv[:, CHUNK:])

    for b in range(bb):
        for m in range(N_HEADS // 2):
            o_ref[b, :, m * LANES:(m + 1) * LANES] = o[b * (N_HEADS // 2) + m]


def _dn_scan(q, k, v, gb, gt, n_lat):
    B, R, _ = q.shape
    nc = R // CHUNK
    nlc = n_lat // CHUNK
    ncc = nc - nlc
    bb = DN_BATCH if B % DN_BATCH == 0 else 1

    def fwd_chunk(i):
        return jnp.where(i < ncc, nlc + i, i - ncc)

    def bwd_chunk(i):
        return jnp.where(i < ncc, nc - 1 - i, nlc - 1 - (i - ncc))

    tokf = lambda w: pl.BlockSpec((bb, CHUNK, w), lambda b, i: (b, fwd_chunk(i), 0))
    tokb = lambda w: pl.BlockSpec((bb, CHUNK, w), lambda b, i: (b, bwd_chunk(i), 0))
    gtf = pl.BlockSpec((bb, 1, N_SCANS, CHUNK), lambda b, i: (b, fwd_chunk(i), 0, 0))
    gtb = pl.BlockSpec((bb, 1, N_SCANS, CHUNK), lambda b, i: (b, bwd_chunk(i), 0, 0))
    sds = jax.ShapeDtypeStruct
    return pl.pallas_call(
        functools.partial(_dnscan_kernel, bb=bb),
        grid=(B // bb, nc),
        in_specs=[tokf(MIX), tokf(MIX), tokf(MIX), tokf(LANES), gtf,
                  tokb(MIX), tokb(MIX), tokb(MIX), tokb(LANES), gtb],
        out_specs=[tokf(MIX), tokb(MIX)],
        out_shape=[sds((B, R, MIX), F32), sds((B, R, MIX), F32)],
        scratch_shapes=[pltpu.VMEM((N_HEADS // 2 * bb, HEAD_DIM, LANES), F32)] * 2,
        compiler_params=_cparams(("parallel", "arbitrary"), 32),
        name="deltanet_scan",
    )(q, k, v, gb, gt, q, k, v, gb, gt)


def _softmax_pv(s_list, v_list, sink):
    m = functools.reduce(jnp.maximum, [s.max(-1, keepdims=True) for s in s_list])
    if sink is not None:
        m = jnp.maximum(m, sink)
    ov = functools.reduce(lambda a, b: a + b, [jnp.dot(jnp.exp2(s - m).astype(BF16), v, preferred_element_type=F32)
                                               for s, v in zip(s_list, v_list)])
    den = ov[:, HEAD_DIM:HEAD_DIM + 1]
    if sink is not None:
        den = den + jnp.exp2(sink - m)
    return ov[:, :HEAD_DIM] / den


def _scores(q, k):
    return lax.dot_general(q, k, (((1,), (1,)), ((), ())), preferred_element_type=F32)


def _store_heads(o_ref, heads, rows):
    o_ref[0, :rows, :] = jnp.concatenate(heads, -1).astype(BF16)


def _gattn_kernel(q_ref, kt_ref, v_ref, o_ref, *, n_lat):
    j = pl.program_id(1)
    tq = q_ref.shape[2]
    n_ctx = kt_ref.shape[3] - n_lat
    group = N_HEADS // N_KV

    def scores_t(q, kt):
        return jnp.dot(q, kt, preferred_element_type=F32)

    @pl.when(j == n_lat // tq)
    def _():
        _store_heads(o_ref, [_softmax_pv([scores_t(q_ref[0, h, :n_ctx, :], kt_ref[0, h // group, :, n_lat:])],
                                         [v_ref[0, h // group, n_lat:, :]], None)
                             for h in range(N_HEADS)], n_ctx)

    @pl.when(j < n_lat // tq)
    def _():
        _store_heads(o_ref, [_softmax_pv([scores_t(q_ref[0, h], kt_ref[0, h // group])], [v_ref[0, h // group]],
                                         None) for h in range(N_HEADS)], tq)


def _global_attention(q, k, v, n_lat):
    B, _, R, _ = q.shape
    tq = TQ_GLOBAL
    return pl.pallas_call(
        functools.partial(_gattn_kernel, n_lat=n_lat),
        grid=(B, pl.cdiv(R, tq)),
        in_specs=[pl.BlockSpec((1, N_HEADS, tq, HEAD_DIM), lambda b, j: (b, 0, j, 0)),
                  pl.BlockSpec((1, N_KV, HEAD_DIM, R), lambda b, j: (b, 0, 0, 0)),
                  pl.BlockSpec((1, N_KV, R, LANES), lambda b, j: (b, 0, 0, 0))],
        out_specs=pl.BlockSpec((1, tq, N_HEADS * HEAD_DIM), lambda b, j: (b, j, 0)),
        out_shape=jax.ShapeDtypeStruct((B, R, N_HEADS * HEAD_DIM), BF16),
        compiler_params=_cparams(("parallel", "parallel"), 56),
        name="global_attention",
    )(q, k, v)


def _wattn_kernel(sink_ref, bias_ref, q_ref, k_ref, v_ref, o_ref, *, n_lat):
    j = pl.program_id(1)
    tq = q_ref.shape[2]
    n_ctx = k_ref.shape[2] - n_lat
    band = tq + 2 * WINDOW
    group = N_HEADS // N_KV
    sinks = [sink_ref[h] * LOG2E for h in range(N_HEADS)]

    @pl.when(j == n_lat // tq)
    def _():
        _store_heads(o_ref, [_softmax_pv([_scores(q_ref[0, h, :n_ctx, :], k_ref[0, h // group, n_lat:, :])],
                                         [v_ref[0, h // group, n_lat:, :]], sinks[h])
                             for h in range(N_HEADS)], n_ctx)

    @pl.when(j < n_lat // tq)
    def _():
        q0 = j * tq
        start = pl.multiple_of(jnp.clip(q0 - WINDOW, 0, n_lat - band), WINDOW)
        bias = bias_ref[(q0 - start) // WINDOW]
        heads = []
        for h in range(N_HEADS):
            kv = h // group
            q = q_ref[0, h]
            s_loc = _scores(q, k_ref[0, kv, pl.ds(start, band), :]) + bias
            s_ctx = _scores(q, k_ref[0, kv, n_lat:, :])
            heads.append(_softmax_pv([s_loc, s_ctx], [v_ref[0, kv, pl.ds(start, band), :],
                                                      v_ref[0, kv, n_lat:, :]], sinks[h]))
        _store_heads(o_ref, heads, tq)


def _window_bias(tq):
    band = tq + 2 * WINDOW
    off = jnp.arange(3, dtype=jnp.int32)[:, None, None] * WINDOW
    r = jnp.arange(tq, dtype=jnp.int32)[None, :, None]
    c = jnp.arange(band, dtype=jnp.int32)[None, None, :]
    return jnp.where(jnp.abs(r + off - c) <= WINDOW, 0.0, NEG_INF).astype(F32)


def _window_attention(q, k, v, sink, n_lat):
    B, _, R, _ = q.shape
    tq = TQ_WINDOW
    bias = _window_bias(tq)
    return pl.pallas_call(
        functools.partial(_wattn_kernel, n_lat=n_lat),
        grid=(B, pl.cdiv(R, tq)),
        in_specs=[pl.BlockSpec(memory_space=pltpu.SMEM),
                  pl.BlockSpec(bias.shape, lambda b, j: (0, 0, 0), pipeline_mode=pl.Buffered(1)),
                  pl.BlockSpec((1, N_HEADS, tq, HEAD_DIM), lambda b, j: (b, 0, j, 0)),
                  pl.BlockSpec((1, N_KV, R, HEAD_DIM), lambda b, j: (b, 0, 0, 0)),
                  pl.BlockSpec((1, N_KV, R, LANES), lambda b, j: (b, 0, 0, 0))],
        out_specs=pl.BlockSpec((1, tq, N_HEADS * HEAD_DIM), lambda b, j: (b, j, 0)),
        out_shape=jax.ShapeDtypeStruct((B, R, N_HEADS * HEAD_DIM), BF16),
        compiler_params=_cparams(("parallel", "parallel"), 48),
        name="window_attention",
    )(sink, bias, q, k, v)


def _dft_kernel(z_ref, cl_ref, sl_ref, cc_ref, sc_ref, o_ref, *, n_lat):
    j = pl.program_id(1)
    half = z_ref.shape[2] // 2

    def mix(c, s, z):
        return (jnp.dot(c, z[:, :half], preferred_element_type=F32)
                + jnp.dot(s, z[:, half:], preferred_element_type=F32)).astype(BF16)

    n_ctx = z_ref.shape[1] - n_lat

    @pl.when(j == n_lat // TQ_DFT)
    def _():
        for b in range(z_ref.shape[0]):
            o_ref[b, :n_ctx, :] = mix(cc_ref[...], sc_ref[...], z_ref[b, n_lat:, :])

    @pl.when(j < n_lat // TQ_DFT)
    def _():
        for b in range(z_ref.shape[0]):
            o_ref[b] = mix(cl_ref[...], sl_ref[...], z_ref[b, :n_lat, :])


def _position_dft(z, cos_lat, msin_lat, cos_ctx, msin_ctx, n_lat):
    B, R, _ = z.shape
    n_ctx = R - n_lat
    bb = DFT_BATCH if B % DFT_BATCH == 0 else 1
    lat_blocks = n_lat // TQ_DFT
    lat = pl.BlockSpec((TQ_DFT, n_lat), lambda b, j: (jnp.minimum(j, lat_blocks - 1), 0))
    ctx = pl.BlockSpec((n_ctx, n_ctx), lambda b, j: (0, 0))
    return pl.pallas_call(
        functools.partial(_dft_kernel, n_lat=n_lat),
        grid=(B // bb, pl.cdiv(R, TQ_DFT)),
        in_specs=[pl.BlockSpec((bb, R, 2 * MIX), lambda b, j: (b, 0, 0), pipeline_mode=pl.Buffered(1)),
                  lat, lat, ctx, ctx],
        out_specs=pl.BlockSpec((bb, TQ_DFT, MIX), lambda b, j: (b, j, 0)),
        out_shape=jax.ShapeDtypeStruct((B, R, MIX), BF16),
        compiler_params=_cparams(("parallel", "parallel"), 48),
        name="position_dft",
    )(z, cos_lat, msin_lat, cos_ctx, msin_ctx)


def _block_tail_kernel(of_ref, ob_ref, z_ref, ga_ref, wa_ref, ft_ref, x_ref, mod_ref, ng_ref, ones_ref, w_ref,
                       g2_ref, wg_ref, wu_ref, wd_ref, o_ref, *, lat_blocks, n_ctx):
    is_ctx = pl.program_id(1) == lat_blocks
    args = (of_ref, ob_ref, z_ref, ga_ref, wa_ref, ft_ref, x_ref, mod_ref, ng_ref, ones_ref, w_ref,
            g2_ref, wg_ref, wu_ref, wd_ref, o_ref)

    @pl.when(is_ctx)
    def _():
        _block_tail_rows(*args, rows=n_ctx)

    @pl.when(jnp.logical_not(is_ctx))
    def _():
        _block_tail_rows(*args, rows=o_ref.shape[1])


def _block_tail_rows(of_ref, ob_ref, z_ref, ga_ref, wa_ref, ft_ref, x_ref, mod_ref, ng_ref, ones_ref, w_ref,
                     g2_ref, wg_ref, wu_ref, wd_ref, o_ref, *, rows):
    o = of_ref[0, :rows] + ob_ref[0, :rows]
    z = z_ref[0, :rows]
    ones_bd = ones_ref[...]
    acc = None
    for s in range(2):
        t = o[:, s * LANES:(s + 1) * LANES]
        y = t * lax.rsqrt(_head_sumsq(t, ones_bd) * (1.0 / HEAD_DIM) + EPS) * ng_ref[...]
        y = (y * _silu(z[:, s * LANES:(s + 1) * LANES])).astype(BF16)
        part = jnp.dot(y, w_ref[s * LANES:(s + 1) * LANES, :], preferred_element_type=F32)
        acc = part if acc is None else acc + part
    for n, ref in enumerate((ga_ref, wa_ref, ft_ref)):
        acc = acc + jnp.dot(ref[0, :rows], w_ref[MIX * (n + 1):MIX * (n + 2), :], preferred_element_type=F32)
    x = x_ref[0, :rows] + mod_ref[0, 2:3, :] * acc

    ms = jnp.mean(x * x, -1, keepdims=True)
    y = x * lax.rsqrt(ms + EPS) * g2_ref[...]
    h = (y * (1.0 + mod_ref[0, 4:5, :]) + mod_ref[0, 3:4, :]).astype(BF16)
    acc = None
    for c in range(FFN_HIDDEN // FFN_CHUNK):
        cs = slice(c * FFN_CHUNK, (c + 1) * FFN_CHUNK)
        g = jnp.dot(h, wg_ref[:, cs], preferred_element_type=F32)
        u = jnp.dot(h, wu_ref[:, cs], preferred_element_type=F32)
        a = (_silu(g) * u).astype(BF16)
        part = jnp.dot(a, wd_ref[cs, :], preferred_element_type=F32)
        acc = part if acc is None else acc + part
    o_ref[0, :rows] = x + mod_ref[0, 5:6, :] * acc


def _block_tail(of, ob, z, y_ga, y_wa, y_ft, xa, mod_l, ng, ones_bd, w_out, g2, wg, wu, wd, layer, n_lat,
                rows_out):
    B = xa.shape[0]
    const2 = lambda b, j: (0, 0)
    tok = lambda w: pl.BlockSpec((1, TM_TAIL, w), lambda b, j: (b, j, 0))
    resident = functools.partial(_layer_weight_spec, layer)
    return pl.pallas_call(
        functools.partial(_block_tail_kernel, lat_blocks=n_lat // TM_TAIL, n_ctx=of.shape[1] - n_lat),
        grid=(B, pl.cdiv(rows_out, TM_TAIL)),
        in_specs=[tok(MIX)] * 6 + [
            tok(D_MODEL),
            _mod_spec(n_lat // TM_TAIL),
            pl.BlockSpec((1, LANES), const2),
            pl.BlockSpec((LANES, LANES), const2),
            resident((D_MODEL, D_MODEL)),
            pl.BlockSpec((1, D_MODEL), const2),
            resident((D_MODEL, FFN_HIDDEN)),
            resident((D_MODEL, FFN_HIDDEN)),
            resident((FFN_HIDDEN, D_MODEL))],
        out_specs=tok(D_MODEL),
        out_shape=jax.ShapeDtypeStruct((B, rows_out, D_MODEL), F32),
        compiler_params=_cparams(("parallel", "parallel"), 52),
        name="out_projection_ffn",
    )(of, ob, z, y_ga, y_wa, y_ft, xa, mod_l, ng, ones_bd, w_out, g2, wg, wu, wd)


def _rope_tables(n_lat, n_rows):
    t = jnp.arange(n_lat, dtype=jnp.int32)
    row = (t // GRID_W).astype(F32)
    col = (t % GRID_W).astype(F32)
    n_freq = HEAD_DIM // 4
    inv_freq = ROPE_THETA ** (-jnp.arange(n_freq, dtype=F32) / n_freq)
    ang = jnp.concatenate([row[:, None] * inv_freq, col[:, None] * inv_freq], -1)
    cos = jnp.tile(jnp.cos(ang), (1, 4))
    sin = jnp.sin(ang)
    sin = jnp.concatenate([-sin, sin, -sin, sin], -1)
    cos = jnp.concatenate([cos, jnp.ones((n_rows - n_lat, LANES), F32)], 0)
    sin = jnp.concatenate([sin, jnp.zeros((n_rows - n_lat, LANES), F32)], 0)
    return cos, sin


def _dft_matrices(n, n_chan):
    scale = 1.0 / math.sqrt(n * n_chan)
    p = 64 if n % 64 == 0 and n > 64 else 1
    q = n // p
    col = jnp.arange(n, dtype=jnp.int32)[None, :]

    def angles(rows, period):
        return ((jnp.arange(rows, dtype=jnp.int32)[:, None] * col) % period).astype(F32) * (2.0 * math.pi / period)

    ang_a = angles(p, p)
    ang_b = angles(q, n)
    ca, sa = jnp.cos(ang_a)[:, None, :], jnp.sin(ang_a)[:, None, :]
    cb, sb = jnp.cos(ang_b)[None, :, :] * scale, jnp.sin(ang_b)[None, :, :] * scale
    cos = (ca * cb - sa * sb).reshape(n, n)
    msin = (-(sa * cb + ca * sb)).reshape(n, n)
    return cos.astype(BF16), msin.astype(BF16)


def _channel_dft_weights():
    n = HEAD_DIM
    idx = np.arange(n)
    ang = 2.0 * np.pi * ((idx[:, None] * idx[None, :]) % n) / n
    eye = np.eye(4)
    w = np.concatenate([np.kron(eye, np.cos(ang)), np.kron(eye, np.sin(ang))], 1)
    return jnp.asarray(w, F32).astype(BF16)


def kernel(x, c, ctx, c_ctx, norm1_g, norm2_g, w_ada, b_ada, w_in, dn_conv_w, dn_A_log, dn_dt_bias, dn_norm_g,
           ga_q_norm, ga_k_norm, wa_q_norm, wa_k_norm, wa_sink, w_out, w_ffn_gate, w_ffn_up, w_ffn_down):
    B, S, _ = x.shape
    L = ctx.shape[1]
    depth = w_in.shape[0]
    R = S + L
    assert S % TM == 0 and S % TM_TAIL == 0 and L <= TM_TAIL and B <= MOD_ROWS // 2
    assert all(S % t == 0 and L <= t for t in (TQ_WINDOW, TQ_DFT, TQ_GLOBAL))

    cc = jnp.zeros((MOD_ROWS, D_MODEL), F32).at[:B].set(c).at[MOD_ROWS // 2].set(c_ctx)
    mod = _modulation(cc, w_ada, b_ada).reshape(depth, MOD_ROWS, N_MOD, D_MODEL)

    cos_t, sin_t = _rope_tables(S, pl.cdiv(R, TM) * TM)
    cos_lat, msin_lat = _dft_matrices(S, HEAD_DIM)
    cos_ctx, msin_ctx = _dft_matrices(L, HEAD_DIM)
    ftw = _channel_dft_weights()
    ones_bd = jnp.asarray(np.kron(np.eye(2), np.ones((HEAD_DIM, HEAD_DIM))), F32).astype(BF16)
    tile2 = lambda g: jnp.tile(g, 2)

    w_in_r = jnp.concatenate([w_in[:, :, :1024], w_in[:, :, 1040:2320], w_in[:, :, 1024:1040],
                              jnp.zeros((depth, D_MODEL, C_END - 2320), F32)], 2).astype(BF16)
    w_out_b, w_gate_b, w_up_b, w_down_b = [t.astype(BF16) for t in (w_out, w_ffn_gate, w_ffn_up, w_ffn_down)]

    xa = jnp.concatenate([x, ctx], 1)
    for l in range(depth):
        last = l == depth - 1
        gains = jnp.zeros((SUBLANES, LANES), F32)
        for n, g in enumerate((ga_q_norm[l], ga_k_norm[l], wa_q_norm[l], wa_k_norm[l])):
            gains = gains.at[n].set(tile2(g))
        conv_w8 = jnp.zeros((SUBLANES, 3 * MIX), F32).at[:dn_conv_w.shape[1]].set(dn_conv_w[l])
        neg_a = jnp.zeros((1, LANES), F32).at[0, :N_SCANS].set(-jnp.exp(dn_A_log[l].astype(F32)).reshape(-1))
        dt_b = jnp.zeros((1, LANES), F32).at[0, :N_SCANS].set(dn_dt_bias[l].astype(F32).reshape(-1))
        (dq, dk, dv, gb, z, gaq, gak, gav, waq, wak, wav, ftz) = _in_projection(
            xa, mod[l], norm1_g[l][None, :], w_in_r, cos_t, sin_t, gains, ones_bd, ftw, conv_w8, neg_a, dt_b, l, S)
        gt = gb[:, :, :N_SCANS].reshape(B, R // CHUNK, CHUNK, N_SCANS).transpose(0, 1, 3, 2)
        o_f, o_b = _dn_scan(dq, dk, dv, gb, gt, S)

        y_ga = _global_attention(gaq, gak, gav, S)
        y_wa = _window_attention(waq, wak, wav, wa_sink[l].astype(F32), S)
        y_ft = _position_dft(ftz, cos_lat, msin_lat, cos_ctx, msin_ctx, S)

        rows_out = S if last else R
        xa = _block_tail(o_f, o_b, z, y_ga, y_wa, y_ft, xa, mod[l], tile2(dn_norm_g[l])[None, :], ones_bd,
                         w_out_b, norm2_g[l][None, :], w_gate_b, w_up_b, w_down_b, l, S, rows_out)
    return xa
```

```python
import functools
import math

import jax
import jax.numpy as jnp
import numpy as np
from jax import lax
from jax.experimental import pallas as pl
from jax.experimental.pallas import tpu as pltpu

F32 = jnp.float32
BF16 = jnp.bfloat16
HIGHEST = lax.Precision.HIGHEST

D_MODEL = 1024
HEAD_DIM = 64
N_HEADS = 4
N_KV = 2
MIX = N_HEADS * HEAD_DIM
GRID_W = 64
CHUNK = 64
WINDOW = 128
ROPE_THETA = 10000.0
FFN_HIDDEN = 2816
N_MOD = 6
EPS = 1e-6
NEG_INF = -1e30

LANES = 128
SUBLANES = 8
N_SCANS = 2 * N_HEADS
TM = 512
TM_TAIL = 512
TQ_GLOBAL = 256
TQ_WINDOW = 512
TQ_DFT = 512
DFT_BATCH = 2
MOD_ROWS = 16
FFN_CHUNK = 256
DN_BATCH = 8
LOG2E = math.log2(math.e)

C_QKV = 0
C_Z = C_QKV + 3 * MIX
C_GA = C_Z + MIX
C_WA = C_GA + 2 * MIX
C_FT = C_WA + 2 * MIX
C_AB = C_FT + MIX
C_END = C_AB + LANES


def _cparams(sem, vmem_mb):
    return pltpu.CompilerParams(dimension_semantics=sem, vmem_limit_bytes=vmem_mb * 1024 * 1024)


def _silu(x):
    return x * (1.0 / (1.0 + jnp.exp(-x)))


def _head_sumsq(t, ones_bd):
    return jnp.dot((t * t).astype(BF16), ones_bd, preferred_element_type=F32)


def _mod_spec(n_lat_blocks):
    ctx_row = MOD_ROWS // 2
    return pl.BlockSpec((1, N_MOD, D_MODEL), lambda b, j: (jnp.where(j == n_lat_blocks, ctx_row, b), 0, 0))


def _layer_weight_spec(layer, shape):
    return pl.BlockSpec((None,) + shape, lambda b, j: (layer, 0, 0), pipeline_mode=pl.Buffered(1))


def _mod_kernel(c_ref, w_ref, b_ref, o_ref):
    a = _silu(c_ref[...])
    o_ref[0] = jnp.dot(a, w_ref[0], preferred_element_type=F32, precision=HIGHEST) + b_ref[0]


def _modulation(cc, w_ada, b_ada):
    depth = w_ada.shape[0]
    return pl.pallas_call(
        _mod_kernel,
        grid=(depth, N_MOD),
        in_specs=[pl.BlockSpec((MOD_ROWS, D_MODEL), lambda l, n: (0, 0)),
                  pl.BlockSpec((1, D_MODEL, D_MODEL), lambda l, n: (l, 0, n)),
                  pl.BlockSpec((1, 1, D_MODEL), lambda l, n: (l, 0, n))],
        out_specs=pl.BlockSpec((1, MOD_ROWS, D_MODEL), lambda l, n: (l, 0, n)),
        out_shape=jax.ShapeDtypeStruct((depth, MOD_ROWS, N_MOD * D_MODEL), F32),
        compiler_params=_cparams(("parallel", "parallel"), 32),
        name="modulation",
    )(cc, w_ada, b_ada.reshape(depth, 1, N_MOD * D_MODEL))


def _inproj_kernel(x_ref, xp_ref, xn_ref, mod_ref, g1_ref, w_ref, cos_ref, sin_ref, gains_ref, ones_ref, ftw_ref,
                   cw_ref, na_ref, dtb_ref,
                   dq_ref, dk_ref, dv_ref, gb_ref, z_ref, gaq_ref, gak_ref, gav_ref, waq_ref, wak_ref, wav_ref,
                   ft_ref, *, lat_blocks, n_ctx):
    j = pl.program_id(1)
    tm = x_ref.shape[1]
    x = jnp.concatenate([x_ref[0], xp_ref[0], xn_ref[0]], 0)
    ms = jnp.mean(x * x, -1, keepdims=True)
    y = x * lax.rsqrt(ms + EPS) * g1_ref[...]
    h_all = (y * (1.0 + mod_ref[0, 1:2, :]) + mod_ref[0, 0:1, :]).astype(BF16)
    h = h_all[:tm]

    def proj(c0, c1):
        return jnp.dot(h, w_ref[:, c0:c1], preferred_element_type=F32)

    z_ref[0] = proj(C_Z, C_GA)
    ones_bd = ones_ref[...]

    qkv = jnp.dot(h_all, w_ref[:, C_QKV:C_Z], preferred_element_type=F32)
    cur = qkv[:tm]
    row = lax.broadcasted_iota(jnp.int32, cur.shape, 0)
    is_ctx = j == lat_blocks
    has_prev = jnp.where((j == 0) | is_ctx, 0.0, 1.0)
    has_next = jnp.where(j >= lat_blocks - 1, 0.0, 1.0)
    last_row = jnp.where(is_ctx, n_ctx - 1, tm - 1)
    prev_row = qkv[tm + SUBLANES - 1:tm + SUBLANES] * has_prev
    next_row = qkv[tm + SUBLANES:tm + SUBLANES + 1] * has_next
    up = jnp.where(row == 0, prev_row, pltpu.roll(cur, 1, 0))
    dn = jnp.where(row == last_row, next_row, pltpu.roll(cur, tm - 1, 0))
    a = _silu(up * cw_ref[0:1, :] + cur * cw_ref[1:2, :] + dn * cw_ref[2:3, :])
    for s in range(2):
        lanes = slice(s * LANES, (s + 1) * LANES)
        tq = a[:, lanes]
        dq_ref[0, :, lanes] = tq * lax.rsqrt(_head_sumsq(tq, ones_bd) + EPS) * HEAD_DIM ** -0.5
        tk = a[:, MIX + s * LANES:MIX + (s + 1) * LANES]
        dk_ref[0, :, lanes] = tk * lax.rsqrt(_head_sumsq(tk, ones_bd) + EPS)
    dv_ref[0] = a[:, 2 * MIX:3 * MIX]

    ab = proj(C_AB, C_END)
    t = ab + dtb_ref[...]
    softplus = jnp.maximum(t, 0.0) + jnp.log(1.0 + jnp.exp(-jnp.abs(t)))
    g = na_ref[...] * softplus
    beta = 1.0 / (1.0 + jnp.exp(-ab))
    lane_c = lax.broadcasted_iota(jnp.int32, (CHUNK, LANES), 1)
    ii = lax.broadcasted_iota(jnp.int32, (CHUNK, CHUNK), 0)
    jj = lax.broadcasted_iota(jnp.int32, (CHUNK, CHUNK), 1)
    prefix = jnp.where(ii >= jj, 1.0, 0.0)
    for c in range(tm // CHUNK):
        rows = slice(c * CHUNK, (c + 1) * CHUNK)
        gc = g[rows]
        fwd = jnp.dot(prefix, gc, preferred_element_type=F32, precision=HIGHEST)
        bwd = fwd[CHUNK - 1:CHUNK, :] - fwd + gc
        gb_ref[0, rows, :] = jnp.where(lane_c < N_HEADS, fwd, jnp.where(lane_c < 2 * N_HEADS, bwd, beta[rows]))

    cos_t = cos_ref[...]
    sin_t = sin_ref[...]
    lane = lax.broadcasted_iota(jnp.int32, cos_t.shape, 1)
    first_half = (lane & (HEAD_DIM // 2)) == 0

    def norm_rope(t, gain_row, scale):
        yv = t * lax.rsqrt(_head_sumsq(t, ones_bd) * (1.0 / HEAD_DIM) + EPS) * gain_row
        partner = jnp.where(first_half, pltpu.roll(yv, LANES - HEAD_DIM // 2, 1), pltpu.roll(yv, HEAD_DIM // 2, 1))
        return (yv * cos_t + partner * sin_t) * scale

    def heads_out(ref, t, n):
        for i in range(n):
            ref[0, i] = t[:, i * HEAD_DIM:(i + 1) * HEAD_DIM].astype(BF16)

    for c0, q_ref, k_ref, v_ref, gq, gk in ((C_GA, gaq_ref, gak_ref, gav_ref, 0, 1),
                                            (C_WA, waq_ref, wak_ref, wav_ref, 2, 3)):
        p = proj(c0, c0 + 2 * MIX)
        scale = HEAD_DIM ** -0.5 * LOG2E
        q = jnp.concatenate([norm_rope(p[:, 0:LANES], gains_ref[gq:gq + 1, :], scale),
                             norm_rope(p[:, LANES:MIX], gains_ref[gq:gq + 1, :], scale)], -1)
        k = norm_rope(p[:, MIX:MIX + LANES], gains_ref[gk:gk + 1, :], 1.0)
        heads_out(q_ref, q, N_HEADS)
        if c0 == C_GA:
            k_ref[0] = k.T.reshape(N_KV, HEAD_DIM, k.shape[0]).astype(BF16)
        else:
            heads_out(k_ref, k, N_KV)
        for i in range(N_KV):
            vh = p[:, MIX + LANES + i * HEAD_DIM:MIX + LANES + (i + 1) * HEAD_DIM]
            v_ref[0, i] = jnp.where(lane < HEAD_DIM, jnp.concatenate([vh, vh], -1),
                                    jnp.where(lane == HEAD_DIM, 1.0, 0.0)).astype(BF16)

    pf = proj(C_FT, C_AB).astype(BF16)
    ft_ref[0] = jnp.dot(pf, ftw_ref[...], preferred_element_type=F32).astype(BF16)


def _in_projection(xa, mod_l, g1, w_in_r, cos_t, sin_t, gains, ones_bd, ftw, conv_w8, neg_a, dt_b, layer, n_lat):
    B, R, _ = xa.shape
    nb = pl.cdiv(R, TM)
    hb = TM // SUBLANES
    const2 = lambda b, j: (0, 0)
    tok = lambda w: pl.BlockSpec((1, TM, w), lambda b, j: (b, j, 0))
    hd = lambda n, w=HEAD_DIM: pl.BlockSpec((1, n, TM, w), lambda b, j: (b, 0, j, 0))
    sds = jax.ShapeDtypeStruct
    heads = lambda n, w=HEAD_DIM: sds((B, n, R, w), BF16)
    return pl.pallas_call(
        functools.partial(_inproj_kernel, lat_blocks=n_lat // TM, n_ctx=R - n_lat),
        grid=(B, nb),
        in_specs=[tok(D_MODEL),
                  pl.BlockSpec((1, SUBLANES, D_MODEL), lambda b, j: (b, jnp.maximum(j * hb - 1, 0), 0)),
                  pl.BlockSpec((1, SUBLANES, D_MODEL),
                               lambda b, j: (b, jnp.minimum((j + 1) * hb, R // SUBLANES - 1), 0)),
                  _mod_spec(n_lat // TM),
                  pl.BlockSpec((1, D_MODEL), const2),
                  _layer_weight_spec(layer, (D_MODEL, C_END)),
                  pl.BlockSpec((TM, LANES), lambda b, j: (j, 0)),
                  pl.BlockSpec((TM, LANES), lambda b, j: (j, 0)),
                  pl.BlockSpec((SUBLANES, LANES), const2),
                  pl.BlockSpec((LANES, LANES), const2),
                  pl.BlockSpec((MIX, 2 * MIX), const2),
                  pl.BlockSpec((SUBLANES, 3 * MIX), const2),
                  pl.BlockSpec((1, LANES), const2),
                  pl.BlockSpec((1, LANES), const2)],
        out_specs=[tok(MIX), tok(MIX), tok(MIX), tok(LANES), tok(MIX),
                   hd(N_HEADS), pl.BlockSpec((1, N_KV, HEAD_DIM, TM), lambda b, j: (b, 0, 0, j)), hd(N_KV, LANES),
                   hd(N_HEADS), hd(N_KV), hd(N_KV, LANES),
                   tok(2 * MIX)],
        out_shape=[sds((B, R, MIX), F32)] * 3 + [sds((B, R, LANES), F32), sds((B, R, MIX), F32),
                   heads(N_HEADS), sds((B, N_KV, HEAD_DIM, R), BF16), heads(N_KV, LANES),
                   heads(N_HEADS), heads(N_KV), heads(N_KV, LANES),
                   sds((B, R, 2 * MIX), BF16)],
        compiler_params=_cparams(("parallel", "parallel"), 48),
        name="in_projection",
    )(xa, xa, xa, mod_l, g1, w_in_r, cos_t, sin_t, gains, ones_bd, ftw, conv_w8, neg_a, dt_b)


def _bdot(a, b, dims):
    return lax.dot_general(a.astype(BF16), b.astype(BF16), dims, preferred_element_type=F32)


_NN = (((2,), (1,)), ((0,), (0,)))
_NT = (((2,), (2,)), ((0,), (0,)))
_TN = (((1,), (1,)), ((0,), (0,)))


def _pair_blockdiag(x):
    first = (lax.broadcasted_iota(jnp.int32, x.shape, 2) & HEAD_DIM) == 0
    return jnp.concatenate([jnp.where(first, x, 0.0), jnp.where(first, 0.0, x)], 1)


def _dnscan_kernel(qf_ref, kf_ref, vf_ref, gbf_ref, gtf_ref, qb_ref, kb_ref, vb_ref, gbb_ref, gtb_ref,
                   of_ref, ob_ref, sf_ref, sb_ref, *, bb):
    i = pl.program_id(1)

    @pl.when(i == 0)
    def _():
        sf_ref[...] = jnp.zeros_like(sf_ref)
        sb_ref[...] = jnp.zeros_like(sb_ref)

    _dn_direction(qf_ref, kf_ref, vf_ref, gbf_ref, gtf_ref, of_ref, sf_ref, bb, 0)
    _dn_direction(qb_ref, kb_ref, vb_ref, gbb_ref, gtb_ref, ob_ref, sb_ref, bb, 1)


def _dn_direction(q_ref, k_ref, v_ref, gb_ref, gt_ref, o_ref, state_ref, bb, d):
    lane2 = lax.broadcasted_iota(jnp.int32, (CHUNK, LANES), 1)
    first2 = (lane2 & HEAD_DIM) == 0
    last = CHUNK - 1 if d == 0 else 0
    gc_cols, b_cols, gc_rows, g_lasts, qs, ks, vs = [], [], [], [], [], [], []
    for b in range(bb):
        gb = gb_ref[b]
        gt = gt_ref[b, 0]
        for m in range(N_HEADS // 2):
            c = d * N_HEADS + 2 * m
            lanes = slice(m * LANES, (m + 1) * LANES)
            qs.append(q_ref[b, :, lanes])
            ks.append(k_ref[b, :, lanes])
            vs.append(v_ref[b, :, lanes])
            col = jnp.where(first2, gb[:, c:c + 1], gb[:, c + 1:c + 2])
            gc_cols.append(col)
            g_lasts.append(col[last:last + 1, :])
            b_cols.append(jnp.where(first2, gb[:, 2 * N_HEADS + c:2 * N_HEADS + c + 1],
                                    gb[:, 2 * N_HEADS + c + 1:2 * N_HEADS + c + 2]))
            gc_rows.append(jnp.concatenate([gt[c:c + 1, :], gt[c + 1:c + 2, :]], 1))
    q = jnp.stack(qs)
    k = jnp.stack(ks)
    v = jnp.stack(vs)
    gc_col = jnp.stack(gc_cols)
    b_col = jnp.stack(b_cols)
    gc_row = jnp.stack(gc_rows)
    g_last = jnp.stack(g_lasts)
    shape = q.shape

    ii = lax.broadcasted_iota(jnp.int32, (CHUNK, LANES), 0)
    jj = lane2 & (HEAD_DIM - 1)
    incl = (ii >= jj) if d == 0 else (ii <= jj)
    strict = (ii > jj) if d == 0 else (ii < jj)
    eye = jnp.where(ii == jj, 1.0, 0.0)
    xor = ii ^ jj
    blocks = [(xor >> sh) == 1 for sh in range(6)]
    bd = _pair_blockdiag

    decay = jnp.exp(jnp.where(incl, gc_col - gc_row, NEG_INF))
    e_gc = jnp.exp(gc_col)
    kb = k * b_col
    kq = _bdot(jnp.concatenate([kb, q], 1), bd(k), _NT)
    a_mat = jnp.where(strict, kq[:, :CHUNK] * decay, 0.0)
    intra = jnp.where(incl, kq[:, CHUNK:] * decay, 0.0)
    t = eye - jnp.where(blocks[0], a_mat, 0.0)
    for mask in blocks[1:-1]:
        t = t - _bdot(t, bd(_bdot(jnp.where(mask, a_mat, 0.0), bd(t), _NN)), _NN)
    rhs = jnp.concatenate([v * b_col, kb * e_gc], 2)
    half = _bdot(t, bd(rhs), _NN)
    sol = half - _bdot(t, bd(_bdot(jnp.where(blocks[-1], a_mat, 0.0), bd(half), _NN)), _NN)
    u, w = sol[:, :, :LANES], sol[:, :, LANES:]
    state = state_ref[...]
    wq = _bdot(jnp.concatenate([w, q * e_gc], 1), bd(state), _NN)
    v_new = u - wq[:, :CHUNK]
    o = wq[:, CHUNK:] + _bdot(intra, bd(v_new), _NN)
    k_dec = k * jnp.exp(g_last - gc_col)
    kv = _bdot(bd(k_dec), bd(v_new), _TN)
    first3 = (lax.broadcasted_iota(jnp.int32, shape, 2) & HEAD_DIM) == 0
    state_ref[...] = state * jnp.exp(g_last) + jnp.where(first3, kv[:, :CHUNK], kv[:, CHUNK:])

    for b in range(bb):
        for m in range(N_HEADS // 2):
            o_ref[b, :, m * LANES:(m + 1) * LANES] = o[b * (N_HEADS // 2) + m]


def _dn_scan(q, k, v, gb, gt, n_lat):
    B, R, _ = q.shape
    nc = R // CHUNK
    nlc = n_lat // CHUNK
    ncc = nc - nlc
    bb = DN_BATCH if B % DN_BATCH == 0 else 1

    def fwd_chunk(i):
        return jnp.where(i < ncc, nlc + i, i - ncc)

    def bwd_chunk(i):
        return jnp.where(i < ncc, nc - 1 - i, nlc - 1 - (i - ncc))

    tokf = lambda w: pl.BlockSpec((bb, CHUNK, w), lambda b, i: (b, fwd_chunk(i), 0))
    tokb = lambda w: pl.BlockSpec((bb, CHUNK, w), lambda b, i: (b, bwd_chunk(i), 0))
    gtf = pl.BlockSpec((bb, 1, N_SCANS, CHUNK), lambda b, i: (b, fwd_chunk(i), 0, 0))
    gtb = pl.BlockSpec((bb, 1, N_SCANS, CHUNK), lambda b, i: (b, bwd_chunk(i), 0, 0))
    sds = jax.ShapeDtypeStruct
    return pl.pallas_call(
        functools.partial(_dnscan_kernel, bb=bb),
        grid=(B // bb, nc),
        in_specs=[tokf(MIX), tokf(MIX), tokf(MIX), tokf(LANES), gtf,
                  tokb(MIX), tokb(MIX), tokb(MIX), tokb(LANES), gtb],
        out_specs=[tokf(MIX), tokb(MIX)],
        out_shape=[sds((B, R, MIX), F32), sds((B, R, MIX), F32)],
        scratch_shapes=[pltpu.VMEM((N_HEADS // 2 * bb, HEAD_DIM, LANES), F32)] * 2,
        compiler_params=_cparams(("parallel", "arbitrary"), 32),
        name="deltanet_scan",
    )(q, k, v, gb, gt, q, k, v, gb, gt)


def _softmax_pv(s_list, v_list, sink):
    m = functools.reduce(jnp.maximum, [s.max(-1, keepdims=True) for s in s_list])
    if sink is not None:
        m = jnp.maximum(m, sink)
    ov = functools.reduce(lambda a, b: a + b, [jnp.dot(jnp.exp2(s - m).astype(BF16), v, preferred_element_type=F32)
                                               for s, v in zip(s_list, v_list)])
    den = ov[:, HEAD_DIM:HEAD_DIM + 1]
    if sink is not None:
        den = den + jnp.exp2(sink - m)
    return ov[:, :HEAD_DIM] / den


def _scores(q, k):
    return lax.dot_general(q, k, (((1,), (1,)), ((), ())), preferred_element_type=F32)


def _store_heads(o_ref, heads, rows):
    o_ref[0, :rows, :] = jnp.concatenate(heads, -1).astype(BF16)


def _gattn_kernel(q_ref, kt_ref, v_ref, o_ref, *, n_lat):
    j = pl.program_id(1)
    tq = q_ref.shape[2]
    n_ctx = kt_ref.shape[3] - n_lat
    group = N_HEADS // N_KV

    def scores_t(q, kt):
        return jnp.dot(q, kt, preferred_element_type=F32)

    @pl.when(j == n_lat // tq)
    def _():
        _store_heads(o_ref, [_softmax_pv([scores_t(q_ref[0, h, :n_ctx, :], kt_ref[0, h // group, :, n_lat:])],
                                         [v_ref[0, h // group, n_lat:, :]], None)
                             for h in range(N_HEADS)], n_ctx)

    @pl.when(j < n_lat // tq)
    def _():
        s = [scores_t(q_ref[0, h], kt_ref[0, h // group]) for h in range(N_HEADS)]
        p = [jnp.exp2(x - x.max(-1, keepdims=True)).astype(BF16) for x in s]
        ov = [jnp.dot(p[h], v_ref[0, h // group], preferred_element_type=F32) for h in range(N_HEADS)]
        _store_heads(o_ref, [x[:, :HEAD_DIM] / x[:, HEAD_DIM:HEAD_DIM + 1] for x in ov], tq)


def _global_attention(q, k, v, n_lat):
    B, _, R, _ = q.shape
    tq = TQ_GLOBAL
    return pl.pallas_call(
        functools.partial(_gattn_kernel, n_lat=n_lat),
        grid=(B, pl.cdiv(R, tq)),
        in_specs=[pl.BlockSpec((1, N_HEADS, tq, HEAD_DIM), lambda b, j: (b, 0, j, 0)),
                  pl.BlockSpec((1, N_KV, HEAD_DIM, R), lambda b, j: (b, 0, 0, 0)),
                  pl.BlockSpec((1, N_KV, R, LANES), lambda b, j: (b, 0, 0, 0))],
        out_specs=pl.BlockSpec((1, tq, N_HEADS * HEAD_DIM), lambda b, j: (b, j, 0)),
        out_shape=jax.ShapeDtypeStruct((B, R, N_HEADS * HEAD_DIM), BF16),
        compiler_params=_cparams(("parallel", "parallel"), 56),
        name="global_attention",
    )(q, k, v)


def _wattn_kernel(sink_ref, bias_ref, q_ref, k_ref, v_ref, o_ref, *, n_lat):
    j = pl.program_id(1)
    tq = q_ref.shape[2]
    n_ctx = k_ref.shape[2] - n_lat
    band = tq + 2 * WINDOW
    group = N_HEADS // N_KV
    sinks = [sink_ref[h] * LOG2E for h in range(N_HEADS)]

    @pl.when(j == n_lat // tq)
    def _():
        _store_heads(o_ref, [_softmax_pv([_scores(q_ref[0, h, :n_ctx, :], k_ref[0, h // group, n_lat:, :])],
                                         [v_ref[0, h // group, n_lat:, :]], sinks[h])
                             for h in range(N_HEADS)], n_ctx)

    @pl.when(j < n_lat // tq)
    def _():
        q0 = j * tq
        start = pl.multiple_of(jnp.clip(q0 - WINDOW, 0, n_lat - band), WINDOW)
        bias = bias_ref[(q0 - start) // WINDOW]
        heads = []
        for h in range(N_HEADS):
            kv = h // group
            q = q_ref[0, h]
            s_loc = _scores(q, k_ref[0, kv, pl.ds(start, band), :]) + bias
            s_ctx = _scores(q, k_ref[0, kv, n_lat:, :])
            heads.append(_softmax_pv([s_loc, s_ctx], [v_ref[0, kv, pl.ds(start, band), :],
                                                      v_ref[0, kv, n_lat:, :]], sinks[h]))
        _store_heads(o_ref, heads, tq)


def _window_bias(tq):
    band = tq + 2 * WINDOW
    off = jnp.arange(3, dtype=jnp.int32)[:, None, None] * WINDOW
    r = jnp.arange(tq, dtype=jnp.int32)[None, :, None]
    c = jnp.arange(band, dtype=jnp.int32)[None, None, :]
    return jnp.where(jnp.abs(r + off - c) <= WINDOW, 0.0, NEG_INF).astype(F32)


def _window_attention(q, k, v, sink, n_lat):
    B, _, R, _ = q.shape
    tq = TQ_WINDOW
    bias = _window_bias(tq)
    return pl.pallas_call(
        functools.partial(_wattn_kernel, n_lat=n_lat),
        grid=(B, pl.cdiv(R, tq)),
        in_specs=[pl.BlockSpec(memory_space=pltpu.SMEM),
                  pl.BlockSpec(bias.shape, lambda b, j: (0, 0, 0), pipeline_mode=pl.Buffered(1)),
                  pl.BlockSpec((1, N_HEADS, tq, HEAD_DIM), lambda b, j: (b, 0, j, 0)),
                  pl.BlockSpec((1, N_KV, R, HEAD_DIM), lambda b, j: (b, 0, 0, 0)),
                  pl.BlockSpec((1, N_KV, R, LANES), lambda b, j: (b, 0, 0, 0))],
        out_specs=pl.BlockSpec((1, tq, N_HEADS * HEAD_DIM), lambda b, j: (b, j, 0)),
        out_shape=jax.ShapeDtypeStruct((B, R, N_HEADS * HEAD_DIM), BF16),
        compiler_params=_cparams(("parallel", "parallel"), 48),
        name="window_attention",
    )(sink, bias, q, k, v)


def _dft_kernel(z_ref, cl_ref, sl_ref, cm_ref, sm_ref, cc_ref, sc_ref, rev_ref, lo_ref, hi_ref, carry_ref, *, n_lat):
    i = pl.program_id(1)
    half = z_ref.shape[2] // 2
    tq = lo_ref.shape[1]
    n_ctx = z_ref.shape[1] - n_lat
    half_blocks = n_lat // (2 * tq)

    @pl.when(i == half_blocks)
    def _():
        for b in range(z_ref.shape[0]):
            z = z_ref[b, n_lat:, :]
            hi_ref[b, :n_ctx, :] = (jnp.dot(cc_ref[...], z[:, :half], preferred_element_type=F32)
                                    + jnp.dot(sc_ref[...], z[:, half:], preferred_element_type=F32)).astype(BF16)

    @pl.when(i == 0)
    def _():
        for b in range(z_ref.shape[0]):
            z = z_ref[b, :n_lat, :]
            carry_ref[b] = (jnp.dot(cm_ref[...], z[:, :half], preferred_element_type=F32)
                            + jnp.dot(sm_ref[...], z[:, half:], preferred_element_type=F32))

    @pl.when(i < half_blocks)
    def _():
        first_row = lax.broadcasted_iota(jnp.int32, (tq, half), 0) == 0
        for b in range(z_ref.shape[0]):
            z = z_ref[b, :n_lat, :]
            p = jnp.dot(cl_ref[...], z[:, :half], preferred_element_type=F32)
            q = jnp.dot(sl_ref[...], z[:, half:], preferred_element_type=F32)
            lo_ref[b] = (p + q).astype(BF16)
            diff = p - q
            mirror = jnp.dot(rev_ref[...], diff.astype(BF16), preferred_element_type=F32)
            hi_ref[b] = jnp.where(first_row, carry_ref[b, 0:1, :], mirror).astype(BF16)
            carry_ref[b, 0:1, :] = diff[0:1]


def _position_dft(z, cos_lat, msin_lat, cos_ctx, msin_ctx, n_lat):
    B, R, _ = z.shape
    n_ctx = R - n_lat
    tq = TQ_DFT
    bb = DFT_BATCH if B % DFT_BATCH == 0 else 1
    hb = n_lat // (2 * tq)
    tile = lambda i: jnp.maximum(hb - 1 - i, 0)
    lat = pl.BlockSpec((tq, n_lat), lambda b, i: (tile(i), 0))
    mid = pl.BlockSpec((SUBLANES, n_lat), lambda b, i: (n_lat // 2 // SUBLANES, 0))
    ctx = pl.BlockSpec((n_ctx, n_ctx), lambda b, i: (0, 0))
    idx = np.arange(tq)
    reverse = jnp.asarray((idx[:, None] + idx[None, :] == tq), F32).astype(BF16)
    return pl.pallas_call(
        functools.partial(_dft_kernel, n_lat=n_lat),
        grid=(B // bb, hb + 1),
        in_specs=[pl.BlockSpec((bb, R, 2 * MIX), lambda b, i: (b, 0, 0), pipeline_mode=pl.Buffered(1)),
                  lat, lat, mid, mid, ctx, ctx,
                  pl.BlockSpec((tq, tq), lambda b, i: (0, 0))],
        out_specs=[pl.BlockSpec((bb, tq, MIX), lambda b, i: (b, tile(i), 0)),
                   pl.BlockSpec((bb, tq, MIX), lambda b, i: (b, i, 0))],
        out_shape=[jax.ShapeDtypeStruct((B, n_lat // 2, MIX), BF16),
                   jax.ShapeDtypeStruct((B, n_lat // 2 + n_ctx, MIX), BF16)],
        scratch_shapes=[pltpu.VMEM((bb, SUBLANES, MIX), F32)],
        compiler_params=_cparams(("parallel", "arbitrary"), 48),
        name="position_dft",
    )(z, cos_lat, msin_lat, cos_lat, msin_lat, cos_ctx, msin_ctx, reverse)


def _block_tail_kernel(of_ref, ob_ref, z_ref, ga_ref, wa_ref, ftl_ref, fth_ref, x_ref, mod_ref, ng_ref, ones_ref,
                       w_ref, g2_ref, wg_ref, wu_ref, wd_ref, o_ref, *, lat_blocks, n_ctx):
    is_ctx = pl.program_id(1) == lat_blocks
    lower = pl.program_id(1) < lat_blocks // 2
    args = (of_ref, ob_ref, z_ref, ga_ref, wa_ref, ftl_ref, fth_ref, lower, x_ref, mod_ref, ng_ref, ones_ref, w_ref,
            g2_ref, wg_ref, wu_ref, wd_ref, o_ref)

    @pl.when(is_ctx)
    def _():
        _block_tail_rows(*args, rows=n_ctx)

    @pl.when(jnp.logical_not(is_ctx))
    def _():
        _block_tail_rows(*args, rows=o_ref.shape[1])


def _block_tail_rows(of_ref, ob_ref, z_ref, ga_ref, wa_ref, ftl_ref, fth_ref, lower, x_ref, mod_ref, ng_ref, ones_ref,
                     w_ref, g2_ref, wg_ref, wu_ref, wd_ref, o_ref, *, rows):
    o = of_ref[0, :rows] + ob_ref[0, :rows]
    z = z_ref[0, :rows]
    ones_bd = ones_ref[...]
    acc = None
    for s in range(2):
        t = o[:, s * LANES:(s + 1) * LANES]
        y = t * lax.rsqrt(_head_sumsq(t, ones_bd) * (1.0 / HEAD_DIM) + EPS) * ng_ref[...]
        y = (y * _silu(z[:, s * LANES:(s + 1) * LANES])).astype(BF16)
        part = jnp.dot(y, w_ref[s * LANES:(s + 1) * LANES, :], preferred_element_type=F32)
        acc = part if acc is None else acc + part
    ft = jnp.where(lower, ftl_ref[0, :rows], fth_ref[0, :rows])
    for n, y in enumerate((ga_ref[0, :rows], wa_ref[0, :rows], ft)):
        acc = acc + jnp.dot(y, w_ref[MIX * (n + 1):MIX * (n + 2), :], preferred_element_type=F32)
    x = x_ref[0, :rows] + mod_ref[0, 2:3, :] * acc

    ms = jnp.mean(x * x, -1, keepdims=True)
    y = x * lax.rsqrt(ms + EPS) * g2_ref[...]
    h = (y * (1.0 + mod_ref[0, 4:5, :]) + mod_ref[0, 3:4, :]).astype(BF16)
    acc = None
    for c in range(FFN_HIDDEN // FFN_CHUNK):
        cs = slice(c * FFN_CHUNK, (c + 1) * FFN_CHUNK)
        g = jnp.dot(h, wg_ref[:, cs], preferred_element_type=F32)
        u = jnp.dot(h, wu_ref[:, cs], preferred_element_type=F32)
        a = (_silu(g) * u).astype(BF16)
        part = jnp.dot(a, wd_ref[cs, :], preferred_element_type=F32)
        acc = part if acc is None else acc + part
    o_ref[0, :rows] = x + mod_ref[0, 5:6, :] * acc


def _block_tail(of, ob, z, y_ga, y_wa, y_ft_lo, y_ft_hi, xa, mod_l, ng, ones_bd, w_out, g2, wg, wu, wd, layer, n_lat,
                rows_out):
    B = xa.shape[0]
    const2 = lambda b, j: (0, 0)
    tok = lambda w: pl.BlockSpec((1, TM_TAIL, w), lambda b, j: (b, j, 0))
    resident = functools.partial(_layer_weight_spec, layer)
    half_blocks = n_lat // TM_TAIL // 2
    return pl.pallas_call(
        functools.partial(_block_tail_kernel, lat_blocks=n_lat // TM_TAIL, n_ctx=of.shape[1] - n_lat),
        grid=(B, pl.cdiv(rows_out, TM_TAIL)),
        in_specs=[tok(MIX)] * 5 + [
            pl.BlockSpec((1, TM_TAIL, MIX), lambda b, j: (b, jnp.minimum(j, half_blocks - 1), 0)),
            pl.BlockSpec((1, TM_TAIL, MIX), lambda b, j: (b, jnp.maximum(j - half_blocks, 0), 0)),
            tok(D_MODEL),
            _mod_spec(n_lat // TM_TAIL),
            pl.BlockSpec((1, LANES), const2),
            pl.BlockSpec((LANES, LANES), const2),
            resident((D_MODEL, D_MODEL)),
            pl.BlockSpec((1, D_MODEL), const2),
            resident((D_MODEL, FFN_HIDDEN)),
            resident((D_MODEL, FFN_HIDDEN)),
            resident((FFN_HIDDEN, D_MODEL))],
        out_specs=tok(D_MODEL),
        out_shape=jax.ShapeDtypeStruct((B, rows_out, D_MODEL), F32),
        compiler_params=_cparams(("parallel", "parallel"), 52),
        name="out_projection_ffn",
    )(of, ob, z, y_ga, y_wa, y_ft_lo, y_ft_hi, xa, mod_l, ng, ones_bd, w_out, g2, wg, wu, wd)


def _rope_tables(n_lat, n_rows):
    t = jnp.arange(n_lat, dtype=jnp.int32)
    row = (t // GRID_W).astype(F32)
    col = (t % GRID_W).astype(F32)
    n_freq = HEAD_DIM // 4
    inv_freq = ROPE_THETA ** (-jnp.arange(n_freq, dtype=F32) / n_freq)
    ang = jnp.concatenate([row[:, None] * inv_freq, col[:, None] * inv_freq], -1)
    cos = jnp.tile(jnp.cos(ang), (1, 4))
    sin = jnp.sin(ang)
    sin = jnp.concatenate([-sin, sin, -sin, sin], -1)
    cos = jnp.concatenate([cos, jnp.ones((n_rows - n_lat, LANES), F32)], 0)
    sin = jnp.concatenate([sin, jnp.zeros((n_rows - n_lat, LANES), F32)], 0)
    return cos, sin


def _dft_matrices(n, n_chan):
    scale = 1.0 / math.sqrt(n * n_chan)
    p = 64 if n % 64 == 0 and n > 64 else 1
    q = n // p
    col = jnp.arange(n, dtype=jnp.int32)[None, :]

    def angles(rows, period):
        return ((jnp.arange(rows, dtype=jnp.int32)[:, None] * col) % period).astype(F32) * (2.0 * math.pi / period)

    ang_a = angles(p, p)
    ang_b = angles(q, n)
    ca, sa = jnp.cos(ang_a)[:, None, :], jnp.sin(ang_a)[:, None, :]
    cb, sb = jnp.cos(ang_b)[None, :, :] * scale, jnp.sin(ang_b)[None, :, :] * scale
    cos = (ca * cb - sa * sb).reshape(n, n)
    msin = (-(sa * cb + ca * sb)).reshape(n, n)
    return cos.astype(BF16), msin.astype(BF16)


def _channel_dft_weights():
    n = HEAD_DIM
    idx = np.arange(n)
    ang = 2.0 * np.pi * ((idx[:, None] * idx[None, :]) % n) / n
    eye = np.eye(4)
    w = np.concatenate([np.kron(eye, np.cos(ang)), np.kron(eye, np.sin(ang))], 1)
    return jnp.asarray(w, F32).astype(BF16)


def kernel(x, c, ctx, c_ctx, norm1_g, norm2_g, w_ada, b_ada, w_in, dn_conv_w, dn_A_log, dn_dt_bias, dn_norm_g,
           ga_q_norm, ga_k_norm, wa_q_norm, wa_k_norm, wa_sink, w_out, w_ffn_gate, w_ffn_up, w_ffn_down):
    B, S, _ = x.shape
    L = ctx.shape[1]
    depth = w_in.shape[0]
    R = S + L
    assert S % TM == 0 and S % TM_TAIL == 0 and L <= TM_TAIL and B <= MOD_ROWS // 2
    assert all(S % t == 0 and L <= t for t in (TQ_WINDOW, TQ_DFT, TQ_GLOBAL))
    assert TM_TAIL == TQ_DFT and S % (2 * TQ_DFT) == 0 and w_in.shape[2] == C_AB + 2 * N_SCANS

    cc = jnp.zeros((MOD_ROWS, D_MODEL), F32).at[:B].set(c).at[MOD_ROWS // 2].set(c_ctx)
    mod = _modulation(cc, w_ada, b_ada).reshape(depth, MOD_ROWS, N_MOD, D_MODEL)

    cos_t, sin_t = _rope_tables(S, pl.cdiv(R, TM) * TM)
    cos_lat, msin_lat = _dft_matrices(S, HEAD_DIM)
    cos_ctx, msin_ctx = _dft_matrices(L, HEAD_DIM)
    ftw = _channel_dft_weights()
    ones_bd = jnp.asarray(np.kron(np.eye(2), np.ones((HEAD_DIM, HEAD_DIM))), F32).astype(BF16)
    tile2 = lambda g: jnp.tile(g, 2)

    n_gate = 2 * N_SCANS
    n_cols = w_in.shape[2]
    w_in_r = jnp.concatenate([w_in[:, :, :C_GA], w_in[:, :, C_GA + n_gate:], w_in[:, :, C_GA:C_GA + n_gate],
                              jnp.zeros((depth, D_MODEL, C_END - n_cols), F32)], 2).astype(BF16)
    w_out_b, w_gate_b, w_up_b, w_down_b = [t.astype(BF16) for t in (w_out, w_ffn_gate, w_ffn_up, w_ffn_down)]

    xa = jnp.concatenate([x, ctx], 1)
    for l in range(depth):
        last = l == depth - 1
        gains = jnp.zeros((SUBLANES, LANES), F32)
        for n, g in enumerate((ga_q_norm[l], ga_k_norm[l], wa_q_norm[l], wa_k_norm[l])):
            gains = gains.at[n].set(tile2(g))
        conv_w8 = jnp.zeros((SUBLANES, 3 * MIX), F32).at[:dn_conv_w.shape[1]].set(dn_conv_w[l])
        neg_a = jnp.zeros((1, LANES), F32).at[0, :N_SCANS].set(-jnp.exp(dn_A_log[l].astype(F32)).reshape(-1))
        dt_b = jnp.zeros((1, LANES), F32).at[0, :N_SCANS].set(dn_dt_bias[l].astype(F32).reshape(-1))
        (dq, dk, dv, gb, z, gaq, gak, gav, waq, wak, wav, ftz) = _in_projection(
            xa, mod[l], norm1_g[l][None, :], w_in_r, cos_t, sin_t, gains, ones_bd, ftw, conv_w8, neg_a, dt_b, l, S)
        gt = gb[:, :, :N_SCANS].reshape(B, R // CHUNK, CHUNK, N_SCANS).transpose(0, 1, 3, 2)
        o_f, o_b = _dn_scan(dq, dk, dv, gb, gt, S)

        y_ga = _global_attention(gaq, gak, gav, S)
        y_wa = _window_attention(waq, wak, wav, wa_sink[l].astype(F32), S)
        y_ft_lo, y_ft_hi = _position_dft(ftz, cos_lat, msin_lat, cos_ctx, msin_ctx, S)

        rows_out = S if last else R
        xa = _block_tail(o_f, o_b, z, y_ga, y_wa, y_ft_lo, y_ft_hi, xa, mod[l], tile2(dn_norm_g[l])[None, :], ones_bd,
                         w_out_b, norm2_g[l][None, :], w_gate_b, w_up_b, w_down_b, l, S, rows_out)
    return xa
```

```python
import functools
import math

import jax
import jax.numpy as jnp
import numpy as np
from jax import lax
from jax.experimental import pallas as pl
from jax.experimental.pallas import tpu as pltpu

F32 = jnp.float32
BF16 = jnp.bfloat16
HIGHEST = lax.Precision.HIGHEST

D_MODEL = 1024
HEAD_DIM = 64
N_HEADS = 4
N_KV = 2
MIX = N_HEADS * HEAD_DIM
GRID_W = 64
CHUNK = 64
WINDOW = 128
ROPE_THETA = 10000.0
FFN_HIDDEN = 2816
N_MOD = 6
EPS = 1e-6
NEG_INF = -1e30

LANES = 128
SUBLANES = 8
N_SCANS = 2 * N_HEADS
TM = 512
TM_TAIL = 512
TQ_GLOBAL = 256
TQ_WINDOW = 512
TQ_DFT = 512
DFT_BATCH = 2
MOD_ROWS = 16
FFN_CHUNK = 256
DN_BATCH = 8
LOG2E = math.log2(math.e)

C_QKV = 0
C_Z = C_QKV + 3 * MIX
C_GA = C_Z + MIX
C_WA = C_GA + 2 * MIX
C_FT = C_WA + 2 * MIX
C_AB = C_FT + MIX
C_END = C_AB + LANES


def _cparams(sem, vmem_mb):
    return pltpu.CompilerParams(dimension_semantics=sem, vmem_limit_bytes=vmem_mb * 1024 * 1024)


def _silu(x):
    return x * (1.0 / (1.0 + jnp.exp(-x)))


def _head_sumsq(t, ones_bd):
    return jnp.dot((t * t).astype(BF16), ones_bd, preferred_element_type=F32)


def _mod_spec(n_lat_blocks):
    ctx_row = MOD_ROWS // 2
    return pl.BlockSpec((1, N_MOD, D_MODEL), lambda b, j: (jnp.where(j == n_lat_blocks, ctx_row, b), 0, 0))


def _layer_weight_spec(layer, shape):
    return pl.BlockSpec((None,) + shape, lambda b, j: (layer, 0, 0), pipeline_mode=pl.Buffered(1))


def _mod_kernel(c_ref, w_ref, b_ref, o_ref):
    a = _silu(c_ref[...])
    o_ref[0] = jnp.dot(a, w_ref[0], preferred_element_type=F32, precision=HIGHEST) + b_ref[0]


def _modulation(cc, w_ada, b_ada):
    depth = w_ada.shape[0]
    return pl.pallas_call(
        _mod_kernel,
        grid=(depth, N_MOD),
        in_specs=[pl.BlockSpec((MOD_ROWS, D_MODEL), lambda l, n: (0, 0)),
                  pl.BlockSpec((1, D_MODEL, D_MODEL), lambda l, n: (l, 0, n)),
                  pl.BlockSpec((1, 1, D_MODEL), lambda l, n: (l, 0, n))],
        out_specs=pl.BlockSpec((1, MOD_ROWS, D_MODEL), lambda l, n: (l, 0, n)),
        out_shape=jax.ShapeDtypeStruct((depth, MOD_ROWS, N_MOD * D_MODEL), F32),
        compiler_params=_cparams(("parallel", "parallel"), 32),
        name="modulation",
    )(cc, w_ada, b_ada.reshape(depth, 1, N_MOD * D_MODEL))


def _inproj_kernel(x_ref, xp_ref, xn_ref, mod_ref, g1_ref, w_ref, cos_ref, sin_ref, gains_ref, ones_ref, ftw_ref,
                   cw_ref, na_ref, dtb_ref,
                   dq_ref, dk_ref, dv_ref, gb_ref, z_ref, gaq_ref, gak_ref, gav_ref, waq_ref, wak_ref, wav_ref,
                   ft_ref, *, lat_blocks, n_ctx):
    j = pl.program_id(1)
    tm = x_ref.shape[1]
    x = jnp.concatenate([x_ref[0], xp_ref[0], xn_ref[0]], 0)
    ms = jnp.mean(x * x, -1, keepdims=True)
    y = x * lax.rsqrt(ms + EPS) * g1_ref[...]
    h_all = (y * (1.0 + mod_ref[0, 1:2, :]) + mod_ref[0, 0:1, :]).astype(BF16)
    h = h_all[:tm]

    def proj(c0, c1):
        return jnp.dot(h, w_ref[:, c0:c1], preferred_element_type=F32)

    z_ref[0] = proj(C_Z, C_GA)
    ones_bd = ones_ref[...]

    qkv = jnp.dot(h_all, w_ref[:, C_QKV:C_Z], preferred_element_type=F32)
    cur = qkv[:tm]
    row = lax.broadcasted_iota(jnp.int32, cur.shape, 0)
    is_ctx = j == lat_blocks
    has_prev = jnp.where((j == 0) | is_ctx, 0.0, 1.0)
    has_next = jnp.where(j >= lat_blocks - 1, 0.0, 1.0)
    last_row = jnp.where(is_ctx, n_ctx - 1, tm - 1)
    prev_row = qkv[tm + SUBLANES - 1:tm + SUBLANES] * has_prev
    next_row = qkv[tm + SUBLANES:tm + SUBLANES + 1] * has_next
    up = jnp.where(row == 0, prev_row, pltpu.roll(cur, 1, 0))
    dn = jnp.where(row == last_row, next_row, pltpu.roll(cur, tm - 1, 0))
    a = _silu(up * cw_ref[0:1, :] + cur * cw_ref[1:2, :] + dn * cw_ref[2:3, :])
    for s in range(2):
        lanes = slice(s * LANES, (s + 1) * LANES)
        tq = a[:, lanes]
        dq_ref[0, :, lanes] = tq * lax.rsqrt(_head_sumsq(tq, ones_bd) + EPS) * HEAD_DIM ** -0.5
        tk = a[:, MIX + s * LANES:MIX + (s + 1) * LANES]
        dk_ref[0, :, lanes] = tk * lax.rsqrt(_head_sumsq(tk, ones_bd) + EPS)
    dv_ref[0] = a[:, 2 * MIX:3 * MIX]

    ab = proj(C_AB, C_END)
    t = ab + dtb_ref[...]
    softplus = jnp.maximum(t, 0.0) + jnp.log(1.0 + jnp.exp(-jnp.abs(t)))
    g = na_ref[...] * softplus
    beta = 1.0 / (1.0 + jnp.exp(-ab))
    lane_c = lax.broadcasted_iota(jnp.int32, (CHUNK, LANES), 1)
    ii = lax.broadcasted_iota(jnp.int32, (CHUNK, CHUNK), 0)
    jj = lax.broadcasted_iota(jnp.int32, (CHUNK, CHUNK), 1)
    prefix = jnp.where(ii >= jj, 1.0, 0.0)
    for c in range(tm // CHUNK):
        rows = slice(c * CHUNK, (c + 1) * CHUNK)
        gc = g[rows]
        fwd = jnp.dot(prefix, gc, preferred_element_type=F32, precision=HIGHEST)
        bwd = fwd[CHUNK - 1:CHUNK, :] - fwd + gc
        gb_ref[0, rows, :] = jnp.where(lane_c < N_HEADS, fwd, jnp.where(lane_c < 2 * N_HEADS, bwd, beta[rows]))

    cos_t = cos_ref[...]
    sin_t = sin_ref[...]
    lane = lax.broadcasted_iota(jnp.int32, cos_t.shape, 1)
    first_half = (lane & (HEAD_DIM // 2)) == 0

    def norm_rope(t, gain_row, scale):
        yv = t * lax.rsqrt(_head_sumsq(t, ones_bd) * (1.0 / HEAD_DIM) + EPS) * gain_row
        partner = jnp.where(first_half, pltpu.roll(yv, LANES - HEAD_DIM // 2, 1), pltpu.roll(yv, HEAD_DIM // 2, 1))
        return (yv * cos_t + partner * sin_t) * scale

    def heads_out(ref, t, n):
        for i in range(n):
            ref[0, i] = t[:, i * HEAD_DIM:(i + 1) * HEAD_DIM].astype(BF16)

    for c0, q_ref, k_ref, v_ref, gq, gk in ((C_GA, gaq_ref, gak_ref, gav_ref, 0, 1),
                                            (C_WA, waq_ref, wak_ref, wav_ref, 2, 3)):
        p = proj(c0, c0 + 2 * MIX)
        scale = HEAD_DIM ** -0.5 * LOG2E
        q = jnp.concatenate([norm_rope(p[:, 0:LANES], gains_ref[gq:gq + 1, :], scale),
                             norm_rope(p[:, LANES:MIX], gains_ref[gq:gq + 1, :], scale)], -1)
        k = norm_rope(p[:, MIX:MIX + LANES], gains_ref[gk:gk + 1, :], 1.0)
        heads_out(q_ref, q, N_HEADS)
        if c0 == C_GA:
            k_ref[0] = k.T.reshape(N_KV, HEAD_DIM, k.shape[0]).astype(BF16)
        else:
            heads_out(k_ref, k, N_KV)
        for i in range(N_KV):
            vh = p[:, MIX + LANES + i * HEAD_DIM:MIX + LANES + (i + 1) * HEAD_DIM]
            v_ref[0, i] = jnp.where(lane < HEAD_DIM, jnp.concatenate([vh, vh], -1),
                                    jnp.where(lane == HEAD_DIM, 1.0, 0.0)).astype(BF16)

    pf = proj(C_FT, C_AB).astype(BF16)
    ft_ref[0] = jnp.dot(pf, ftw_ref[...], preferred_element_type=F32).astype(BF16)


def _in_projection(xa, mod_l, g1, w_in_r, cos_t, sin_t, gains, ones_bd, ftw, conv_w8, neg_a, dt_b, layer, n_lat):
    B, R, _ = xa.shape
    nb = pl.cdiv(R, TM)
    hb = TM // SUBLANES
    const2 = lambda b, j: (0, 0)
    tok = lambda w: pl.BlockSpec((1, TM, w), lambda b, j: (b, j, 0))
    hd = lambda n, w=HEAD_DIM: pl.BlockSpec((1, n, TM, w), lambda b, j: (b, 0, j, 0))
    sds = jax.ShapeDtypeStruct
    heads = lambda n, w=HEAD_DIM: sds((B, n, R, w), BF16)
    return pl.pallas_call(
        functools.partial(_inproj_kernel, lat_blocks=n_lat // TM, n_ctx=R - n_lat),
        grid=(B, nb),
        in_specs=[tok(D_MODEL),
                  pl.BlockSpec((1, SUBLANES, D_MODEL), lambda b, j: (b, jnp.maximum(j * hb - 1, 0), 0)),
                  pl.BlockSpec((1, SUBLANES, D_MODEL),
                               lambda b, j: (b, jnp.minimum((j + 1) * hb, R // SUBLANES - 1), 0)),
                  _mod_spec(n_lat // TM),
                  pl.BlockSpec((1, D_MODEL), const2),
                  _layer_weight_spec(layer, (D_MODEL, C_END)),
                  pl.BlockSpec((TM, LANES), lambda b, j: (j, 0)),
                  pl.BlockSpec((TM, LANES), lambda b, j: (j, 0)),
                  pl.BlockSpec((SUBLANES, LANES), const2),
                  pl.BlockSpec((LANES, LANES), const2),
                  pl.BlockSpec((MIX, 2 * MIX), const2),
                  pl.BlockSpec((SUBLANES, 3 * MIX), const2),
                  pl.BlockSpec((1, LANES), const2),
                  pl.BlockSpec((1, LANES), const2)],
        out_specs=[tok(MIX), tok(MIX), tok(MIX), tok(LANES), tok(MIX),
                   hd(N_HEADS), pl.BlockSpec((1, N_KV, HEAD_DIM, TM), lambda b, j: (b, 0, 0, j)), hd(N_KV, LANES),
                   hd(N_HEADS), hd(N_KV), hd(N_KV, LANES),
                   tok(2 * MIX)],
        out_shape=[sds((B, R, MIX), F32)] * 3 + [sds((B, R, LANES), F32), sds((B, R, MIX), F32),
                   heads(N_HEADS), sds((B, N_KV, HEAD_DIM, R), BF16), heads(N_KV, LANES),
                   heads(N_HEADS), heads(N_KV), heads(N_KV, LANES),
                   sds((B, R, 2 * MIX), BF16)],
        compiler_params=_cparams(("parallel", "parallel"), 48),
        name="in_projection",
    )(xa, xa, xa, mod_l, g1, w_in_r, cos_t, sin_t, gains, ones_bd, ftw, conv_w8, neg_a, dt_b)


def _bdot(a, b, dims):
    return lax.dot_general(a.astype(BF16), b.astype(BF16), dims, preferred_element_type=F32)


_NN = (((2,), (1,)), ((0,), (0,)))
_NT = (((2,), (2,)), ((0,), (0,)))
_TN = (((1,), (1,)), ((0,), (0,)))


def _pair_blockdiag(x):
    first = (lax.broadcasted_iota(jnp.int32, x.shape, 2) & HEAD_DIM) == 0
    return jnp.concatenate([jnp.where(first, x, 0.0), jnp.where(first, 0.0, x)], 1)


def _dnscan_kernel(qf_ref, kf_ref, vf_ref, gbf_ref, gtf_ref, qb_ref, kb_ref, vb_ref, gbb_ref, gtb_ref,
                   of_ref, ob_ref, sf_ref, sb_ref, *, bb):
    i = pl.program_id(1)

    @pl.when(i == 0)
    def _():
        sf_ref[...] = jnp.zeros_like(sf_ref)
        sb_ref[...] = jnp.zeros_like(sb_ref)

    _dn_direction(qf_ref, kf_ref, vf_ref, gbf_ref, gtf_ref, of_ref, sf_ref, bb, 0)
    _dn_direction(qb_ref, kb_ref, vb_ref, gbb_ref, gtb_ref, ob_ref, sb_ref, bb, 1)


def _dn_direction(q_ref, k_ref, v_ref, gb_ref, gt_ref, o_ref, state_ref, bb, d):
    lane2 = lax.broadcasted_iota(jnp.int32, (CHUNK, LANES), 1)
    first2 = (lane2 & HEAD_DIM) == 0
    last = CHUNK - 1 if d == 0 else 0
    gc_cols, b_cols, gc_rows, g_lasts, qs, ks, vs = [], [], [], [], [], [], []
    for b in range(bb):
        gb = gb_ref[b]
        gt = gt_ref[b, 0]
        for m in range(N_HEADS // 2):
            c = d * N_HEADS + 2 * m
            lanes = slice(m * LANES, (m + 1) * LANES)
            qs.append(q_ref[b, :, lanes])
            ks.append(k_ref[b, :, lanes])
            vs.append(v_ref[b, :, lanes])
            col = jnp.where(first2, gb[:, c:c + 1], gb[:, c + 1:c + 2])
            gc_cols.append(col)
            g_lasts.append(col[last:last + 1, :])
            b_cols.append(jnp.where(first2, gb[:, 2 * N_HEADS + c:2 * N_HEADS + c + 1],
                                    gb[:, 2 * N_HEADS + c + 1:2 * N_HEADS + c + 2]))
            gc_rows.append(jnp.concatenate([gt[c:c + 1, :], gt[c + 1:c + 2, :]], 1))
    q = jnp.stack(qs)
    k = jnp.stack(ks)
    v = jnp.stack(vs)
    gc_col = jnp.stack(gc_cols)
    b_col = jnp.stack(b_cols)
    gc_row = jnp.stack(gc_rows)
    g_last = jnp.stack(g_lasts)
    shape = q.shape

    ii = lax.broadcasted_iota(jnp.int32, (CHUNK, LANES), 0)
    jj = lane2 & (HEAD_DIM - 1)
    incl = (ii >= jj) if d == 0 else (ii <= jj)
    strict = (ii > jj) if d == 0 else (ii < jj)
    eye = jnp.where(ii == jj, 1.0, 0.0)
    xor = ii ^ jj
    blocks = [(xor >> sh) == 1 for sh in range(6)]
    bd = _pair_blockdiag

    decay = jnp.exp(jnp.where(incl, gc_col - gc_row, NEG_INF))
    e_gc = jnp.exp(gc_col)
    kb = k * b_col
    kq = _bdot(jnp.concatenate([kb, q], 1), bd(k), _NT)
    a_mat = jnp.where(strict, kq[:, :CHUNK] * decay, 0.0)
    intra = jnp.where(incl, kq[:, CHUNK:] * decay, 0.0)
    t = eye - jnp.where(blocks[0], a_mat, 0.0)
    for mask in blocks[1:-1]:
        t = t - _bdot(t, bd(_bdot(jnp.where(mask, a_mat, 0.0), bd(t), _NN)), _NN)
    rhs = jnp.concatenate([v * b_col, kb * e_gc], 2)
    half = _bdot(t, bd(rhs), _NN)
    sol = half - _bdot(t, bd(_bdot(jnp.where(blocks[-1], a_mat, 0.0), bd(half), _NN)), _NN)
    u, w = sol[:, :, :LANES], sol[:, :, LANES:]
    state = state_ref[...]
    wq = _bdot(jnp.concatenate([w, q * e_gc], 1), bd(state), _NN)
    v_new = u - wq[:, :CHUNK]
    o = wq[:, CHUNK:] + _bdot(intra, bd(v_new), _NN)
    k_dec = k * jnp.exp(g_last - gc_col)
    kv = _bdot(bd(k_dec), bd(v_new), _TN)
    first3 = (lax.broadcasted_iota(jnp.int32, shape, 2) & HEAD_DIM) == 0
    state_ref[...] = state * jnp.exp(g_last) + jnp.where(first3, kv[:, :CHUNK], kv[:, CHUNK:])

    for b in range(bb):
        for m in range(N_HEADS // 2):
            o_ref[b, :, m * LANES:(m + 1) * LANES] = o[b * (N_HEADS // 2) + m]


def _dn_scan(q, k, v, gb, gt, n_lat):
    B, R, _ = q.shape
    nc = R // CHUNK
    nlc = n_lat // CHUNK
    ncc = nc - nlc
    bb = DN_BATCH if B % DN_BATCH == 0 else 1

    def fwd_chunk(i):
        return jnp.where(i < ncc, nlc + i, i - ncc)

    def bwd_chunk(i):
        return jnp.where(i < ncc, nc - 1 - i, nlc - 1 - (i - ncc))

    tokf = lambda w: pl.BlockSpec((bb, CHUNK, w), lambda b, i: (b, fwd_chunk(i), 0))
    tokb = lambda w: pl.BlockSpec((bb, CHUNK, w), lambda b, i: (b, bwd_chunk(i), 0))
    gtf = pl.BlockSpec((bb, 1, N_SCANS, CHUNK), lambda b, i: (b, fwd_chunk(i), 0, 0))
    gtb = pl.BlockSpec((bb, 1, N_SCANS, CHUNK), lambda b, i: (b, bwd_chunk(i), 0, 0))
    sds = jax.ShapeDtypeStruct
    return pl.pallas_call(
        functools.partial(_dnscan_kernel, bb=bb),
        grid=(B // bb, nc),
        in_specs=[tokf(MIX), tokf(MIX), tokf(MIX), tokf(LANES), gtf,
                  tokb(MIX), tokb(MIX), tokb(MIX), tokb(LANES), gtb],
        out_specs=[tokf(MIX), tokb(MIX)],
        out_shape=[sds((B, R, MIX), F32), sds((B, R, MIX), F32)],
        scratch_shapes=[pltpu.VMEM((N_HEADS // 2 * bb, HEAD_DIM, LANES), F32)] * 2,
        compiler_params=_cparams(("parallel", "arbitrary"), 32),
        name="deltanet_scan",
    )(q, k, v, gb, gt, q, k, v, gb, gt)


def _softmax_pv(s_list, v_list, sink):
    m = functools.reduce(jnp.maximum, [s.max(-1, keepdims=True) for s in s_list])
    if sink is not None:
        m = jnp.maximum(m, sink)
    ov = functools.reduce(lambda a, b: a + b, [jnp.dot(jnp.exp2(s - m).astype(BF16), v, preferred_element_type=F32)
                                               for s, v in zip(s_list, v_list)])
    den = ov[:, HEAD_DIM:HEAD_DIM + 1]
    if sink is not None:
        den = den + jnp.exp2(sink - m)
    return ov[:, :HEAD_DIM] / den


def _scores(q, k):
    return lax.dot_general(q, k, (((1,), (1,)), ((), ())), preferred_element_type=F32)


def _store_heads(o_ref, heads, rows):
    o_ref[0, :rows, :] = jnp.concatenate(heads, -1).astype(BF16)


def _gattn_kernel(q_ref, kt_ref, v_ref, o_ref, *, n_lat):
    j = pl.program_id(1)
    tq = q_ref.shape[2]
    n_ctx = kt_ref.shape[3] - n_lat
    group = N_HEADS // N_KV

    def scores_t(q, kt):
        return jnp.dot(q, kt, preferred_element_type=F32)

    @pl.when(j == n_lat // tq)
    def _():
        _store_heads(o_ref, [_softmax_pv([scores_t(q_ref[0, h, :n_ctx, :], kt_ref[0, h // group, :, n_lat:])],
                                         [v_ref[0, h // group, n_lat:, :]], None)
                             for h in range(N_HEADS)], n_ctx)

    @pl.when(j < n_lat // tq)
    def _():
        s = [scores_t(q_ref[0, h], kt_ref[0, h // group]) for h in range(N_HEADS)]
        p = [jnp.exp2(x - x.max(-1, keepdims=True)).astype(BF16) for x in s]
        ov = [jnp.dot(p[h], v_ref[0, h // group], preferred_element_type=F32) for h in range(N_HEADS)]
        _store_heads(o_ref, [x[:, :HEAD_DIM] / x[:, HEAD_DIM:HEAD_DIM + 1] for x in ov], tq)


def _global_attention(q, k, v, n_lat):
    B, _, R, _ = q.shape
    tq = TQ_GLOBAL
    return pl.pallas_call(
        functools.partial(_gattn_kernel, n_lat=n_lat),
        grid=(B, pl.cdiv(R, tq)),
        in_specs=[pl.BlockSpec((1, N_HEADS, tq, HEAD_DIM), lambda b, j: (b, 0, j, 0)),
                  pl.BlockSpec((1, N_KV, HEAD_DIM, R), lambda b, j: (b, 0, 0, 0)),
                  pl.BlockSpec((1, N_KV, R, LANES), lambda b, j: (b, 0, 0, 0))],
        out_specs=pl.BlockSpec((1, tq, N_HEADS * HEAD_DIM), lambda b, j: (b, j, 0)),
        out_shape=jax.ShapeDtypeStruct((B, R, N_HEADS * HEAD_DIM), BF16),
        compiler_params=_cparams(("parallel", "parallel"), 56),
        name="global_attention",
    )(q, k, v)


def _wattn_kernel(sink_ref, bias_ref, q_ref, k_ref, v_ref, o_ref, *, n_lat):
    j = pl.program_id(1)
    tq = q_ref.shape[2]
    n_ctx = k_ref.shape[2] - n_lat
    band = tq + 2 * WINDOW
    group = N_HEADS // N_KV
    sinks = [sink_ref[h] * LOG2E for h in range(N_HEADS)]

    @pl.when(j == n_lat // tq)
    def _():
        _store_heads(o_ref, [_softmax_pv([_scores(q_ref[0, h, :n_ctx, :], k_ref[0, h // group, n_lat:, :])],
                                         [v_ref[0, h // group, n_lat:, :]], sinks[h])
                             for h in range(N_HEADS)], n_ctx)

    @pl.when(j < n_lat // tq)
    def _():
        q0 = j * tq
        start = pl.multiple_of(jnp.clip(q0 - WINDOW, 0, n_lat - band), WINDOW)
        bias = bias_ref[(q0 - start) // WINDOW]
        scores = [[_scores(q_ref[0, h], k_ref[0, h // group, pl.ds(start, band), :]) + bias,
                   _scores(q_ref[0, h], k_ref[0, h // group, n_lat:, :])] for h in range(N_HEADS)]
        _store_heads(o_ref, [_softmax_pv(scores[h], [v_ref[0, h // group, pl.ds(start, band), :],
                                                     v_ref[0, h // group, n_lat:, :]], sinks[h])
                             for h in range(N_HEADS)], tq)


def _window_bias(tq):
    band = tq + 2 * WINDOW
    off = jnp.arange(3, dtype=jnp.int32)[:, None, None] * WINDOW
    r = jnp.arange(tq, dtype=jnp.int32)[None, :, None]
    c = jnp.arange(band, dtype=jnp.int32)[None, None, :]
    return jnp.where(jnp.abs(r + off - c) <= WINDOW, 0.0, NEG_INF).astype(F32)


def _window_attention(q, k, v, sink, n_lat):
    B, _, R, _ = q.shape
    tq = TQ_WINDOW
    bias = _window_bias(tq)
    return pl.pallas_call(
        functools.partial(_wattn_kernel, n_lat=n_lat),
        grid=(B, pl.cdiv(R, tq)),
        in_specs=[pl.BlockSpec(memory_space=pltpu.SMEM),
                  pl.BlockSpec(bias.shape, lambda b, j: (0, 0, 0), pipeline_mode=pl.Buffered(1)),
                  pl.BlockSpec((1, N_HEADS, tq, HEAD_DIM), lambda b, j: (b, 0, j, 0)),
                  pl.BlockSpec((1, N_KV, R, HEAD_DIM), lambda b, j: (b, 0, 0, 0)),
                  pl.BlockSpec((1, N_KV, R, LANES), lambda b, j: (b, 0, 0, 0))],
        out_specs=pl.BlockSpec((1, tq, N_HEADS * HEAD_DIM), lambda b, j: (b, j, 0)),
        out_shape=jax.ShapeDtypeStruct((B, R, N_HEADS * HEAD_DIM), BF16),
        compiler_params=_cparams(("parallel", "parallel"), 48),
        name="window_attention",
    )(sink, bias, q, k, v)


def _dft_kernel(z_ref, cl_ref, sl_ref, cm_ref, sm_ref, cc_ref, sc_ref, rev_ref, lo_ref, hi_ref, carry_ref, *, n_lat):
    i = pl.program_id(1)
    half = z_ref.shape[2] // 2
    tq = lo_ref.shape[1]
    n_ctx = z_ref.shape[1] - n_lat
    half_blocks = n_lat // (2 * tq)

    @pl.when(i == half_blocks)
    def _():
        for b in range(z_ref.shape[0]):
            z = z_ref[b, n_lat:, :]
            hi_ref[b, :n_ctx, :] = (jnp.dot(cc_ref[...], z[:, :half], preferred_element_type=F32)
                                    + jnp.dot(sc_ref[...], z[:, half:], preferred_element_type=F32)).astype(BF16)

    @pl.when(i == 0)
    def _():
        for b in range(z_ref.shape[0]):
            z = z_ref[b, :n_lat, :]
            carry_ref[b] = (jnp.dot(cm_ref[...], z[:, :half], preferred_element_type=F32)
                            + jnp.dot(sm_ref[...], z[:, half:], preferred_element_type=F32))

    @pl.when(i < half_blocks)
    def _():
        first_row = lax.broadcasted_iota(jnp.int32, (tq, half), 0) == 0
        for b in range(z_ref.shape[0]):
            z = z_ref[b, :n_lat, :]
            p = jnp.dot(cl_ref[...], z[:, :half], preferred_element_type=F32)
            q = jnp.dot(sl_ref[...], z[:, half:], preferred_element_type=F32)
            lo_ref[b] = (p + q).astype(BF16)
            diff = p - q
            mirror = jnp.dot(rev_ref[...], diff.astype(BF16), preferred_element_type=F32)
            hi_ref[b] = jnp.where(first_row, carry_ref[b, 0:1, :], mirror).astype(BF16)
            carry_ref[b, 0:1, :] = diff[0:1]


def _position_dft(z, cos_lat, msin_lat, cos_ctx, msin_ctx, n_lat):
    B, R, _ = z.shape
    n_ctx = R - n_lat
    tq = TQ_DFT
    bb = DFT_BATCH if B % DFT_BATCH == 0 else 1
    hb = n_lat // (2 * tq)
    tile = lambda i: jnp.maximum(hb - 1 - i, 0)
    lat = pl.BlockSpec((tq, n_lat), lambda b, i: (tile(i), 0))
    mid = pl.BlockSpec((SUBLANES, n_lat), lambda b, i: (n_lat // 2 // SUBLANES, 0))
    ctx = pl.BlockSpec((n_ctx, n_ctx), lambda b, i: (0, 0))
    idx = np.arange(tq)
    reverse = jnp.asarray((idx[:, None] + idx[None, :] == tq), F32).astype(BF16)
    return pl.pallas_call(
        functools.partial(_dft_kernel, n_lat=n_lat),
        grid=(B // bb, hb + 1),
        in_specs=[pl.BlockSpec((bb, R, 2 * MIX), lambda b, i: (b, 0, 0), pipeline_mode=pl.Buffered(1)),
                  lat, lat, mid, mid, ctx, ctx,
                  pl.BlockSpec((tq, tq), lambda b, i: (0, 0))],
        out_specs=[pl.BlockSpec((bb, tq, MIX), lambda b, i: (b, tile(i), 0)),
                   pl.BlockSpec((bb, tq, MIX), lambda b, i: (b, i, 0))],
        out_shape=[jax.ShapeDtypeStruct((B, n_lat // 2, MIX), BF16),
                   jax.ShapeDtypeStruct((B, n_lat // 2 + n_ctx, MIX), BF16)],
        scratch_shapes=[pltpu.VMEM((bb, SUBLANES, MIX), F32)],
        compiler_params=_cparams(("parallel", "arbitrary"), 48),
        name="position_dft",
    )(z, cos_lat, msin_lat, cos_lat, msin_lat, cos_ctx, msin_ctx, reverse)


def _block_tail_kernel(of_ref, ob_ref, z_ref, ga_ref, wa_ref, ftl_ref, fth_ref, x_ref, mod_ref, ng_ref, ones_ref,
                       w_ref, g2_ref, wg_ref, wu_ref, wd_ref, o_ref, *, lat_blocks, n_ctx):
    is_ctx = pl.program_id(1) == lat_blocks
    lower = pl.program_id(1) < lat_blocks // 2
    args = (of_ref, ob_ref, z_ref, ga_ref, wa_ref, ftl_ref, fth_ref, lower, x_ref, mod_ref, ng_ref, ones_ref, w_ref,
            g2_ref, wg_ref, wu_ref, wd_ref, o_ref)

    @pl.when(is_ctx)
    def _():
        _block_tail_rows(*args, rows=n_ctx)

    @pl.when(jnp.logical_not(is_ctx))
    def _():
        _block_tail_rows(*args, rows=o_ref.shape[1])


def _block_tail_rows(of_ref, ob_ref, z_ref, ga_ref, wa_ref, ftl_ref, fth_ref, lower, x_ref, mod_ref, ng_ref, ones_ref,
                     w_ref, g2_ref, wg_ref, wu_ref, wd_ref, o_ref, *, rows):
    o = of_ref[0, :rows] + ob_ref[0, :rows]
    z = z_ref[0, :rows]
    ones_bd = ones_ref[...]
    acc = None
    for s in range(2):
        t = o[:, s * LANES:(s + 1) * LANES]
        y = t * lax.rsqrt(_head_sumsq(t, ones_bd) * (1.0 / HEAD_DIM) + EPS) * ng_ref[...]
        y = (y * _silu(z[:, s * LANES:(s + 1) * LANES])).astype(BF16)
        part = jnp.dot(y, w_ref[s * LANES:(s + 1) * LANES, :], preferred_element_type=F32)
        acc = part if acc is None else acc + part
    ft = jnp.where(lower, ftl_ref[0, :rows], fth_ref[0, :rows])
    for n, y in enumerate((ga_ref[0, :rows], wa_ref[0, :rows], ft)):
        acc = acc + jnp.dot(y, w_ref[MIX * (n + 1):MIX * (n + 2), :], preferred_element_type=F32)
    x = x_ref[0, :rows] + mod_ref[0, 2:3, :] * acc

    ms = jnp.mean(x * x, -1, keepdims=True)
    y = x * lax.rsqrt(ms + EPS) * g2_ref[...]
    h = (y * (1.0 + mod_ref[0, 4:5, :]) + mod_ref[0, 3:4, :]).astype(BF16)
    acc = None
    for c in range(FFN_HIDDEN // FFN_CHUNK):
        cs = slice(c * FFN_CHUNK, (c + 1) * FFN_CHUNK)
        g = jnp.dot(h, wg_ref[:, cs], preferred_element_type=F32)
        u = jnp.dot(h, wu_ref[:, cs], preferred_element_type=F32)
        a = (_silu(g) * u).astype(BF16)
        part = jnp.dot(a, wd_ref[cs, :], preferred_element_type=F32)
        acc = part if acc is None else acc + part
    o_ref[0, :rows] = x + mod_ref[0, 5:6, :] * acc


def _block_tail(of, ob, z, y_ga, y_wa, y_ft_lo, y_ft_hi, xa, mod_l, ng, ones_bd, w_out, g2, wg, wu, wd, layer, n_lat,
                rows_out):
    B = xa.shape[0]
    const2 = lambda b, j: (0, 0)
    tok = lambda w: pl.BlockSpec((1, TM_TAIL, w), lambda b, j: (b, j, 0))
    resident = functools.partial(_layer_weight_spec, layer)
    half_blocks = n_lat // TM_TAIL // 2
    return pl.pallas_call(
        functools.partial(_block_tail_kernel, lat_blocks=n_lat // TM_TAIL, n_ctx=of.shape[1] - n_lat),
        grid=(B, pl.cdiv(rows_out, TM_TAIL)),
        in_specs=[tok(MIX)] * 5 + [
            pl.BlockSpec((1, TM_TAIL, MIX), lambda b, j: (b, jnp.minimum(j, half_blocks - 1), 0)),
            pl.BlockSpec((1, TM_TAIL, MIX), lambda b, j: (b, jnp.maximum(j - half_blocks, 0), 0)),
            tok(D_MODEL),
            _mod_spec(n_lat // TM_TAIL),
            pl.BlockSpec((1, LANES), const2),
            pl.BlockSpec((LANES, LANES), const2),
            resident((D_MODEL, D_MODEL)),
            pl.BlockSpec((1, D_MODEL), const2),
            resident((D_MODEL, FFN_HIDDEN)),
            resident((D_MODEL, FFN_HIDDEN)),
            resident((FFN_HIDDEN, D_MODEL))],
        out_specs=tok(D_MODEL),
        out_shape=jax.ShapeDtypeStruct((B, rows_out, D_MODEL), F32),
        compiler_params=_cparams(("parallel", "parallel"), 52),
        name="out_projection_ffn",
    )(of, ob, z, y_ga, y_wa, y_ft_lo, y_ft_hi, xa, mod_l, ng, ones_bd, w_out, g2, wg, wu, wd)


def _rope_tables(n_lat, n_rows):
    t = jnp.arange(n_lat, dtype=jnp.int32)
    row = (t // GRID_W).astype(F32)
    col = (t % GRID_W).astype(F32)
    n_freq = HEAD_DIM // 4
    inv_freq = ROPE_THETA ** (-jnp.arange(n_freq, dtype=F32) / n_freq)
    ang = jnp.concatenate([row[:, None] * inv_freq, col[:, None] * inv_freq], -1)
    cos = jnp.tile(jnp.cos(ang), (1, 4))
    sin = jnp.sin(ang)
    sin = jnp.concatenate([-sin, sin, -sin, sin], -1)
    cos = jnp.concatenate([cos, jnp.ones((n_rows - n_lat, LANES), F32)], 0)
    sin = jnp.concatenate([sin, jnp.zeros((n_rows - n_lat, LANES), F32)], 0)
    return cos, sin


def _dft_matrices(n, n_chan):
    scale = 1.0 / math.sqrt(n * n_chan)
    p = 64 if n % 64 == 0 and n > 64 else 1
    q = n // p
    col = jnp.arange(n, dtype=jnp.int32)[None, :]

    def angles(rows, period):
        return ((jnp.arange(rows, dtype=jnp.int32)[:, None] * col) % period).astype(F32) * (2.0 * math.pi / period)

    ang_a = angles(p, p)
    ang_b = angles(q, n)
    ca, sa = jnp.cos(ang_a)[:, None, :], jnp.sin(ang_a)[:, None, :]
    cb, sb = jnp.cos(ang_b)[None, :, :] * scale, jnp.sin(ang_b)[None, :, :] * scale
    cos = (ca * cb - sa * sb).reshape(n, n)
    msin = (-(sa * cb + ca * sb)).reshape(n, n)
    return cos.astype(BF16), msin.astype(BF16)


def _channel_dft_weights():
    n = HEAD_DIM
    idx = np.arange(n)
    ang = 2.0 * np.pi * ((idx[:, None] * idx[None, :]) % n) / n
    eye = np.eye(4)
    w = np.concatenate([np.kron(eye, np.cos(ang)), np.kron(eye, np.sin(ang))], 1)
    return jnp.asarray(w, F32).astype(BF16)


def kernel(x, c, ctx, c_ctx, norm1_g, norm2_g, w_ada, b_ada, w_in, dn_conv_w, dn_A_log, dn_dt_bias, dn_norm_g,
           ga_q_norm, ga_k_norm, wa_q_norm, wa_k_norm, wa_sink, w_out, w_ffn_gate, w_ffn_up, w_ffn_down):
    B, S, _ = x.shape
    L = ctx.shape[1]
    depth = w_in.shape[0]
    R = S + L
    assert S % TM == 0 and S % TM_TAIL == 0 and L <= TM_TAIL and B <= MOD_ROWS // 2
    assert all(S % t == 0 and L <= t for t in (TQ_WINDOW, TQ_DFT, TQ_GLOBAL))
    assert TM_TAIL == TQ_DFT and S % (2 * TQ_DFT) == 0 and w_in.shape[2] == C_AB + 2 * N_SCANS

    cc = jnp.zeros((MOD_ROWS, D_MODEL), F32).at[:B].set(c).at[MOD_ROWS // 2].set(c_ctx)
    mod = _modulation(cc, w_ada, b_ada).reshape(depth, MOD_ROWS, N_MOD, D_MODEL)

    cos_t, sin_t = _rope_tables(S, pl.cdiv(R, TM) * TM)
    cos_lat, msin_lat = _dft_matrices(S, HEAD_DIM)
    cos_ctx, msin_ctx = _dft_matrices(L, HEAD_DIM)
    ftw = _channel_dft_weights()
    ones_bd = jnp.asarray(np.kron(np.eye(2), np.ones((HEAD_DIM, HEAD_DIM))), F32).astype(BF16)
    tile2 = lambda g: jnp.tile(g, 2)

    n_gate = 2 * N_SCANS
    n_cols = w_in.shape[2]
    w_in_r = jnp.concatenate([w_in[:, :, :C_GA], w_in[:, :, C_GA + n_gate:], w_in[:, :, C_GA:C_GA + n_gate],
                              jnp.zeros((depth, D_MODEL, C_END - n_cols), F32)], 2).astype(BF16)
    w_out_b, w_gate_b, w_up_b, w_down_b = [t.astype(BF16) for t in (w_out, w_ffn_gate, w_ffn_up, w_ffn_down)]

    xa = jnp.concatenate([x, ctx], 1)
    for l in range(depth):
        last = l == depth - 1
        gains = jnp.zeros((SUBLANES, LANES), F32)
        for n, g in enumerate((ga_q_norm[l], ga_k_norm[l], wa_q_norm[l], wa_k_norm[l])):
            gains = gains.at[n].set(tile2(g))
        conv_w8 = jnp.zeros((SUBLANES, 3 * MIX), F32).at[:dn_conv_w.shape[1]].set(dn_conv_w[l])
        neg_a = jnp.zeros((1, LANES), F32).at[0, :N_SCANS].set(-jnp.exp(dn_A_log[l].astype(F32)).reshape(-1))
        dt_b = jnp.zeros((1, LANES), F32).at[0, :N_SCANS].set(dn_dt_bias[l].astype(F32).reshape(-1))
        (dq, dk, dv, gb, z, gaq, gak, gav, waq, wak, wav, ftz) = _in_projection(
            xa, mod[l], norm1_g[l][None, :], w_in_r, cos_t, sin_t, gains, ones_bd, ftw, conv_w8, neg_a, dt_b, l, S)
        gt = gb[:, :, :N_SCANS].reshape(B, R // CHUNK, CHUNK, N_SCANS).transpose(0, 1, 3, 2)
        o_f, o_b = _dn_scan(dq, dk, dv, gb, gt, S)

        y_ga = _global_attention(gaq, gak, gav, S)
        y_wa = _window_attention(waq, wak, wav, wa_sink[l].astype(F32), S)
        y_ft_lo, y_ft_hi = _position_dft(ftz, cos_lat, msin_lat, cos_ctx, msin_ctx, S)

        rows_out = S if last else R
        xa = _block_tail(o_f, o_b, z, y_ga, y_wa, y_ft_lo, y_ft_hi, xa, mod[l], tile2(dn_norm_g[l])[None, :], ones_bd,
                         w_out_b, norm2_g[l][None, :], w_gate_b, w_up_b, w_down_b, l, S, rows_out)
    return xa
```
